```python
import jax, jax.numpy as jnp
from jax import lax
import numpy as np

D_MODEL = 1024
BATCH = 8
SEQ = 8192
DEPTH = 2
DEC_BATCH = 8
DEC_SEQ = 4096
PAST_LEN = 128

MLA_HEADS = 16
Q_LORA = 256
KV_LORA = 128
QK_NOPE = 64
QK_ROPE = 32
V_DIM = 64
ROPE_THETA = 10000.0
Q_BLOCK = 128
NA_HEADS = 16
NA_HEAD_DIM = D_MODEL // NA_HEADS
GRID_W = 64
WIN_R = 8
WIN_C = 16
FFN_HIDDEN = -(-8 * D_MODEL // (3 * 256)) * 256
N_MLA = (DEPTH + 1) // 2
N_NA = DEPTH // 2
N_MOD = 6
RMS_EPS = 1e-6
NEG_INF = -1e30

kernel_name = 'hybrid_mla_natten_encoder'


def _rmsnorm(x, g):
    xf = x.astype(jnp.float32)
    y = xf * lax.rsqrt(jnp.mean(xf * xf, axis=-1, keepdims=True) + RMS_EPS)
    return (y * g.astype(jnp.float32)).astype(x.dtype)


def _rope_tables(seq_len):
    inv_freq = 1.0 / (ROPE_THETA ** (jnp.arange(0, QK_ROPE, 2, dtype=jnp.float32) / QK_ROPE))
    ang = jnp.arange(seq_len, dtype=jnp.float32)[:, None] * inv_freq[None, :]
    return jnp.cos(ang), jnp.sin(ang)


def _apply_rope(x, cos, sin):
    cos = cos.astype(x.dtype)
    sin = sin.astype(x.dtype)
    x1, x2 = jnp.split(x, 2, axis=-1)
    return jnp.concatenate([x1 * cos - x2 * sin, x2 * cos + x1 * sin], axis=-1)


def _mla(h, w_dkv, q_norm, kv_norm, w_uq, w_ukv, w_o):
    B, S, _ = h.shape
    lat = h @ w_dkv
    cq = _rmsnorm(lat[..., :Q_LORA], q_norm)
    ckv = _rmsnorm(lat[..., Q_LORA:Q_LORA + KV_LORA], kv_norm)
    k_rope = lat[..., Q_LORA + KV_LORA:]
    q = (cq @ w_uq).reshape(B, S, MLA_HEADS, QK_NOPE + QK_ROPE)
    kv = (ckv @ w_ukv).reshape(B, S, MLA_HEADS, QK_NOPE + V_DIM)
    k_nope, v = kv[..., :QK_NOPE], kv[..., QK_NOPE:]
    cos, sin = _rope_tables(S)
    scale = (QK_NOPE + QK_ROPE) ** -0.5
    q_nope = q[..., :QK_NOPE] * scale
    q_rope = _apply_rope(q[..., QK_NOPE:], cos[:, None, :], sin[:, None, :]) * scale
    k_rope = _apply_rope(k_rope, cos, sin)
    nb = S // Q_BLOCK
    qn = q_nope.reshape(B, nb, Q_BLOCK, MLA_HEADS, QK_NOPE).transpose(1, 0, 2, 3, 4)
    qr = q_rope.reshape(B, nb, Q_BLOCK, MLA_HEADS, QK_ROPE).transpose(1, 0, 2, 3, 4)

    def attend(blk):
        qn_b, qr_b = blk
        s = (jnp.einsum('bqhd,bkhd->bhqk', qn_b, k_nope, preferred_element_type=jnp.float32)
             + jnp.einsum('bqhr,bkr->bhqk', qr_b, k_rope, preferred_element_type=jnp.float32))
        p = jax.nn.softmax(s, axis=-1).astype(v.dtype)
        return jnp.einsum('bhqk,bkhd->bqhd', p, v)

    o = lax.map(attend, (qn, qr))
    o = o.transpose(1, 0, 2, 3, 4).reshape(B, S, MLA_HEADS * V_DIM)
    return o @ w_o


def _neighbourhood_attention(h, w_qkv, rpb, w_o):
    B, S, _ = h.shape
    rows = S // GRID_W
    win_r = min(WIN_R, rows)
    qkv = (h @ w_qkv).reshape(B, rows, GRID_W, 3, NA_HEADS, NA_HEAD_DIM)
    q = qkv[:, :, :, 0] * (NA_HEAD_DIM ** -0.5)
    k = qkv[:, :, :, 1]
    v = qkv[:, :, :, 2]
    c_idx = jnp.arange(GRID_W)
    c_start = jnp.clip(c_idx - WIN_C // 2, 0, GRID_W - WIN_C)
    col_valid = (c_idx[None, :] >= c_start[:, None]) & (c_idx[None, :] < c_start[:, None] + WIN_C)
    dc_idx = jnp.clip(c_idx[None, :] - c_idx[:, None] + WIN_C - 1, 0, 2 * WIN_C - 2)
    col_bias = rpb.astype(jnp.float32)[:, :, dc_idx]
    col_bias = jnp.where(col_valid[None, None], col_bias, NEG_INF)

    def row_block(r):
        r_start = jnp.clip(r - win_r // 2, 0, rows - win_r)
        q_r = lax.dynamic_index_in_dim(q, r, axis=1, keepdims=False)
        k_band = lax.dynamic_slice_in_dim(k, r_start, win_r, axis=1)
        v_band = lax.dynamic_slice_in_dim(v, r_start, win_r, axis=1)
        dr_idx = r_start + jnp.arange(win_r) - r + WIN_R - 1
        bias = col_bias[:, dr_idx].transpose(0, 2, 1, 3)
        s = jnp.einsum('bqhd,bjkhd->bhqjk', q_r, k_band, preferred_element_type=jnp.float32) + bias[None]
        p = jax.nn.softmax(s.reshape(B, NA_HEADS, GRID_W, win_r * GRID_W), axis=-1)
        p = p.reshape(B, NA_HEADS, GRID_W, win_r, GRID_W).astype(v.dtype)
        return jnp.einsum('bhqjk,bjkhd->bqhd', p, v_band)

    o = lax.map(row_block, jnp.arange(rows, dtype=jnp.int32))
    o = o.transpose(1, 0, 2, 3, 4).reshape(B, S, NA_HEADS * NA_HEAD_DIM)
    return o @ w_o


def _swiglu(h, w_gu, w_down):
    gu = h @ w_gu
    g, u = gu[..., :FFN_HIDDEN], gu[..., FFN_HIDDEN:]
    return (jax.nn.silu(g) * u) @ w_down


def _trunk(x, c, ada_w, ada_b, norm_pre_mix, norm_post_mix, norm_pre_ffn, norm_post_ffn,
           mla_w_dkv, mla_q_norm, mla_kv_norm, mla_w_uq, mla_w_ukv, mla_w_o,
           na_w_qkv, na_rpb, na_w_o, ffn_w_gu, ffn_w_down):
    B = x.shape[0]
    c_act = jax.nn.silu(c)
    for i in range(DEPTH):
        mod = (c_act @ ada_w[i] + ada_b[i]).astype(x.dtype).reshape(B, N_MOD, 1, D_MODEL)
        shift_m, scale_m, gate_m = mod[:, 0], mod[:, 1], mod[:, 2]
        shift_f, scale_f, gate_f = mod[:, 3], mod[:, 4], mod[:, 5]
        h = _rmsnorm(x, norm_pre_mix[i]) * (1 + scale_m) + shift_m
        j = i // 2
        if i % 2 == 0:
            h = _mla(h, mla_w_dkv[j], mla_q_norm[j], mla_kv_norm[j], mla_w_uq[j], mla_w_ukv[j], mla_w_o[j])
        else:
            h = _neighbourhood_attention(h, na_w_qkv[j], na_rpb[j], na_w_o[j])
        x = x + gate_m * _rmsnorm(h, norm_post_mix[i])
        h = _rmsnorm(x, norm_pre_ffn[i]) * (1 + scale_f) + shift_f
        h = _swiglu(h, ffn_w_gu[i], ffn_w_down[i])
        x = x + gate_f * _rmsnorm(h, norm_post_ffn[i])
    return x


def _w(k, shape, fan_in, gain=1.0):
    return (gain * fan_in ** -0.5) * jax.random.normal(k, shape, dtype=jnp.float32)


def _gain(k, shape):
    return 1.0 + 0.05 * jax.random.normal(k, shape, dtype=jnp.float32)


def setup_inputs(seed: int = 0) -> dict:
    key = jax.random.key(seed)
    ks = jax.random.split(key, 24)
    D = D_MODEL
    return {
        'x_prompt': jax.random.normal(ks[0], (BATCH, SEQ, D), dtype=jnp.float32),
        'x_sample': jax.random.normal(ks[1], (DEC_BATCH, DEC_SEQ, D), dtype=jnp.float32),
        'c_prompt': jax.random.normal(ks[2], (BATCH, D), dtype=jnp.float32),
        'c_sample': jax.random.normal(ks[3], (DEC_BATCH, D), dtype=jnp.float32),
        'ada_w': _w(ks[4], (DEPTH, D, N_MOD * D), D, 0.5),
        'ada_b': 0.02 * jax.random.normal(ks[5], (DEPTH, N_MOD * D), dtype=jnp.float32),
        'norm_pre_mix': _gain(ks[6], (DEPTH, D)),
        'norm_post_mix': _gain(ks[7], (DEPTH, D)),
        'norm_pre_ffn': _gain(ks[8], (DEPTH, D)),
        'norm_post_ffn': _gain(ks[9], (DEPTH, D)),
        'mla_w_dkv': _w(ks[10], (N_MLA, D, Q_LORA + KV_LORA + QK_ROPE), D),
        'mla_q_norm': _gain(ks[11], (N_MLA, Q_LORA)),
        'mla_kv_norm': _gain(ks[12], (N_MLA, KV_LORA)),
        'mla_w_uq': _w(ks[13], (N_MLA, Q_LORA, MLA_HEADS * (QK_NOPE + QK_ROPE)), Q_LORA),
        'mla_w_ukv': _w(ks[14], (N_MLA, KV_LORA, MLA_HEADS * (QK_NOPE + V_DIM)), KV_LORA),
        'mla_w_o': _w(ks[15], (N_MLA, MLA_HEADS * V_DIM, D), MLA_HEADS * V_DIM),
        'na_w_qkv': _w(ks[16], (N_NA, D, 3 * NA_HEADS * NA_HEAD_DIM), D),
        'na_rpb': 0.1 * jax.random.normal(ks[17], (N_NA, NA_HEADS, 2 * WIN_R - 1, 2 * WIN_C - 1), dtype=jnp.float32),
        'na_w_o': _w(ks[18], (N_NA, NA_HEADS * NA_HEAD_DIM, D), NA_HEADS * NA_HEAD_DIM),
        'ffn_w_gu': _w(ks[19], (DEPTH, D, 2 * FFN_HIDDEN), D),
        'ffn_w_down': _w(ks[20], (DEPTH, FFN_HIDDEN, D), FFN_HIDDEN),
    }


def reference(x_prompt, x_sample, c_prompt, c_sample, ada_w, ada_b, norm_pre_mix, norm_post_mix,
              norm_pre_ffn, norm_post_ffn, mla_w_dkv, mla_q_norm, mla_kv_norm, mla_w_uq, mla_w_ukv,
              mla_w_o, na_w_qkv, na_rpb, na_w_o, ffn_w_gu, ffn_w_down):
    y_prompt = _trunk(x_prompt, c_prompt, ada_w, ada_b, norm_pre_mix, norm_post_mix, norm_pre_ffn,
                      norm_post_ffn, mla_w_dkv, mla_q_norm, mla_kv_norm, mla_w_uq, mla_w_ukv, mla_w_o,
                      na_w_qkv, na_rpb, na_w_o, ffn_w_gu, ffn_w_down)
    y_sample = _trunk(x_sample, c_sample, ada_w, ada_b, norm_pre_mix, norm_post_mix, norm_pre_ffn,
                      norm_post_ffn, mla_w_dkv, mla_q_norm, mla_kv_norm, mla_w_uq, mla_w_ukv, mla_w_o,
                      na_w_qkv, na_rpb, na_w_o, ffn_w_gu, ffn_w_down)
    return (y_prompt, y_sample)
```

```python
import functools

import jax
import jax.numpy as jnp
import numpy as np
from jax import lax
from jax.experimental import pallas as pl
from jax.experimental.pallas import tpu as pltpu

F32 = jnp.float32
BF16 = jnp.bfloat16

D_MODEL = 1024
DEPTH = 2
N_MOD = 6
RMS_EPS = 1e-6
NEG_INF = -1e30

MLA_HEADS = 16
Q_LORA = 256
KV_LORA = 128
QK_NOPE = 64
QK_ROPE = 32
V_DIM = 64
ROPE_THETA = 10000.0
MLA_SCALE = (QK_NOPE + QK_ROPE) ** -0.5
HEAD_PAD = 128
V_ROWS = 80
LAT_COLS = 640

NA_HEADS = 16
NA_HEAD_DIM = 64
GRID_W = 64
WIN_R = 8
WIN_C = 16
NA_QROWS = 4
NA_BAND = 12
NA_SCALE = NA_HEAD_DIM ** -0.5

FFN_HIDDEN = 2816
FFN_CHUNKS = ((0, 1536), (1536, 2816))

TOKEN_TILE = 512
MLA_TQ = 256
MLA_TK = 512
NA_CHUNK = 256

VMEM_LIMIT = 56 * 1024 * 1024

_NT = (((1,), (1,)), ((), ()))


def _dot(a, b):
    return jnp.dot(a, b, preferred_element_type=F32)


def _dot_nt(a, b):
    return lax.dot_general(a, b, _NT, preferred_element_type=F32)


def _rms(x, g):
    ms = jnp.mean(x * x, axis=-1, keepdims=True)
    return x * lax.rsqrt(ms + RMS_EPS) * g


def _const_spec(shape):
    zeros = (0,) * len(shape)
    return pl.BlockSpec(shape, lambda *_: zeros, pipeline_mode=pl.Buffered(1))


def _params(n_axes):
    return pltpu.CompilerParams(
        dimension_semantics=("arbitrary",) * n_axes, vmem_limit_bytes=VMEM_LIMIT)


def _ada_kernel(c_ref, w_ref, b_ref, o_ref):
    c = c_ref[...]
    c_act = c / (1.0 + jnp.exp(-c))
    o_ref[0] = _dot(c_act.astype(BF16), w_ref[0]) + b_ref[0]


def _ada_mod(c_all, ada_w, ada_b):
    n_rows = c_all.shape[0]
    n_out = N_MOD * D_MODEL
    tn = 1536
    return pl.pallas_call(
        _ada_kernel,
        grid=(DEPTH, n_out // tn),
        in_specs=[
            pl.BlockSpec((n_rows, D_MODEL), lambda i, j: (0, 0)),
            pl.BlockSpec((1, D_MODEL, tn), lambda i, j: (i, 0, j)),
            pl.BlockSpec((1, 1, tn), lambda i, j: (i, 0, j)),
        ],
        out_specs=pl.BlockSpec((1, n_rows, tn), lambda i, j: (i, 0, j)),
        out_shape=jax.ShapeDtypeStruct((DEPTH, n_rows, n_out), F32),
        compiler_params=_params(2),
        name="ada_mod",
    )(c_all, ada_w.astype(BF16), ada_b.reshape(DEPTH, 1, n_out))


def _mla_pre_kernel(x_ref, mod_ref, gains_ref, wdkv_ref, qn_ref, kvn_ref, wqT_ref, wk_ref,
                    wvT_ref, ones_ref, ck_ref, sk_ref, cT_ref, sT_ref, qT_out, k_out, vT_out):
    x = x_ref[0]
    shift = mod_ref[0, 0:1, :]
    scale = mod_ref[0, 1:2, :]
    h = (_rms(x, gains_ref[0:1, :]) * (1.0 + scale) + shift).astype(BF16)
    lat = _dot(h, wdkv_ref[...])
    cq = _rms(lat[:, 0:Q_LORA], qn_ref[...]).astype(BF16)
    ckv = _rms(lat[:, Q_LORA:Q_LORA + KV_LORA], kvn_ref[...]).astype(BF16)
    kr = lat[:, 384:512] * ck_ref[...] + lat[:, 512:640] * sk_ref[...]
    kin = jnp.concatenate([ckv, kr.astype(BF16)], axis=1)
    k_out[0] = _dot(kin, wk_ref[...]).astype(BF16)

    qT = _dot_nt(wqT_ref[...], cq) * MLA_SCALE
    cT = cT_ref[...]
    sT = sT_ref[...]
    zpad = jnp.zeros((HEAD_PAD - QK_NOPE - QK_ROPE, qT.shape[1]), BF16)
    half = QK_ROPE // 2
    for hd in range(MLA_HEADS):
        b0 = hd * HEAD_PAD
        x1 = qT[b0 + QK_NOPE:b0 + QK_NOPE + half]
        x2 = qT[b0 + QK_NOPE + half:b0 + QK_NOPE + QK_ROPE]
        qT_out[0, b0:b0 + QK_NOPE, :] = qT[b0:b0 + QK_NOPE].astype(BF16)
        qT_out[0, b0 + QK_NOPE:b0 + QK_NOPE + half, :] = (x1 * cT - x2 * sT).astype(BF16)
        qT_out[0, b0 + QK_NOPE + half:b0 + QK_NOPE + QK_ROPE, :] = (x2 * cT + x1 * sT).astype(BF16)
        qT_out[0, b0 + QK_NOPE + QK_ROPE:b0 + HEAD_PAD, :] = zpad

    vT = _dot_nt(wvT_ref[...], ckv) + ones_ref[...]
    vT_out[0, 0] = vT.astype(BF16)


def _mla_pre(x, mod, gains, w, tables):
    B, S, _ = x.shape
    tm = MLA_TK
    n_t = S // tm
    ck, sk, cT, sT = tables
    hq = MLA_HEADS * HEAD_PAD
    hv = MLA_HEADS * V_ROWS
    return pl.pallas_call(
        _mla_pre_kernel,
        grid=(B, n_t),
        in_specs=[
            pl.BlockSpec((1, tm, D_MODEL), lambda b, i: (b, i, 0)),
            pl.BlockSpec((1, N_MOD, D_MODEL), lambda b, i: (b, 0, 0)),
            _const_spec((4, D_MODEL)),
            _const_spec((D_MODEL, LAT_COLS)),
            _const_spec((1, Q_LORA)),
            _const_spec((1, KV_LORA)),
            _const_spec((hq, Q_LORA)),
            _const_spec((2 * KV_LORA, hq)),
            _const_spec((hv, KV_LORA)),
            _const_spec((hv, 1)),
            pl.BlockSpec((tm, 128), lambda b, i: (i, 0)),
            pl.BlockSpec((tm, 128), lambda b, i: (i, 0)),
            pl.BlockSpec((QK_ROPE // 2, tm), lambda b, i: (0, i)),
            pl.BlockSpec((QK_ROPE // 2, tm), lambda b, i: (0, i)),
        ],
        out_specs=[
            pl.BlockSpec((1, hq, tm), lambda b, i: (b, 0, i)),
            pl.BlockSpec((1, tm, hq), lambda b, i: (b, i, 0)),
            pl.BlockSpec((1, 1, hv, tm), lambda b, i: (b, i, 0, 0)),
        ],
        out_shape=[
            jax.ShapeDtypeStruct((B, hq, S), BF16),
            jax.ShapeDtypeStruct((B, S, hq), BF16),
            jax.ShapeDtypeStruct((B, n_t, hv, tm), BF16),
        ],
        compiler_params=_params(2),
        name="mla_pre",
    )(x, mod, gains, w["wdkv"], w["qn"], w["kvn"], w["wqT"], w["wk"], w["wvT"], w["ones"],
      ck, sk, cT, sT)


def _mla_attn_kernel(qT_ref, k_ref, vT_ref, o_ref, *, n_chunks):
    tq = qT_ref.shape[2]
    qTs = [qT_ref[0, hh * HEAD_PAD:(hh + 1) * HEAD_PAD, :] for hh in range(2)]

    def body(j, carry):
        off = pl.multiple_of(j * MLA_TK, MLA_TK)
        out = []
        for hh in range(2):
            m, acc = carry[hh]
            k = k_ref[0, pl.ds(off, MLA_TK), hh * HEAD_PAD:(hh + 1) * HEAD_PAD]
            s = _dot(k, qTs[hh])
            m_new = jnp.maximum(m, jnp.max(s, axis=0, keepdims=True))
            alpha = jnp.exp(m - m_new)
            p = jnp.exp(s - m_new).astype(BF16)
            vT = vT_ref[0, j, hh * V_ROWS:(hh + 1) * V_ROWS, :]
            acc = alpha * acc + _dot(vT, p)
            out.append((m_new, acc))
        return tuple(out)

    init = tuple((jnp.full((1, tq), NEG_INF, F32), jnp.zeros((V_ROWS, tq), F32)) for _ in range(2))
    res = lax.fori_loop(0, n_chunks, body, init)
    oT = jnp.concatenate([acc[0:V_DIM] / acc[V_DIM:V_DIM + 1] for _, acc in res], axis=0)
    o_ref[0] = oT.T.astype(BF16)


def _mla_attn(qT, k, vT):
    B, _, S = qT.shape
    n_chunks = S // MLA_TK
    hp = MLA_HEADS // 2
    return pl.pallas_call(
        functools.partial(_mla_attn_kernel, n_chunks=n_chunks),
        grid=(B, hp, S // MLA_TQ),
        in_specs=[
            pl.BlockSpec((1, 2 * HEAD_PAD, MLA_TQ), lambda b, h, i: (b, h, i)),
            pl.BlockSpec((1, S, 2 * HEAD_PAD), lambda b, h, i: (b, 0, h)),
            pl.BlockSpec((1, n_chunks, 2 * V_ROWS, MLA_TK), lambda b, h, i: (b, 0, h, 0)),
        ],
        out_specs=pl.BlockSpec((1, MLA_TQ, 2 * V_DIM), lambda b, h, i: (b, i, h)),
        out_shape=jax.ShapeDtypeStruct((B, S, MLA_HEADS * V_DIM), BF16),
        compiler_params=_params(3),
        name="mla_attn",
    )(qT, k, vT)


def _na_pre_kernel(x_ref, mod_ref, gains_ref, wqT_ref, wk_ref, wvT_ref, ones_ref,
                   qT_out, k_out, vT_out):
    x = x_ref[0]
    shift = mod_ref[0, 0:1, :]
    scale = mod_ref[0, 1:2, :]
    h = (_rms(x, gains_ref[0:1, :]) * (1.0 + scale) + shift).astype(BF16)
    k_out[0] = _dot(h, wk_ref[...]).astype(BF16)
    qT = _dot_nt(wqT_ref[...], h) * NA_SCALE
    zpad = jnp.zeros((NA_HEAD_DIM, qT.shape[1]), BF16)
    for hd in range(NA_HEADS):
        lo = hd * HEAD_PAD + (hd % 2) * NA_HEAD_DIM
        zo = hd * HEAD_PAD + (1 - hd % 2) * NA_HEAD_DIM
        qT_out[0, lo:lo + NA_HEAD_DIM, :] = qT[hd * NA_HEAD_DIM:(hd + 1) * NA_HEAD_DIM].astype(BF16)
        qT_out[0, zo:zo + NA_HEAD_DIM, :] = zpad
    vT = _dot_nt(wvT_ref[...], h) + ones_ref[...]
    for c in range(vT.shape[1] // NA_CHUNK):
        vT_out[0, c] = vT[:, c * NA_CHUNK:(c + 1) * NA_CHUNK].astype(BF16)


def _na_pre(x, mod, gains, w):
    B, S, _ = x.shape
    tm = TOKEN_TILE
    cpt = tm // NA_CHUNK
    hq = NA_HEADS * HEAD_PAD
    hk = NA_HEADS * NA_HEAD_DIM
    hv = NA_HEADS * V_ROWS
    return pl.pallas_call(
        _na_pre_kernel,
        grid=(B, S // tm),
        in_specs=[
            pl.BlockSpec((1, tm, D_MODEL), lambda b, i: (b, i, 0)),
            pl.BlockSpec((1, N_MOD, D_MODEL), lambda b, i: (b, 0, 0)),
            _const_spec((4, D_MODEL)),
            _const_spec((hk, D_MODEL)),
            _const_spec((D_MODEL, hk)),
            _const_spec((hv, D_MODEL)),
            _const_spec((hv, 1)),
        ],
        out_specs=[
            pl.BlockSpec((1, hq, tm), lambda b, i: (b, 0, i)),
            pl.BlockSpec((1, tm, hk), lambda b, i: (b, i, 0)),
            pl.BlockSpec((1, cpt, hv, NA_CHUNK), lambda b, i: (b, i, 0, 0)),
        ],
        out_shape=[
            jax.ShapeDtypeStruct((B, hq, S), BF16),
            jax.ShapeDtypeStruct((B, S, hk), BF16),
            jax.ShapeDtypeStruct((B, S // NA_CHUNK, hv, NA_CHUNK), BF16),
        ],
        compiler_params=_params(2),
        name="na_pre",
    )(x, mod, gains, w["wqT"], w["wk"], w["wvT"], w["ones"])


def _na_band_start(blk, n_blk):
    return jnp.clip(blk - 1, 0, n_blk - NA_BAND // NA_QROWS)


def _na_attn_kernel(qT_ref, k_ref, vT_ref, bias_ref, o_ref, *, n_blk):
    blk = pl.program_id(2)
    c0 = _na_band_start(blk, n_blk)
    n_keys = NA_BAND * GRID_W
    kband = k_ref[0, pl.ds(pl.multiple_of(c0 * NA_CHUNK, NA_CHUNK), n_keys), :]
    outs = []
    for hh in range(2):
        qT = qT_ref[0, hh * HEAD_PAD:(hh + 1) * HEAD_PAD, :]
        s = _dot(kband, qT) + bias_ref[0, hh]
        m = jnp.max(s, axis=0, keepdims=True)
        p = jnp.exp(s - m).astype(BF16)
        acc = jnp.zeros((V_ROWS, p.shape[1]), F32)
        for c in range(n_keys // NA_CHUNK):
            vT = vT_ref[0, c0 + c, hh * V_ROWS:(hh + 1) * V_ROWS, :]
            acc = acc + _dot(vT, p[c * NA_CHUNK:(c + 1) * NA_CHUNK])
        outs.append(acc[0:NA_HEAD_DIM] / acc[NA_HEAD_DIM:NA_HEAD_DIM + 1])
    o_ref[0] = jnp.concatenate(outs, axis=0).T.astype(BF16)


def _na_attn(qT, k, vT, bias):
    B, _, S = qT.shape
    tq = NA_QROWS * GRID_W
    n_blk = S // tq
    hp = NA_HEADS // 2

    def bias_map(b, h, i):
        return (jnp.where(i == 0, 0, jnp.where(i == n_blk - 1, 2, 1)), h, 0, 0)

    return pl.pallas_call(
        functools.partial(_na_attn_kernel, n_blk=n_blk),
        grid=(B, hp, n_blk),
        in_specs=[
            pl.BlockSpec((1, 2 * HEAD_PAD, tq), lambda b, h, i: (b, h, i)),
            pl.BlockSpec((1, S, 2 * NA_HEAD_DIM), lambda b, h, i: (b, 0, h)),
            pl.BlockSpec((1, S // NA_CHUNK, 2 * V_ROWS, NA_CHUNK), lambda b, h, i: (b, 0, h, 0)),
            pl.BlockSpec((1, 2, NA_BAND * GRID_W, tq), bias_map),
        ],
        out_specs=pl.BlockSpec((1, tq, 2 * NA_HEAD_DIM), lambda b, h, i: (b, i, h)),
        out_shape=jax.ShapeDtypeStruct((B, S, NA_HEADS * NA_HEAD_DIM), BF16),
        compiler_params=_params(3),
        name="na_attn",
    )(qT, k, vT, bias)


def _na_bias_tables(rpb):
    p = np.arange(3)[:, None]
    qi = np.arange(NA_QROWS)[None, :]
    qr = NA_QROWS * p + qi
    r_start = np.clip(qr - WIN_R // 2, 0, NA_BAND - WIN_R)
    kr = np.arange(NA_BAND)[None, :, None]
    valid_r = (kr >= r_start[:, None, :]) & (kr < r_start[:, None, :] + WIN_R)
    dr = np.clip(kr - qr[:, None, :] + WIN_R - 1, 0, 2 * WIN_R - 2)
    c = np.arange(GRID_W)
    c_start = np.clip(c - WIN_C // 2, 0, GRID_W - WIN_C)
    kc = c[:, None]
    valid_c = (kc >= c_start[None, :]) & (kc < c_start[None, :] + WIN_C)
    dc = np.clip(kc - c[None, :] + WIN_C - 1, 0, 2 * WIN_C - 2)
    dr_i = dr[:, :, None, :, None]
    dc_i = dc[None, None, :, None, :]
    valid = valid_r[:, :, None, :, None] & valid_c[None, None, :, None, :]
    gathered = rpb.astype(F32)[:, dr_i, dc_i]
    bias = jnp.where(valid[None], gathered, NEG_INF)
    bias = jnp.transpose(bias, (1, 0, 2, 3, 4, 5))
    return bias.reshape(3, NA_HEADS, NA_BAND * GRID_W, NA_QROWS * GRID_W)


def _post_ffn_kernel(x_ref, a_ref, mod_ref, gains_ref, wo_ref, wgu_ref, wd_ref, o_ref):
    x = x_ref[0]
    gate_m = mod_ref[0, 2:3, :]
    shift_f = mod_ref[0, 3:4, :]
    scale_f = mod_ref[0, 4:5, :]
    gate_f = mod_ref[0, 5:6, :]
    mix = _dot(a_ref[0], wo_ref[...])
    x = x + gate_m * _rms(mix, gains_ref[1:2, :])
    h = (_rms(x, gains_ref[2:3, :]) * (1.0 + scale_f) + shift_f).astype(BF16)
    down = None
    for lo, hi in FFN_CHUNKS:
        g = _dot(h, wgu_ref[:, lo:hi])
        u = _dot(h, wgu_ref[:, FFN_HIDDEN + lo:FFN_HIDDEN + hi])
        act = ((g / (1.0 + jnp.exp(-g))) * u).astype(BF16)
        part = _dot(act, wd_ref[lo:hi, :])
        down = part if down is None else down + part
    o_ref[0] = x + gate_f * _rms(down, gains_ref[3:4, :])


def _post_ffn(x, attn, mod, gains, wo, wgu, wd):
    B, S, _ = x.shape
    tm = TOKEN_TILE
    return pl.pallas_call(
        _post_ffn_kernel,
        grid=(B, S // tm),
        in_specs=[
            pl.BlockSpec((1, tm, D_MODEL), lambda b, i: (b, i, 0)),
            pl.BlockSpec((1, tm, D_MODEL), lambda b, i: (b, i, 0)),
            pl.BlockSpec((1, N_MOD, D_MODEL), lambda b, i: (b, 0, 0)),
            _const_spec((4, D_MODEL)),
            _const_spec((D_MODEL, D_MODEL)),
            _const_spec((D_MODEL, 2 * FFN_HIDDEN)),
            _const_spec((FFN_HIDDEN, D_MODEL)),
        ],
        out_specs=pl.BlockSpec((1, tm, D_MODEL), lambda b, i: (b, i, 0)),
        out_shape=jax.ShapeDtypeStruct((B, S, D_MODEL), F32),
        compiler_params=_params(2),
        name="post_ffn",
    )(x, attn, mod, gains, wo, wgu, wd)


def _ones_column(n_heads):
    col = np.zeros((n_heads, V_ROWS, 1), np.float32)
    col[:, V_DIM, 0] = 1.0
    return jnp.asarray(col.reshape(n_heads * V_ROWS, 1))


def _prep_mla(w_dkv, q_norm, kv_norm, w_uq, w_ukv):
    half = QK_ROPE // 2
    r0 = Q_LORA + KV_LORA
    wdkv = jnp.zeros((D_MODEL, LAT_COLS), F32)
    wdkv = wdkv.at[:, :r0 + QK_ROPE].set(w_dkv)
    wdkv = wdkv.at[:, 512:512 + half].set(-w_dkv[:, r0 + half:r0 + QK_ROPE])
    wdkv = wdkv.at[:, 512 + half:512 + QK_ROPE].set(w_dkv[:, r0:r0 + half])
    wq = w_uq.reshape(Q_LORA, MLA_HEADS, QK_NOPE + QK_ROPE)
    wq = jnp.pad(wq, ((0, 0), (0, 0), (0, HEAD_PAD - QK_NOPE - QK_ROPE)))
    wqT = wq.reshape(Q_LORA, MLA_HEADS * HEAD_PAD).T
    wkv = w_ukv.reshape(KV_LORA, MLA_HEADS, QK_NOPE + V_DIM)
    wk_nope = jnp.pad(wkv[:, :, :QK_NOPE], ((0, 0), (0, 0), (0, HEAD_PAD - QK_NOPE)))
    eye = np.zeros((KV_LORA, MLA_HEADS, HEAD_PAD), np.float32)
    for r in range(QK_ROPE):
        eye[r, :, QK_NOPE + r] = 1.0
    wk = jnp.concatenate([wk_nope, jnp.asarray(eye)], axis=0).reshape(2 * KV_LORA, MLA_HEADS * HEAD_PAD)
    wv = jnp.pad(wkv[:, :, QK_NOPE:], ((0, 0), (0, 0), (0, V_ROWS - V_DIM)))
    wvT = wv.reshape(KV_LORA, MLA_HEADS * V_ROWS).T
    return dict(wdkv=wdkv.astype(BF16), qn=q_norm.reshape(1, Q_LORA), kvn=kv_norm.reshape(1, KV_LORA),
                wqT=wqT.astype(BF16), wk=wk.astype(BF16), wvT=wvT.astype(BF16), ones=_ones_column(MLA_HEADS))


def _prep_na(w_qkv):
    hk = NA_HEADS * NA_HEAD_DIM
    wv = w_qkv[:, 2 * hk:].reshape(D_MODEL, NA_HEADS, NA_HEAD_DIM)
    wv = jnp.pad(wv, ((0, 0), (0, 0), (0, V_ROWS - NA_HEAD_DIM)))
    return dict(wqT=w_qkv[:, :hk].T.astype(BF16), wk=w_qkv[:, hk:2 * hk].astype(BF16),
                wvT=wv.reshape(D_MODEL, NA_HEADS * V_ROWS).T.astype(BF16), ones=_ones_column(NA_HEADS))


def _rope_tables(seq_len):
    half = QK_ROPE // 2
    inv_freq = 1.0 / (ROPE_THETA ** (jnp.arange(0, QK_ROPE, 2, dtype=F32) / QK_ROPE))
    ang = jnp.arange(seq_len, dtype=F32)[:, None] * inv_freq[None, :]
    cos, sin = jnp.cos(ang), jnp.sin(ang)
    pad = jnp.zeros((seq_len, 128 - 2 * half), F32)
    ck = jnp.concatenate([cos, cos, pad], axis=1)
    sk = jnp.concatenate([sin, sin, pad], axis=1)
    return ck, sk, cos.T, sin.T


def _trunk(x, mods, gains, mla_w, na_w, na_bias, post_w):
    tables = _rope_tables(x.shape[1])
    for i in range(DEPTH):
        if i % 2 == 0:
            qT, k, vT = _mla_pre(x, mods[i], gains[i], mla_w[i // 2], tables)
            attn = _mla_attn(qT, k, vT)
        else:
            qT, k, vT = _na_pre(x, mods[i], gains[i], na_w[i // 2])
            attn = _na_attn(qT, k, vT, na_bias[i // 2])
        x = _post_ffn(x, attn, mods[i], gains[i], *post_w[i])
    return x


def kernel(x_prompt, x_sample, c_prompt, c_sample, ada_w, ada_b, norm_pre_mix, norm_post_mix, norm_pre_ffn, norm_post_ffn, mla_w_dkv, mla_q_norm, mla_kv_norm, mla_w_uq, mla_w_ukv, mla_w_o, na_w_qkv, na_rpb, na_w_o, ffn_w_gu, ffn_w_down):
    bp = x_prompt.shape[0]
    bs = x_sample.shape[0]
    mod = _ada_mod(jnp.concatenate([c_prompt, c_sample], axis=0), ada_w, ada_b)
    mod = mod.reshape(DEPTH, bp + bs, N_MOD, D_MODEL)
    gains = [jnp.stack([norm_pre_mix[i], norm_post_mix[i], norm_pre_ffn[i], norm_post_ffn[i]]) for i in range(DEPTH)]
    mla_w = [_prep_mla(mla_w_dkv[j], mla_q_norm[j], mla_kv_norm[j], mla_w_uq[j], mla_w_ukv[j])
             for j in range(mla_w_dkv.shape[0])]
    na_w = [_prep_na(na_w_qkv[j]) for j in range(na_w_qkv.shape[0])]
    na_bias = [_na_bias_tables(na_rpb[j]) for j in range(na_rpb.shape[0])]
    post_w = []
    for i in range(DEPTH):
        wo = mla_w_o[i // 2] if i % 2 == 0 else na_w_o[i // 2]
        post_w.append((wo.astype(BF16), ffn_w_gu[i].astype(BF16), ffn_w_down[i].astype(BF16)))
    y_prompt = _trunk(x_prompt, [mod[i, :bp] for i in range(DEPTH)], gains, mla_w, na_w, na_bias, post_w)
    y_sample = _trunk(x_sample, [mod[i, bp:] for i in range(DEPTH)], gains, mla_w, na_w, na_bias, post_w)
    return (y_prompt, y_sample)
```

```python
import functools

import jax
import jax.numpy as jnp
import numpy as np
from jax import lax
from jax.experimental import pallas as pl
from jax.experimental.pallas import tpu as pltpu

F32 = jnp.float32
BF16 = jnp.bfloat16

D_MODEL = 1024
DEPTH = 2
N_MOD = 6
RMS_EPS = 1e-6
NEG_INF = -1e30

MLA_HEADS = 16
Q_LORA = 256
KV_LORA = 128
QK_NOPE = 64
QK_ROPE = 32
V_DIM = 64
ROPE_THETA = 10000.0
MLA_SCALE = (QK_NOPE + QK_ROPE) ** -0.5
LOG2E = 1.4426950408889634
HEAD_PAD = 128
V_ROWS = 80
LAT_COLS = 640

NA_HEADS = 16
NA_HEAD_DIM = 64
GRID_W = 64
WIN_R = 8
WIN_C = 16
NA_QROWS = 4
NA_BAND = 12
NA_SCALE = NA_HEAD_DIM ** -0.5

FFN_HIDDEN = 2816
FFN_CHUNKS = ((0, 1536), (1536, 2816))

TOKEN_TILE = 512
MLA_TQ = 256
MLA_TK = 512
MLA_LOOKAHEAD = 2
NA_CHUNK = 256

VMEM_LIMIT = 56 * 1024 * 1024

_NT = (((1,), (1,)), ((), ()))


def _dot(a, b):
    return jnp.dot(a, b, preferred_element_type=F32)


def _dot_nt(a, b):
    return lax.dot_general(a, b, _NT, preferred_element_type=F32)


def _rms(x, g):
    ms = jnp.mean(x * x, axis=-1, keepdims=True)
    return x * lax.rsqrt(ms + RMS_EPS) * g


def _const_spec(shape):
    zeros = (0,) * len(shape)
    return pl.BlockSpec(shape, lambda *_: zeros, pipeline_mode=pl.Buffered(1))


def _params(n_axes):
    return pltpu.CompilerParams(
        dimension_semantics=("arbitrary",) * n_axes, vmem_limit_bytes=VMEM_LIMIT)


def _ada_kernel(c_ref, w_ref, b_ref, o_ref):
    c = c_ref[...]
    c_act = c / (1.0 + jnp.exp(-c))
    o_ref[0] = _dot(c_act.astype(BF16), w_ref[0]) + b_ref[0]


def _ada_mod(c_all, ada_w, ada_b):
    n_rows = c_all.shape[0]
    n_out = N_MOD * D_MODEL
    tn = 1536
    return pl.pallas_call(
        _ada_kernel,
        grid=(DEPTH, n_out // tn),
        in_specs=[
            pl.BlockSpec((n_rows, D_MODEL), lambda i, j: (0, 0)),
            pl.BlockSpec((1, D_MODEL, tn), lambda i, j: (i, 0, j)),
            pl.BlockSpec((1, 1, tn), lambda i, j: (i, 0, j)),
        ],
        out_specs=pl.BlockSpec((1, n_rows, tn), lambda i, j: (i, 0, j)),
        out_shape=jax.ShapeDtypeStruct((DEPTH, n_rows, n_out), F32),
        compiler_params=_params(2),
        name="ada_mod",
    )(c_all, ada_w.astype(BF16), ada_b.reshape(DEPTH, 1, n_out))


def _mla_pre_kernel(x_ref, mod_ref, gains_ref, wdkv_ref, qn_ref, kvn_ref, wqT_ref, wk_ref,
                    wvT_ref, ones_ref, ck_ref, sk_ref, cT_ref, sT_ref, qT_out, k_out, vT_out):
    x = x_ref[0]
    shift = mod_ref[0, 0:1, :]
    scale = mod_ref[0, 1:2, :]
    h = (_rms(x, gains_ref[0:1, :]) * (1.0 + scale) + shift).astype(BF16)
    lat = _dot(h, wdkv_ref[...])
    cq = _rms(lat[:, 0:Q_LORA], qn_ref[...]).astype(BF16)
    ckv = _rms(lat[:, Q_LORA:Q_LORA + KV_LORA], kvn_ref[...]).astype(BF16)
    kr = lat[:, 384:512] * ck_ref[...] + lat[:, 512:640] * sk_ref[...]
    kin = jnp.concatenate([ckv, kr.astype(BF16)], axis=1)
    k_out[0] = _dot(kin, wk_ref[...]).astype(BF16)

    qT = _dot_nt(wqT_ref[...], cq) * (MLA_SCALE * LOG2E)
    cT = cT_ref[...]
    sT = sT_ref[...]
    zpad = jnp.zeros((HEAD_PAD - QK_NOPE - QK_ROPE, qT.shape[1]), BF16)
    half = QK_ROPE // 2
    for hd in range(MLA_HEADS):
        b0 = hd * HEAD_PAD
        x1 = qT[b0 + QK_NOPE:b0 + QK_NOPE + half]
        x2 = qT[b0 + QK_NOPE + half:b0 + QK_NOPE + QK_ROPE]
        qT_out[0, b0:b0 + QK_NOPE, :] = qT[b0:b0 + QK_NOPE].astype(BF16)
        qT_out[0, b0 + QK_NOPE:b0 + QK_NOPE + half, :] = (x1 * cT - x2 * sT).astype(BF16)
        qT_out[0, b0 + QK_NOPE + half:b0 + QK_NOPE + QK_ROPE, :] = (x2 * cT + x1 * sT).astype(BF16)
        qT_out[0, b0 + QK_NOPE + QK_ROPE:b0 + HEAD_PAD, :] = zpad

    vT = _dot_nt(wvT_ref[...], ckv) + ones_ref[...]
    vT_out[0, 0] = vT.astype(BF16)


def _mla_pre(x, mod, gains, w, tables):
    B, S, _ = x.shape
    tm = MLA_TK
    n_t = S // tm
    ck, sk, cT, sT = tables
    hq = MLA_HEADS * HEAD_PAD
    hv = MLA_HEADS * V_ROWS
    return pl.pallas_call(
        _mla_pre_kernel,
        grid=(B, n_t),
        in_specs=[
            pl.BlockSpec((1, tm, D_MODEL), lambda b, i: (b, i, 0)),
            pl.BlockSpec((1, N_MOD, D_MODEL), lambda b, i: (b, 0, 0)),
            _const_spec((4, D_MODEL)),
            _const_spec((D_MODEL, LAT_COLS)),
            _const_spec((1, Q_LORA)),
            _const_spec((1, KV_LORA)),
            _const_spec((hq, Q_LORA)),
            _const_spec((2 * KV_LORA, hq)),
            _const_spec((hv, KV_LORA)),
            _const_spec((hv, 1)),
            pl.BlockSpec((tm, 128), lambda b, i: (i, 0)),
            pl.BlockSpec((tm, 128), lambda b, i: (i, 0)),
            pl.BlockSpec((QK_ROPE // 2, tm), lambda b, i: (0, i)),
            pl.BlockSpec((QK_ROPE // 2, tm), lambda b, i: (0, i)),
        ],
        out_specs=[
            pl.BlockSpec((1, hq, tm), lambda b, i: (b, 0, i)),
            pl.BlockSpec((1, tm, hq), lambda b, i: (b, i, 0)),
            pl.BlockSpec((1, 1, hv, tm), lambda b, i: (b, i, 0, 0)),
        ],
        out_shape=[
            jax.ShapeDtypeStruct((B, hq, S), BF16),
            jax.ShapeDtypeStruct((B, S, hq), BF16),
            jax.ShapeDtypeStruct((B, n_t, hv, tm), BF16),
        ],
        compiler_params=_params(2),
        name="mla_pre",
    )(x, mod, gains, w["wdkv"], w["qn"], w["kvn"], w["wqT"], w["wk"], w["wvT"], w["ones"],
      ck, sk, cT, sT)


def _mla_attn_kernel(qT_ref, k_ref, vT_ref, o_ref, *, n_chunks):
    tq = qT_ref.shape[2]
    qTs = [qT_ref[0, hh * HEAD_PAD:(hh + 1) * HEAD_PAD, :] for hh in range(2)]

    def scores(hh, j):
        k = k_ref[0, j * MLA_TK:(j + 1) * MLA_TK, hh * HEAD_PAD:(hh + 1) * HEAD_PAD]
        return _dot(k, qTs[hh])

    res = [(jnp.full((1, tq), NEG_INF, F32), jnp.zeros((V_ROWS, tq), F32)) for _ in range(2)]
    pending = {}
    for j in range(min(MLA_LOOKAHEAD, n_chunks)):
        for hh in range(2):
            pending[hh, j] = scores(hh, j)
    for j in range(n_chunks):
        for hh in range(2):
            if j + MLA_LOOKAHEAD < n_chunks:
                pending[hh, j + MLA_LOOKAHEAD] = scores(hh, j + MLA_LOOKAHEAD)
            s = pending.pop((hh, j))
            m, acc = res[hh]
            m_new = jnp.maximum(m, jnp.max(s, axis=0, keepdims=True))
            alpha = jnp.exp2(m - m_new)
            p = jnp.exp2(s - m_new).astype(BF16)
            vT = vT_ref[0, j, hh * V_ROWS:(hh + 1) * V_ROWS, :]
            res[hh] = (m_new, alpha * acc + _dot(vT, p))
    oT = jnp.concatenate([acc[0:V_DIM] / acc[V_DIM:V_DIM + 1] for _, acc in res], axis=0)
    o_ref[0] = oT.T.astype(BF16)


def _mla_attn(qT, k, vT):
    B, _, S = qT.shape
    n_chunks = S // MLA_TK
    hp = MLA_HEADS // 2
    return pl.pallas_call(
        functools.partial(_mla_attn_kernel, n_chunks=n_chunks),
        grid=(B, hp, S // MLA_TQ),
        in_specs=[
            pl.BlockSpec((1, 2 * HEAD_PAD, MLA_TQ), lambda b, h, i: (b, h, i)),
            pl.BlockSpec((1, S, 2 * HEAD_PAD), lambda b, h, i: (b, 0, h)),
            pl.BlockSpec((1, n_chunks, 2 * V_ROWS, MLA_TK), lambda b, h, i: (b, 0, h, 0)),
        ],
        out_specs=pl.BlockSpec((1, MLA_TQ, 2 * V_DIM), lambda b, h, i: (b, i, h)),
        out_shape=jax.ShapeDtypeStruct((B, S, MLA_HEADS * V_DIM), BF16),
        compiler_params=_params(3),
        name="mla_attn",
    )(qT, k, vT)


def _na_pre_kernel(x_ref, mod_ref, gains_ref, wqT_ref, wk_ref, wvT_ref, ones_ref,
                   qT_out, k_out, vT_out):
    x = x_ref[0]
    shift = mod_ref[0, 0:1, :]
    scale = mod_ref[0, 1:2, :]
    h = (_rms(x, gains_ref[0:1, :]) * (1.0 + scale) + shift).astype(BF16)
    k_out[0] = _dot(h, wk_ref[...]).astype(BF16)
    qT = _dot_nt(wqT_ref[...], h) * NA_SCALE
    zpad = jnp.zeros((NA_HEAD_DIM, qT.shape[1]), BF16)
    for hd in range(NA_HEADS):
        lo = hd * HEAD_PAD + (hd % 2) * NA_HEAD_DIM
        zo = hd * HEAD_PAD + (1 - hd % 2) * NA_HEAD_DIM
        qT_out[0, lo:lo + NA_HEAD_DIM, :] = qT[hd * NA_HEAD_DIM:(hd + 1) * NA_HEAD_DIM].astype(BF16)
        qT_out[0, zo:zo + NA_HEAD_DIM, :] = zpad
    vT = _dot_nt(wvT_ref[...], h) + ones_ref[...]
    for c in range(vT.shape[1] // NA_CHUNK):
        vT_out[0, c] = vT[:, c * NA_CHUNK:(c + 1) * NA_CHUNK].astype(BF16)


def _na_pre(x, mod, gains, w):
    B, S, _ = x.shape
    tm = TOKEN_TILE
    cpt = tm // NA_CHUNK
    hq = NA_HEADS * HEAD_PAD
    hk = NA_HEADS * NA_HEAD_DIM
    hv = NA_HEADS * V_ROWS
    return pl.pallas_call(
        _na_pre_kernel,
        grid=(B, S // tm),
        in_specs=[
            pl.BlockSpec((1, tm, D_MODEL), lambda b, i: (b, i, 0)),
            pl.BlockSpec((1, N_MOD, D_MODEL), lambda b, i: (b, 0, 0)),
            _const_spec((4, D_MODEL)),
            _const_spec((hk, D_MODEL)),
            _const_spec((D_MODEL, hk)),
            _const_spec((hv, D_MODEL)),
            _const_spec((hv, 1)),
        ],
        out_specs=[
            pl.BlockSpec((1, hq, tm), lambda b, i: (b, 0, i)),
            pl.BlockSpec((1, tm, hk), lambda b, i: (b, i, 0)),
            pl.BlockSpec((1, cpt, hv, NA_CHUNK), lambda b, i: (b, i, 0, 0)),
        ],
        out_shape=[
            jax.ShapeDtypeStruct((B, hq, S), BF16),
            jax.ShapeDtypeStruct((B, S, hk), BF16),
            jax.ShapeDtypeStruct((B, S // NA_CHUNK, hv, NA_CHUNK), BF16),
        ],
        compiler_params=_params(2),
        name="na_pre",
    )(x, mod, gains, w["wqT"], w["wk"], w["wvT"], w["ones"])


def _na_band_start(blk, n_blk):
    return jnp.clip(blk - 1, 0, n_blk - NA_BAND // NA_QROWS)


def _na_attn_kernel(qT_ref, k_ref, vT_ref, bias_ref, o_ref, *, n_blk):
    blk = pl.program_id(2)
    c0 = _na_band_start(blk, n_blk)
    n_keys = NA_BAND * GRID_W
    kband = k_ref[0, pl.ds(pl.multiple_of(c0 * NA_CHUNK, NA_CHUNK), n_keys), :]
    outs = []
    for hh in range(2):
        qT = qT_ref[0, hh * HEAD_PAD:(hh + 1) * HEAD_PAD, :]
        s = _dot(kband, qT) + bias_ref[0, hh]
        m = jnp.max(s, axis=0, keepdims=True)
        p = jnp.exp(s - m).astype(BF16)
        acc = jnp.zeros((V_ROWS, p.shape[1]), F32)
        for c in range(n_keys // NA_CHUNK):
            vT = vT_ref[0, c0 + c, hh * V_ROWS:(hh + 1) * V_ROWS, :]
            acc = acc + _dot(vT, p[c * NA_CHUNK:(c + 1) * NA_CHUNK])
        outs.append(acc[0:NA_HEAD_DIM] / acc[NA_HEAD_DIM:NA_HEAD_DIM + 1])
    o_ref[0] = jnp.concatenate(outs, axis=0).T.astype(BF16)


def _na_attn(qT, k, vT, bias):
    B, _, S = qT.shape
    tq = NA_QROWS * GRID_W
    n_blk = S // tq
    hp = NA_HEADS // 2

    def bias_map(b, h, i):
        return (jnp.where(i == 0, 0, jnp.where(i == n_blk - 1, 2, 1)), h, 0, 0)

    return pl.pallas_call(
        functools.partial(_na_attn_kernel, n_blk=n_blk),
        grid=(B, hp, n_blk),
        in_specs=[
            pl.BlockSpec((1, 2 * HEAD_PAD, tq), lambda b, h, i: (b, h, i)),
            pl.BlockSpec((1, S, 2 * NA_HEAD_DIM), lambda b, h, i: (b, 0, h)),
            pl.BlockSpec((1, S // NA_CHUNK, 2 * V_ROWS, NA_CHUNK), lambda b, h, i: (b, 0, h, 0)),
            pl.BlockSpec((1, 2, NA_BAND * GRID_W, tq), bias_map),
        ],
        out_specs=pl.BlockSpec((1, tq, 2 * NA_HEAD_DIM), lambda b, h, i: (b, i, h)),
        out_shape=jax.ShapeDtypeStruct((B, S, NA_HEADS * NA_HEAD_DIM), BF16),
        compiler_params=_params(3),
        name="na_attn",
    )(qT, k, vT, bias)


def _na_bias_tables(rpb):
    p = np.arange(3)[:, None]
    qi = np.arange(NA_QROWS)[None, :]
    qr = NA_QROWS * p + qi
    r_start = np.clip(qr - WIN_R // 2, 0, NA_BAND - WIN_R)
    kr = np.arange(NA_BAND)[None, :, None]
    valid_r = (kr >= r_start[:, None, :]) & (kr < r_start[:, None, :] + WIN_R)
    dr = np.clip(kr - qr[:, None, :] + WIN_R - 1, 0, 2 * WIN_R - 2)
    c = np.arange(GRID_W)
    c_start = np.clip(c - WIN_C // 2, 0, GRID_W - WIN_C)
    kc = c[:, None]
    valid_c = (kc >= c_start[None, :]) & (kc < c_start[None, :] + WIN_C)
    dc = np.clip(kc - c[None, :] + WIN_C - 1, 0, 2 * WIN_C - 2)
    valid = valid_r[:, :, None, :, None] & valid_c[None, None, :, None, :]
    onehot = (dc[None] == np.arange(2 * WIN_C - 1)[:, None, None]).astype(np.float32)
    cols = jnp.sum(rpb.astype(F32)[:, :, :, None, None] * onehot[None, None], axis=2)
    planes = jnp.stack([cols[:, int(d)] for d in dr.reshape(-1)], axis=1)
    planes = planes.reshape(NA_HEADS, 3, NA_BAND, NA_QROWS, GRID_W, GRID_W)
    bias = jnp.transpose(planes, (1, 0, 2, 4, 3, 5))
    bias = jnp.where(valid[:, None], bias, NEG_INF)
    return bias.reshape(3, NA_HEADS, NA_BAND * GRID_W, NA_QROWS * GRID_W)


def _post_ffn_kernel(x_ref, a_ref, mod_ref, gains_ref, wo_ref, wgu_ref, wd_ref, o_ref):
    x = x_ref[0]
    gate_m = mod_ref[0, 2:3, :]
    shift_f = mod_ref[0, 3:4, :]
    scale_f = mod_ref[0, 4:5, :]
    gate_f = mod_ref[0, 5:6, :]
    mix = _dot(a_ref[0], wo_ref[...])
    x = x + gate_m * _rms(mix, gains_ref[1:2, :])
    h = (_rms(x, gains_ref[2:3, :]) * (1.0 + scale_f) + shift_f).astype(BF16)
    down = None
    for lo, hi in FFN_CHUNKS:
        g = _dot(h, wgu_ref[:, lo:hi])
        u = _dot(h, wgu_ref[:, FFN_HIDDEN + lo:FFN_HIDDEN + hi])
        act = ((g / (1.0 + jnp.exp(-g))) * u).astype(BF16)
        part = _dot(act, wd_ref[lo:hi, :])
        down = part if down is None else down + part
    o_ref[0] = x + gate_f * _rms(down, gains_ref[3:4, :])


def _post_ffn(x, attn, mod, gains, wo, wgu, wd):
    B, S, _ = x.shape
    tm = TOKEN_TILE
    return pl.pallas_call(
        _post_ffn_kernel,
        grid=(B, S // tm),
        in_specs=[
            pl.BlockSpec((1, tm, D_MODEL), lambda b, i: (b, i, 0)),
            pl.BlockSpec((1, tm, D_MODEL), lambda b, i: (b, i, 0)),
            pl.BlockSpec((1, N_MOD, D_MODEL), lambda b, i: (b, 0, 0)),
            _const_spec((4, D_MODEL)),
            _const_spec((D_MODEL, D_MODEL)),
            _const_spec((D_MODEL, 2 * FFN_HIDDEN)),
            _const_spec((FFN_HIDDEN, D_MODEL)),
        ],
        out_specs=pl.BlockSpec((1, tm, D_MODEL), lambda b, i: (b, i, 0)),
        out_shape=jax.ShapeDtypeStruct((B, S, D_MODEL), F32),
        compiler_params=_params(2),
        name="post_ffn",
    )(x, attn, mod, gains, wo, wgu, wd)


def _ones_column(n_heads):
    col = np.zeros((n_heads, V_ROWS, 1), np.float32)
    col[:, V_DIM, 0] = 1.0
    return jnp.asarray(col.reshape(n_heads * V_ROWS, 1))


def _prep_mla(w_dkv, q_norm, kv_norm, w_uq, w_ukv):
    half = QK_ROPE // 2
    r0 = Q_LORA + KV_LORA
    wdkv = jnp.zeros((D_MODEL, LAT_COLS), F32)
    wdkv = wdkv.at[:, :r0 + QK_ROPE].set(w_dkv)
    wdkv = wdkv.at[:, 512:512 + half].set(-w_dkv[:, r0 + half:r0 + QK_ROPE])
    wdkv = wdkv.at[:, 512 + half:512 + QK_ROPE].set(w_dkv[:, r0:r0 + half])
    wq = w_uq.reshape(Q_LORA, MLA_HEADS, QK_NOPE + QK_ROPE)
    wq = jnp.pad(wq, ((0, 0), (0, 0), (0, HEAD_PAD - QK_NOPE - QK_ROPE)))
    wqT = wq.reshape(Q_LORA, MLA_HEADS * HEAD_PAD).T
    wkv = w_ukv.reshape(KV_LORA, MLA_HEADS, QK_NOPE + V_DIM)
    wk_nope = jnp.pad(wkv[:, :, :QK_NOPE], ((0, 0), (0, 0), (0, HEAD_PAD - QK_NOPE)))
    eye = np.zeros((KV_LORA, MLA_HEADS, HEAD_PAD), np.float32)
    for r in range(QK_ROPE):
        eye[r, :, QK_NOPE + r] = 1.0
    wk = jnp.concatenate([wk_nope, jnp.asarray(eye)], axis=0).reshape(2 * KV_LORA, MLA_HEADS * HEAD_PAD)
    wv = jnp.pad(wkv[:, :, QK_NOPE:], ((0, 0), (0, 0), (0, V_ROWS - V_DIM)))
    wvT = wv.reshape(KV_LORA, MLA_HEADS * V_ROWS).T
    return dict(wdkv=wdkv.astype(BF16), qn=q_norm.reshape(1, Q_LORA), kvn=kv_norm.reshape(1, KV_LORA),
                wqT=wqT.astype(BF16), wk=wk.astype(BF16), wvT=wvT.astype(BF16), ones=_ones_column(MLA_HEADS))


def _prep_na(w_qkv):
    hk = NA_HEADS * NA_HEAD_DIM
    wv = w_qkv[:, 2 * hk:].reshape(D_MODEL, NA_HEADS, NA_HEAD_DIM)
    wv = jnp.pad(wv, ((0, 0), (0, 0), (0, V_ROWS - NA_HEAD_DIM)))
    return dict(wqT=w_qkv[:, :hk].T.astype(BF16), wk=w_qkv[:, hk:2 * hk].astype(BF16),
                wvT=wv.reshape(D_MODEL, NA_HEADS * V_ROWS).T.astype(BF16), ones=_ones_column(NA_HEADS))


def _rope_tables(seq_len):
    half = QK_ROPE // 2
    inv_freq = 1.0 / (ROPE_THETA ** (jnp.arange(0, QK_ROPE, 2, dtype=F32) / QK_ROPE))
    ang = jnp.arange(seq_len, dtype=F32)[:, None] * inv_freq[None, :]
    cos, sin = jnp.cos(ang), jnp.sin(ang)
    pad = jnp.zeros((seq_len, 128 - 2 * half), F32)
    ck = jnp.concatenate([cos, cos, pad], axis=1)
    sk = jnp.concatenate([sin, sin, pad], axis=1)
    return ck, sk, cos.T, sin.T


def _trunk(x, mods, gains, mla_w, na_w, na_bias, post_w):
    tables = _rope_tables(x.shape[1])
    for i in range(DEPTH):
        if i % 2 == 0:
            qT, k, vT = _mla_pre(x, mods[i], gains[i], mla_w[i // 2], tables)
            attn = _mla_attn(qT, k, vT)
        else:
            qT, k, vT = _na_pre(x, mods[i], gains[i], na_w[i // 2])
            attn = _na_attn(qT, k, vT, na_bias[i // 2])
        x = _post_ffn(x, attn, mods[i], gains[i], *post_w[i])
    return x


def kernel(x_prompt, x_sample, c_prompt, c_sample, ada_w, ada_b, norm_pre_mix, norm_post_mix, norm_pre_ffn, norm_post_ffn, mla_w_dkv, mla_q_norm, mla_kv_norm, mla_w_uq, mla_w_ukv, mla_w_o, na_w_qkv, na_rpb, na_w_o, ffn_w_gu, ffn_w_down):
    bp = x_prompt.shape[0]
    bs = x_sample.shape[0]
    mod = _ada_mod(jnp.concatenate([c_prompt, c_sample], axis=0), ada_w, ada_b)
    mod = mod.reshape(DEPTH, bp + bs, N_MOD, D_MODEL)
    gains = [jnp.stack([norm_pre_mix[i], norm_post_mix[i], norm_pre_ffn[i], norm_post_ffn[i]]) for i in range(DEPTH)]
    mla_w = [_prep_mla(mla_w_dkv[j], mla_q_norm[j], mla_kv_norm[j], mla_w_uq[j], mla_w_ukv[j])
             for j in range(mla_w_dkv.shape[0])]
    na_w = [_prep_na(na_w_qkv[j]) for j in range(na_w_qkv.shape[0])]
    na_bias = [_na_bias_tables(na_rpb[j]) for j in range(na_rpb.shape[0])]
    post_w = []
    for i in range(DEPTH):
        wo = mla_w_o[i // 2] if i % 2 == 0 else na_w_o[i // 2]
        post_w.append((wo.astype(BF16), ffn_w_gu[i].astype(BF16), ffn_w_down[i].astype(BF16)))
    y_prompt = _trunk(x_prompt, [mod[i, :bp] for i in range(DEPTH)], gains, mla_w, na_w, na_bias, post_w)
    y_sample = _trunk(x_sample, [mod[i, bp:] for i in range(DEPTH)], gains, mla_w, na_w, na_bias, post_w)
    return (y_prompt, y_sample)
```

```python
import functools

import jax
import jax.numpy as jnp
import numpy as np
from jax import lax
from jax.experimental import pallas as pl
from jax.experimental.pallas import tpu as pltpu

F32 = jnp.float32
BF16 = jnp.bfloat16

D_MODEL = 1024
DEPTH = 2
N_MOD = 6
RMS_EPS = 1e-6
NEG_INF = -1e30

MLA_HEADS = 16
Q_LORA = 256
KV_LORA = 128
QK_NOPE = 64
QK_ROPE = 32
V_DIM = 64
ROPE_THETA = 10000.0
MLA_SCALE = (QK_NOPE + QK_ROPE) ** -0.5
LOG2E = 1.4426950408889634
HEAD_PAD = 128
V_ROWS = 80
LAT_COLS = 640

NA_HEADS = 16
NA_HEAD_DIM = 64
GRID_W = 64
WIN_R = 8
WIN_C = 16
NA_QROWS = 4
NA_BAND = 12
NA_SCALE = NA_HEAD_DIM ** -0.5
NA_GROUP = 8
NA_LOOKAHEAD = 4

FFN_HIDDEN = 2816
FFN_CHUNKS = ((0, 1536), (1536, 2816))

TOKEN_TILE = 512
MLA_TQ = 256
MLA_TK = 512
MLA_LOOKAHEAD = 2
NA_CHUNK = 256

VMEM_LIMIT = 56 * 1024 * 1024

_NT = (((1,), (1,)), ((), ()))


def _dot(a, b):
    return jnp.dot(a, b, preferred_element_type=F32)


def _dot_nt(a, b):
    return lax.dot_general(a, b, _NT, preferred_element_type=F32)


def _rms(x, g):
    ms = jnp.mean(x * x, axis=-1, keepdims=True)
    return x * lax.rsqrt(ms + RMS_EPS) * g


def _const_spec(shape):
    zeros = (0,) * len(shape)
    return pl.BlockSpec(shape, lambda *_: zeros, pipeline_mode=pl.Buffered(1))


def _params(n_axes):
    return pltpu.CompilerParams(
        dimension_semantics=("arbitrary",) * n_axes, vmem_limit_bytes=VMEM_LIMIT)


def _ada_kernel(c_ref, w_ref, b_ref, o_ref):
    c = c_ref[...]
    c_act = c / (1.0 + jnp.exp(-c))
    o_ref[0] = _dot(c_act.astype(BF16), w_ref[0]) + b_ref[0]


def _ada_mod(c_all, ada_w, ada_b):
    n_rows = c_all.shape[0]
    n_out = N_MOD * D_MODEL
    tn = 1536
    return pl.pallas_call(
        _ada_kernel,
        grid=(DEPTH, n_out // tn),
        in_specs=[
            pl.BlockSpec((n_rows, D_MODEL), lambda i, j: (0, 0)),
            pl.BlockSpec((1, D_MODEL, tn), lambda i, j: (i, 0, j)),
            pl.BlockSpec((1, 1, tn), lambda i, j: (i, 0, j)),
        ],
        out_specs=pl.BlockSpec((1, n_rows, tn), lambda i, j: (i, 0, j)),
        out_shape=jax.ShapeDtypeStruct((DEPTH, n_rows, n_out), F32),
        compiler_params=_params(2),
        name="ada_mod",
    )(c_all, ada_w.astype(BF16), ada_b.reshape(DEPTH, 1, n_out))


def _mla_pre_kernel(x_ref, mod_ref, gains_ref, wdkv_ref, qn_ref, kvn_ref, wqT_ref, wk_ref,
                    wvT_ref, ones_ref, ck_ref, sk_ref, cT_ref, sT_ref, qT_out, k_out, vT_out):
    x = x_ref[0]
    shift = mod_ref[0, 0:1, :]
    scale = mod_ref[0, 1:2, :]
    h = (_rms(x, gains_ref[0:1, :]) * (1.0 + scale) + shift).astype(BF16)
    lat = _dot(h, wdkv_ref[...])
    cq = _rms(lat[:, 0:Q_LORA], qn_ref[...]).astype(BF16)
    ckv = _rms(lat[:, Q_LORA:Q_LORA + KV_LORA], kvn_ref[...]).astype(BF16)
    kr = lat[:, 384:512] * ck_ref[...] + lat[:, 512:640] * sk_ref[...]
    kin = jnp.concatenate([ckv, kr.astype(BF16)], axis=1)
    k_out[0] = _dot(kin, wk_ref[...]).astype(BF16)

    qT = _dot_nt(wqT_ref[...], cq) * (MLA_SCALE * LOG2E)
    cT = cT_ref[...]
    sT = sT_ref[...]
    zpad = jnp.zeros((HEAD_PAD - QK_NOPE - QK_ROPE, qT.shape[1]), BF16)
    half = QK_ROPE // 2
    for hd in range(MLA_HEADS):
        b0 = hd * HEAD_PAD
        x1 = qT[b0 + QK_NOPE:b0 + QK_NOPE + half]
        x2 = qT[b0 + QK_NOPE + half:b0 + QK_NOPE + QK_ROPE]
        qT_out[0, b0:b0 + QK_NOPE, :] = qT[b0:b0 + QK_NOPE].astype(BF16)
        qT_out[0, b0 + QK_NOPE:b0 + QK_NOPE + half, :] = (x1 * cT - x2 * sT).astype(BF16)
        qT_out[0, b0 + QK_NOPE + half:b0 + QK_NOPE + QK_ROPE, :] = (x2 * cT + x1 * sT).astype(BF16)
        qT_out[0, b0 + QK_NOPE + QK_ROPE:b0 + HEAD_PAD, :] = zpad

    vT = _dot_nt(wvT_ref[...], ckv) + ones_ref[...]
    vT_out[0, 0] = vT.astype(BF16)


def _mla_pre(x, mod, gains, w, tables):
    B, S, _ = x.shape
    tm = MLA_TK
    n_t = S // tm
    ck, sk, cT, sT = tables
    hq = MLA_HEADS * HEAD_PAD
    hv = MLA_HEADS * V_ROWS
    return pl.pallas_call(
        _mla_pre_kernel,
        grid=(B, n_t),
        in_specs=[
            pl.BlockSpec((1, tm, D_MODEL), lambda b, i: (b, i, 0)),
            pl.BlockSpec((1, N_MOD, D_MODEL), lambda b, i: (b, 0, 0)),
            _const_spec((4, D_MODEL)),
            _const_spec((D_MODEL, LAT_COLS)),
            _const_spec((1, Q_LORA)),
            _const_spec((1, KV_LORA)),
            _const_spec((hq, Q_LORA)),
            _const_spec((2 * KV_LORA, hq)),
            _const_spec((hv, KV_LORA)),
            _const_spec((hv, 1)),
            pl.BlockSpec((tm, 128), lambda b, i: (i, 0)),
            pl.BlockSpec((tm, 128), lambda b, i: (i, 0)),
            pl.BlockSpec((QK_ROPE // 2, tm), lambda b, i: (0, i)),
            pl.BlockSpec((QK_ROPE // 2, tm), lambda b, i: (0, i)),
        ],
        out_specs=[
            pl.BlockSpec((1, hq, tm), lambda b, i: (b, 0, i)),
            pl.BlockSpec((1, tm, hq), lambda b, i: (b, i, 0)),
            pl.BlockSpec((1, 1, hv, tm), lambda b, i: (b, i, 0, 0)),
        ],
        out_shape=[
            jax.ShapeDtypeStruct((B, hq, S), BF16),
            jax.ShapeDtypeStruct((B, S, hq), BF16),
            jax.ShapeDtypeStruct((B, n_t, hv, tm), BF16),
        ],
        compiler_params=_params(2),
        name="mla_pre",
    )(x, mod, gains, w["wdkv"], w["qn"], w["kvn"], w["wqT"], w["wk"], w["wvT"], w["ones"],
      ck, sk, cT, sT)


def _mla_attn_kernel(qT_ref, k_ref, vT_ref, o_ref, *, n_chunks):
    tq = qT_ref.shape[2]
    qTs = [qT_ref[0, hh * HEAD_PAD:(hh + 1) * HEAD_PAD, :] for hh in range(2)]

    def scores(hh, j):
        k = k_ref[0, j * MLA_TK:(j + 1) * MLA_TK, hh * HEAD_PAD:(hh + 1) * HEAD_PAD]
        return _dot(k, qTs[hh])

    res = [(jnp.full((1, tq), NEG_INF, F32), jnp.zeros((V_ROWS, tq), F32)) for _ in range(2)]
    pending = {}
    for j in range(min(MLA_LOOKAHEAD, n_chunks)):
        for hh in range(2):
            pending[hh, j] = scores(hh, j)
    for j in range(n_chunks):
        for hh in range(2):
            if j + MLA_LOOKAHEAD < n_chunks:
                pending[hh, j + MLA_LOOKAHEAD] = scores(hh, j + MLA_LOOKAHEAD)
            s = pending.pop((hh, j))
            m, acc = res[hh]
            m_new = jnp.maximum(m, jnp.max(s, axis=0, keepdims=True))
            alpha = jnp.exp2(m - m_new)
            p = jnp.exp2((s - m_new).astype(BF16))
            vT = vT_ref[0, j, hh * V_ROWS:(hh + 1) * V_ROWS, :]
            res[hh] = (m_new, alpha * acc + _dot(vT, p))
    oT = jnp.concatenate([acc[0:V_DIM] / acc[V_DIM:V_DIM + 1] for _, acc in res], axis=0)
    o_ref[0] = oT.T.astype(BF16)


def _mla_attn(qT, k, vT):
    B, _, S = qT.shape
    n_chunks = S // MLA_TK
    hp = MLA_HEADS // 2
    return pl.pallas_call(
        functools.partial(_mla_attn_kernel, n_chunks=n_chunks),
        grid=(B, hp, S // MLA_TQ),
        in_specs=[
            pl.BlockSpec((1, 2 * HEAD_PAD, MLA_TQ), lambda b, h, i: (b, h, i)),
            pl.BlockSpec((1, S, 2 * HEAD_PAD), lambda b, h, i: (b, 0, h)),
            pl.BlockSpec((1, n_chunks, 2 * V_ROWS, MLA_TK), lambda b, h, i: (b, 0, h, 0)),
        ],
        out_specs=pl.BlockSpec((1, MLA_TQ, 2 * V_DIM), lambda b, h, i: (b, i, h)),
        out_shape=jax.ShapeDtypeStruct((B, S, MLA_HEADS * V_DIM), BF16),
        compiler_params=_params(3),
        name="mla_attn",
    )(qT, k, vT)


def _na_pre_kernel(x_ref, mod_ref, gains_ref, wqT_ref, wk_ref, wvT_ref, ones_ref,
                   qT_out, k_out, vT_out):
    x = x_ref[0]
    shift = mod_ref[0, 0:1, :]
    scale = mod_ref[0, 1:2, :]
    h = (_rms(x, gains_ref[0:1, :]) * (1.0 + scale) + shift).astype(BF16)
    k_out[0] = _dot(h, wk_ref[...]).astype(BF16)
    qT = _dot_nt(wqT_ref[...], h) * (NA_SCALE * LOG2E)
    zpad = jnp.zeros((NA_HEAD_DIM, qT.shape[1]), BF16)
    for hd in range(NA_HEADS):
        lo = hd * HEAD_PAD + (hd % 2) * NA_HEAD_DIM
        zo = hd * HEAD_PAD + (1 - hd % 2) * NA_HEAD_DIM
        qT_out[0, lo:lo + NA_HEAD_DIM, :] = qT[hd * NA_HEAD_DIM:(hd + 1) * NA_HEAD_DIM].astype(BF16)
        qT_out[0, zo:zo + NA_HEAD_DIM, :] = zpad
    vT = _dot_nt(wvT_ref[...], h) + ones_ref[...]
    for c in range(vT.shape[1] // NA_CHUNK):
        vT_out[0, c] = vT[:, c * NA_CHUNK:(c + 1) * NA_CHUNK].astype(BF16)


def _na_pre(x, mod, gains, w):
    B, S, _ = x.shape
    tm = TOKEN_TILE
    cpt = tm // NA_CHUNK
    hq = NA_HEADS * HEAD_PAD
    hk = NA_HEADS * NA_HEAD_DIM
    hv = NA_HEADS * V_ROWS
    return pl.pallas_call(
        _na_pre_kernel,
        grid=(B, S // tm),
        in_specs=[
            pl.BlockSpec((1, tm, D_MODEL), lambda b, i: (b, i, 0)),
            pl.BlockSpec((1, N_MOD, D_MODEL), lambda b, i: (b, 0, 0)),
            _const_spec((4, D_MODEL)),
            _const_spec((hk, D_MODEL)),
            _const_spec((D_MODEL, hk)),
            _const_spec((hv, D_MODEL)),
            _const_spec((hv, 1)),
        ],
        out_specs=[
            pl.BlockSpec((1, hq, tm), lambda b, i: (b, 0, i)),
            pl.BlockSpec((1, tm, hk), lambda b, i: (b, i, 0)),
            pl.BlockSpec((1, cpt, hv, NA_CHUNK), lambda b, i: (b, i, 0, 0)),
        ],
        out_shape=[
            jax.ShapeDtypeStruct((B, hq, S), BF16),
            jax.ShapeDtypeStruct((B, S, hk), BF16),
            jax.ShapeDtypeStruct((B, S // NA_CHUNK, hv, NA_CHUNK), BF16),
        ],
        compiler_params=_params(2),
        name="na_pre",
    )(x, mod, gains, w["wqT"], w["wk"], w["wvT"], w["ones"])


def _na_band_start(blk, n_blk):
    return jnp.clip(blk - 1, 0, n_blk - NA_BAND // NA_QROWS)


def _na_attn_kernel(qT_ref, k_ref, vT_ref, bias_ref, o_ref, *, n_blk, group):
    step = pl.program_id(2)
    tq = NA_QROWS * GRID_W
    n_keys = NA_BAND * GRID_W
    units = [(g, hh) for g in range(group) for hh in range(2)]

    def scores(g, hh):
        blk = step * group + g
        c0 = _na_band_start(blk, n_blk)
        pattern = jnp.where(blk == 0, 0, jnp.where(blk == n_blk - 1, 2, 1))
        kband = k_ref[0, pl.ds(pl.multiple_of(c0 * NA_CHUNK, NA_CHUNK), n_keys), :]
        qT = qT_ref[0, hh * HEAD_PAD:(hh + 1) * HEAD_PAD, g * tq:(g + 1) * tq]
        return c0, _dot(kband, qT) + bias_ref[pattern, hh]

    pending = {}
    for u in units[:NA_LOOKAHEAD]:
        pending[u] = scores(*u)
    outs = {}
    for i, (g, hh) in enumerate(units):
        if i + NA_LOOKAHEAD < len(units):
            nxt = units[i + NA_LOOKAHEAD]
            pending[nxt] = scores(*nxt)
        c0, s = pending.pop((g, hh))
        m = jnp.max(s, axis=0, keepdims=True)
        p = jnp.exp2((s - m).astype(BF16))
        acc = None
        for c in range(n_keys // NA_CHUNK):
            vT = vT_ref[0, c0 + c, hh * V_ROWS:(hh + 1) * V_ROWS, :]
            part = _dot(vT, p[c * NA_CHUNK:(c + 1) * NA_CHUNK])
            acc = part if acc is None else acc + part
        outs[g, hh] = acc[0:NA_HEAD_DIM] / acc[NA_HEAD_DIM:NA_HEAD_DIM + 1]
        if hh == 1:
            pair = jnp.concatenate([outs.pop((g, 0)), outs.pop((g, 1))], axis=0)
            o_ref[0, g * tq:(g + 1) * tq, :] = pair.T.astype(BF16)


def _na_attn(qT, k, vT, bias):
    B, _, S = qT.shape
    tq = NA_QROWS * GRID_W
    n_blk = S // tq
    hp = NA_HEADS // 2
    group = max(g for g in range(1, NA_GROUP + 1) if n_blk % g == 0)
    return pl.pallas_call(
        functools.partial(_na_attn_kernel, n_blk=n_blk, group=group),
        grid=(B, hp, n_blk // group),
        in_specs=[
            pl.BlockSpec((1, 2 * HEAD_PAD, group * tq), lambda b, h, i: (b, h, i)),
            pl.BlockSpec((1, S, 2 * NA_HEAD_DIM), lambda b, h, i: (b, 0, h)),
            pl.BlockSpec((1, S // NA_CHUNK, 2 * V_ROWS, NA_CHUNK), lambda b, h, i: (b, 0, h, 0)),
            pl.BlockSpec((3, 2, NA_BAND * GRID_W, tq), lambda b, h, i: (0, h, 0, 0)),
        ],
        out_specs=pl.BlockSpec((1, group * tq, 2 * NA_HEAD_DIM), lambda b, h, i: (b, i, h)),
        out_shape=jax.ShapeDtypeStruct((B, S, NA_HEADS * NA_HEAD_DIM), BF16),
        compiler_params=_params(3),
        name="na_attn",
    )(qT, k, vT, bias)


def _na_bias_tables(rpb):
    p = np.arange(3)[:, None]
    qi = np.arange(NA_QROWS)[None, :]
    qr = NA_QROWS * p + qi
    r_start = np.clip(qr - WIN_R // 2, 0, NA_BAND - WIN_R)
    kr = np.arange(NA_BAND)[None, :, None]
    valid_r = (kr >= r_start[:, None, :]) & (kr < r_start[:, None, :] + WIN_R)
    dr = np.clip(kr - qr[:, None, :] + WIN_R - 1, 0, 2 * WIN_R - 2)
    c = np.arange(GRID_W)
    c_start = np.clip(c - WIN_C // 2, 0, GRID_W - WIN_C)
    kc = c[:, None]
    valid_c = (kc >= c_start[None, :]) & (kc < c_start[None, :] + WIN_C)
    dc = np.clip(kc - c[None, :] + WIN_C - 1, 0, 2 * WIN_C - 2)
    valid = valid_r[:, :, None, :, None] & valid_c[None, None, :, None, :]
    onehot = (dc[None] == np.arange(2 * WIN_C - 1)[:, None, None]).astype(np.float32)
    cols = jnp.sum(rpb.astype(F32)[:, :, :, None, None] * onehot[None, None], axis=2)
    planes = jnp.stack([cols[:, int(d)] for d in dr.reshape(-1)], axis=1)
    planes = planes.reshape(NA_HEADS, 3, NA_BAND, NA_QROWS, GRID_W, GRID_W)
    bias = jnp.transpose(planes, (1, 0, 2, 4, 3, 5))
    bias = jnp.where(valid[:, None], bias * LOG2E, NEG_INF)
    return bias.reshape(3, NA_HEADS, NA_BAND * GRID_W, NA_QROWS * GRID_W)


def _post_ffn_kernel(x_ref, a_ref, mod_ref, gains_ref, wo_ref, wgu_ref, wd_ref, o_ref):
    x = x_ref[0]
    gate_m = mod_ref[0, 2:3, :]
    shift_f = mod_ref[0, 3:4, :]
    scale_f = mod_ref[0, 4:5, :]
    gate_f = mod_ref[0, 5:6, :]
    mix = _dot(a_ref[0], wo_ref[...])
    x = x + gate_m * _rms(mix, gains_ref[1:2, :])
    h = (_rms(x, gains_ref[2:3, :]) * (1.0 + scale_f) + shift_f).astype(BF16)
    down = None
    for lo, hi in FFN_CHUNKS:
        g = _dot(h, wgu_ref[:, lo:hi])
        u = _dot(h, wgu_ref[:, FFN_HIDDEN + lo:FFN_HIDDEN + hi])
        act = ((g / (1.0 + jnp.exp(-g))) * u).astype(BF16)
        part = _dot(act, wd_ref[lo:hi, :])
        down = part if down is None else down + part
    o_ref[0] = x + gate_f * _rms(down, gains_ref[3:4, :])


def _post_ffn(x, attn, mod, gains, wo, wgu, wd):
    B, S, _ = x.shape
    tm = TOKEN_TILE
    return pl.pallas_call(
        _post_ffn_kernel,
        grid=(B, S // tm),
        in_specs=[
            pl.BlockSpec((1, tm, D_MODEL), lambda b, i: (b, i, 0)),
            pl.BlockSpec((1, tm, D_MODEL), lambda b, i: (b, i, 0)),
            pl.BlockSpec((1, N_MOD, D_MODEL), lambda b, i: (b, 0, 0)),
            _const_spec((4, D_MODEL)),
            _const_spec((D_MODEL, D_MODEL)),
            _const_spec((D_MODEL, 2 * FFN_HIDDEN)),
            _const_spec((FFN_HIDDEN, D_MODEL)),
        ],
        out_specs=pl.BlockSpec((1, tm, D_MODEL), lambda b, i: (b, i, 0)),
        out_shape=jax.ShapeDtypeStruct((B, S, D_MODEL), F32),
        compiler_params=_params(2),
        name="post_ffn",
    )(x, attn, mod, gains, wo, wgu, wd)


def _ones_column(n_heads):
    col = np.zeros((n_heads, V_ROWS, 1), np.float32)
    col[:, V_DIM, 0] = 1.0
    return jnp.asarray(col.reshape(n_heads * V_ROWS, 1))


def _prep_mla(w_dkv, q_norm, kv_norm, w_uq, w_ukv):
    half = QK_ROPE // 2
    r0 = Q_LORA + KV_LORA
    wdkv = jnp.zeros((D_MODEL, LAT_COLS), F32)
    wdkv = wdkv.at[:, :r0 + QK_ROPE].set(w_dkv)
    wdkv = wdkv.at[:, 512:512 + half].set(-w_dkv[:, r0 + half:r0 + QK_ROPE])
    wdkv = wdkv.at[:, 512 + half:512 + QK_ROPE].set(w_dkv[:, r0:r0 + half])
    wq = w_uq.reshape(Q_LORA, MLA_HEADS, QK_NOPE + QK_ROPE)
    wq = jnp.pad(wq, ((0, 0), (0, 0), (0, HEAD_PAD - QK_NOPE - QK_ROPE)))
    wqT = wq.reshape(Q_LORA, MLA_HEADS * HEAD_PAD).T
    wkv = w_ukv.reshape(KV_LORA, MLA_HEADS, QK_NOPE + V_DIM)
    wk_nope = jnp.pad(wkv[:, :, :QK_NOPE], ((0, 0), (0, 0), (0, HEAD_PAD - QK_NOPE)))
    eye = np.zeros((KV_LORA, MLA_HEADS, HEAD_PAD), np.float32)
    for r in range(QK_ROPE):
        eye[r, :, QK_NOPE + r] = 1.0
    wk = jnp.concatenate([wk_nope, jnp.asarray(eye)], axis=0).reshape(2 * KV_LORA, MLA_HEADS * HEAD_PAD)
    wv = jnp.pad(wkv[:, :, QK_NOPE:], ((0, 0), (0, 0), (0, V_ROWS - V_DIM)))
    wvT = wv.reshape(KV_LORA, MLA_HEADS * V_ROWS).T
    return dict(wdkv=wdkv.astype(BF16), qn=q_norm.reshape(1, Q_LORA), kvn=kv_norm.reshape(1, KV_LORA),
                wqT=wqT.astype(BF16), wk=wk.astype(BF16), wvT=wvT.astype(BF16), ones=_ones_column(MLA_HEADS))


def _prep_na(w_qkv):
    hk = NA_HEADS * NA_HEAD_DIM
    wv = w_qkv[:, 2 * hk:].reshape(D_MODEL, NA_HEADS, NA_HEAD_DIM)
    wv = jnp.pad(wv, ((0, 0), (0, 0), (0, V_ROWS - NA_HEAD_DIM)))
    return dict(wqT=w_qkv[:, :hk].T.astype(BF16), wk=w_qkv[:, hk:2 * hk].astype(BF16),
                wvT=wv.reshape(D_MODEL, NA_HEADS * V_ROWS).T.astype(BF16), ones=_ones_column(NA_HEADS))


def _rope_tables(seq_len):
    half = QK_ROPE // 2
    inv_freq = 1.0 / (ROPE_THETA ** (jnp.arange(0, QK_ROPE, 2, dtype=F32) / QK_ROPE))
    ang = jnp.arange(seq_len, dtype=F32)[:, None] * inv_freq[None, :]
    cos, sin = jnp.cos(ang), jnp.sin(ang)
    pad = jnp.zeros((seq_len, 128 - 2 * half), F32)
    ck = jnp.concatenate([cos, cos, pad], axis=1)
    sk = jnp.concatenate([sin, sin, pad], axis=1)
    return ck, sk, cos.T, sin.T


def _trunk(x, mods, gains, mla_w, na_w, na_bias, post_w):
    tables = _rope_tables(x.shape[1])
    for i in range(DEPTH):
        if i % 2 == 0:
            qT, k, vT = _mla_pre(x, mods[i], gains[i], mla_w[i // 2], tables)
            attn = _mla_attn(qT, k, vT)
        else:
            qT, k, vT = _na_pre(x, mods[i], gains[i], na_w[i // 2])
            attn = _na_attn(qT, k, vT, na_bias[i // 2])
        x = _post_ffn(x, attn, mods[i], gains[i], *post_w[i])
    return x


def kernel(x_prompt, x_sample, c_prompt, c_sample, ada_w, ada_b, norm_pre_mix, norm_post_mix, norm_pre_ffn, norm_post_ffn, mla_w_dkv, mla_q_norm, mla_kv_norm, mla_w_uq, mla_w_ukv, mla_w_o, na_w_qkv, na_rpb, na_w_o, ffn_w_gu, ffn_w_down):
    bp = x_prompt.shape[0]
    bs = x_sample.shape[0]
    mod = _ada_mod(jnp.concatenate([c_prompt, c_sample], axis=0), ada_w, ada_b)
    mod = mod.reshape(DEPTH, bp + bs, N_MOD, D_MODEL)
    gains = [jnp.stack([norm_pre_mix[i], norm_post_mix[i], norm_pre_ffn[i], norm_post_ffn[i]]) for i in range(DEPTH)]
    mla_w = [_prep_mla(mla_w_dkv[j], mla_q_norm[j], mla_kv_norm[j], mla_w_uq[j], mla_w_ukv[j])
             for j in range(mla_w_dkv.shape[0])]
    na_w = [_prep_na(na_w_qkv[j]) for j in range(na_w_qkv.shape[0])]
    na_bias = [_na_bias_tables(na_rpb[j]) for j in range(na_rpb.shape[0])]
    post_w = []
    for i in range(DEPTH):
        wo = mla_w_o[i // 2] if i % 2 == 0 else na_w_o[i // 2]
        post_w.append((wo.astype(BF16), ffn_w_gu[i].astype(BF16), ffn_w_down[i].astype(BF16)))
    y_prompt = _trunk(x_prompt, [mod[i, :bp] for i in range(DEPTH)], gains, mla_w, na_w, na_bias, post_w)
    y_sample = _trunk(x_sample, [mod[i, bp:] for i in range(DEPTH)], gains, mla_w, na_w, na_bias, post_w)
    return (y_prompt, y_sample)
```

```python
import functools

import jax
import jax.numpy as jnp
import numpy as np
from jax import lax
from jax.experimental import pallas as pl
from jax.experimental.pallas import tpu as pltpu

F32 = jnp.float32
BF16 = jnp.bfloat16

D_MODEL = 1024
DEPTH = 2
N_MOD = 6
RMS_EPS = 1e-6
NEG_INF = -1e30

MLA_HEADS = 16
Q_LORA = 256
KV_LORA = 128
QK_NOPE = 64
QK_ROPE = 32
V_DIM = 64
ROPE_THETA = 10000.0
MLA_SCALE = (QK_NOPE + QK_ROPE) ** -0.5
LOG2E = 1.4426950408889634
HEAD_PAD = 128
V_ROWS = 80
LAT_COLS = 640

NA_HEADS = 16
NA_HEAD_DIM = 64
GRID_W = 64
WIN_R = 8
WIN_C = 16
NA_QROWS = 4
NA_BAND = 12
NA_SCALE = NA_HEAD_DIM ** -0.5
NA_GROUP = 8
NA_LOOKAHEAD = 4

FFN_HIDDEN = 2816
FFN_CHUNKS = ((0, 1536), (1536, 2816))

TOKEN_TILE = 512
MLA_TQ = 256
MLA_QBLOCKS = 4
MLA_TK = 512
MLA_LOOKAHEAD = 2
NA_CHUNK = 256

VMEM_LIMIT = 56 * 1024 * 1024

_NT = (((1,), (1,)), ((), ()))


def _dot(a, b):
    return jnp.dot(a, b, preferred_element_type=F32)


def _dot_nt(a, b):
    return lax.dot_general(a, b, _NT, preferred_element_type=F32)


def _rms(x, g):
    ms = jnp.mean(x * x, axis=-1, keepdims=True)
    return x * lax.rsqrt(ms + RMS_EPS) * g


def _const_spec(shape):
    zeros = (0,) * len(shape)
    return pl.BlockSpec(shape, lambda *_: zeros, pipeline_mode=pl.Buffered(1))


def _params(n_axes):
    return pltpu.CompilerParams(
        dimension_semantics=("arbitrary",) * n_axes, vmem_limit_bytes=VMEM_LIMIT)


def _ada_kernel(c_ref, w_ref, b_ref, o_ref):
    c = c_ref[...]
    c_act = c / (1.0 + jnp.exp(-c))
    o_ref[0] = _dot(c_act.astype(BF16), w_ref[0]) + b_ref[0]


def _ada_mod(c_all, ada_w, ada_b):
    n_rows = c_all.shape[0]
    n_out = N_MOD * D_MODEL
    tn = 1536
    return pl.pallas_call(
        _ada_kernel,
        grid=(DEPTH, n_out // tn),
        in_specs=[
            pl.BlockSpec((n_rows, D_MODEL), lambda i, j: (0, 0)),
            pl.BlockSpec((1, D_MODEL, tn), lambda i, j: (i, 0, j)),
            pl.BlockSpec((1, 1, tn), lambda i, j: (i, 0, j)),
        ],
        out_specs=pl.BlockSpec((1, n_rows, tn), lambda i, j: (i, 0, j)),
        out_shape=jax.ShapeDtypeStruct((DEPTH, n_rows, n_out), F32),
        compiler_params=_params(2),
        name="ada_mod",
    )(c_all, ada_w.astype(BF16), ada_b.reshape(DEPTH, 1, n_out))


def _mla_pre_kernel(x_ref, mod_ref, gains_ref, wdkv_ref, qn_ref, kvn_ref, wqT_ref, wk_ref,
                    wvT_ref, ones_ref, ck_ref, sk_ref, cT_ref, sT_ref, qT_out, k_out, vT_out):
    x = x_ref[0]
    shift = mod_ref[0, 0:1, :]
    scale = mod_ref[0, 1:2, :]
    h = (_rms(x, gains_ref[0:1, :]) * (1.0 + scale) + shift).astype(BF16)
    lat = _dot(h, wdkv_ref[...])
    cq = _rms(lat[:, 0:Q_LORA], qn_ref[...]).astype(BF16)
    ckv = _rms(lat[:, Q_LORA:Q_LORA + KV_LORA], kvn_ref[...]).astype(BF16)
    kr = lat[:, 384:512] * ck_ref[...] + lat[:, 512:640] * sk_ref[...]
    kin = jnp.concatenate([ckv, kr.astype(BF16)], axis=1)
    k_out[0] = _dot(kin, wk_ref[...]).astype(BF16)

    qT = _dot_nt(wqT_ref[...], cq) * (MLA_SCALE * LOG2E)
    cT = cT_ref[...]
    sT = sT_ref[...]
    zpad = jnp.zeros((HEAD_PAD - QK_NOPE - QK_ROPE, qT.shape[1]), BF16)
    half = QK_ROPE // 2
    for hd in range(MLA_HEADS):
        b0 = hd * HEAD_PAD
        x1 = qT[b0 + QK_NOPE:b0 + QK_NOPE + half]
        x2 = qT[b0 + QK_NOPE + half:b0 + QK_NOPE + QK_ROPE]
        qT_out[0, b0:b0 + QK_NOPE, :] = qT[b0:b0 + QK_NOPE].astype(BF16)
        qT_out[0, b0 + QK_NOPE:b0 + QK_NOPE + half, :] = (x1 * cT - x2 * sT).astype(BF16)
        qT_out[0, b0 + QK_NOPE + half:b0 + QK_NOPE + QK_ROPE, :] = (x2 * cT + x1 * sT).astype(BF16)
        qT_out[0, b0 + QK_NOPE + QK_ROPE:b0 + HEAD_PAD, :] = zpad

    vT = _dot_nt(wvT_ref[...], ckv) + ones_ref[...]
    vT_out[0, 0] = vT.astype(BF16)


def _mla_pre(x, mod, gains, w, tables):
    B, S, _ = x.shape
    tm = MLA_TK
    n_t = S // tm
    ck, sk, cT, sT = tables
    hq = MLA_HEADS * HEAD_PAD
    hv = MLA_HEADS * V_ROWS
    return pl.pallas_call(
        _mla_pre_kernel,
        grid=(B, n_t),
        in_specs=[
            pl.BlockSpec((1, tm, D_MODEL), lambda b, i: (b, i, 0)),
            pl.BlockSpec((1, N_MOD, D_MODEL), lambda b, i: (b, 0, 0)),
            _const_spec((4, D_MODEL)),
            _const_spec((D_MODEL, LAT_COLS)),
            _const_spec((1, Q_LORA)),
            _const_spec((1, KV_LORA)),
            _const_spec((hq, Q_LORA)),
            _const_spec((2 * KV_LORA, hq)),
            _const_spec((hv, KV_LORA)),
            _const_spec((hv, 1)),
            pl.BlockSpec((tm, 128), lambda b, i: (i, 0)),
            pl.BlockSpec((tm, 128), lambda b, i: (i, 0)),
            pl.BlockSpec((QK_ROPE // 2, tm), lambda b, i: (0, i)),
            pl.BlockSpec((QK_ROPE // 2, tm), lambda b, i: (0, i)),
        ],
        out_specs=[
            pl.BlockSpec((1, hq, tm), lambda b, i: (b, 0, i)),
            pl.BlockSpec((1, tm, hq), lambda b, i: (b, i, 0)),
            pl.BlockSpec((1, 1, hv, tm), lambda b, i: (b, i, 0, 0)),
        ],
        out_shape=[
            jax.ShapeDtypeStruct((B, hq, S), BF16),
            jax.ShapeDtypeStruct((B, S, hq), BF16),
            jax.ShapeDtypeStruct((B, n_t, hv, tm), BF16),
        ],
        compiler_params=_params(2),
        name="mla_pre",
    )(x, mod, gains, w["wdkv"], w["qn"], w["kvn"], w["wqT"], w["wk"], w["wvT"], w["ones"],
      ck, sk, cT, sT)


def _mla_attn_kernel(qT_ref, k_ref, vT_ref, o_ref, *, n_chunks):
    tq = MLA_TQ
    n_qb = qT_ref.shape[2] // tq

    def scores(qb, j, hh):
        k = k_ref[0, j * MLA_TK:(j + 1) * MLA_TK, hh * HEAD_PAD:(hh + 1) * HEAD_PAD]
        return _dot(k, qT_ref[0, hh * HEAD_PAD:(hh + 1) * HEAD_PAD, qb * tq:(qb + 1) * tq])

    steps = [(qb, j) for qb in range(n_qb) for j in range(n_chunks)]
    pending = {}
    for qb, j in steps[:MLA_LOOKAHEAD]:
        for hh in range(2):
            pending[qb, j, hh] = scores(qb, j, hh)
    res = {}
    for i, (qb, j) in enumerate(steps):
        for hh in range(2):
            if i + MLA_LOOKAHEAD < len(steps):
                nqb, nj = steps[i + MLA_LOOKAHEAD]
                pending[nqb, nj, hh] = scores(nqb, nj, hh)
            s = pending.pop((qb, j, hh))
            if j == 0:
                m, acc = jnp.full((1, tq), NEG_INF, F32), jnp.zeros((V_ROWS, tq), F32)
            else:
                m, acc = res[hh]
            m_new = jnp.maximum(m, jnp.max(s, axis=0, keepdims=True))
            alpha = jnp.exp2(m - m_new)
            p = jnp.exp2(s - m_new).astype(BF16)
            vT = vT_ref[0, j, hh * V_ROWS:(hh + 1) * V_ROWS, :]
            res[hh] = (m_new, alpha * acc + _dot(vT, p))
        if j == n_chunks - 1:
            oT = jnp.concatenate([res[hh][1][0:V_DIM] / res[hh][1][V_DIM:V_DIM + 1] for hh in range(2)], axis=0)
            o_ref[0, qb * tq:(qb + 1) * tq, :] = oT.T.astype(BF16)


def _mla_attn(qT, k, vT):
    B, _, S = qT.shape
    n_chunks = S // MLA_TK
    hp = MLA_HEADS // 2
    n_qb = max(g for g in range(1, MLA_QBLOCKS + 1) if (S // MLA_TQ) % g == 0)
    tq_step = MLA_TQ * n_qb
    return pl.pallas_call(
        functools.partial(_mla_attn_kernel, n_chunks=n_chunks),
        grid=(B, hp, S // tq_step),
        in_specs=[
            pl.BlockSpec((1, 2 * HEAD_PAD, tq_step), lambda b, h, i: (b, h, i)),
            pl.BlockSpec((1, S, 2 * HEAD_PAD), lambda b, h, i: (b, 0, h)),
            pl.BlockSpec((1, n_chunks, 2 * V_ROWS, MLA_TK), lambda b, h, i: (b, 0, h, 0)),
        ],
        out_specs=pl.BlockSpec((1, tq_step, 2 * V_DIM), lambda b, h, i: (b, i, h)),
        out_shape=jax.ShapeDtypeStruct((B, S, MLA_HEADS * V_DIM), BF16),
        compiler_params=_params(3),
        name="mla_attn",
    )(qT, k, vT)


def _na_pre_kernel(x_ref, mod_ref, gains_ref, wqT_ref, wk_ref, wvT_ref, ones_ref,
                   qT_out, k_out, vT_out):
    x = x_ref[0]
    shift = mod_ref[0, 0:1, :]
    scale = mod_ref[0, 1:2, :]
    h = (_rms(x, gains_ref[0:1, :]) * (1.0 + scale) + shift).astype(BF16)
    k_out[0] = _dot(h, wk_ref[...]).astype(BF16)
    qT = _dot_nt(wqT_ref[...], h) * (NA_SCALE * LOG2E)
    zpad = jnp.zeros((NA_HEAD_DIM, qT.shape[1]), BF16)
    for hd in range(NA_HEADS):
        lo = hd * HEAD_PAD + (hd % 2) * NA_HEAD_DIM
        zo = hd * HEAD_PAD + (1 - hd % 2) * NA_HEAD_DIM
        qT_out[0, lo:lo + NA_HEAD_DIM, :] = qT[hd * NA_HEAD_DIM:(hd + 1) * NA_HEAD_DIM].astype(BF16)
        qT_out[0, zo:zo + NA_HEAD_DIM, :] = zpad
    vT = _dot_nt(wvT_ref[...], h) + ones_ref[...]
    for c in range(vT.shape[1] // NA_CHUNK):
        vT_out[0, c] = vT[:, c * NA_CHUNK:(c + 1) * NA_CHUNK].astype(BF16)


def _na_pre(x, mod, gains, w):
    B, S, _ = x.shape
    tm = TOKEN_TILE
    cpt = tm // NA_CHUNK
    hq = NA_HEADS * HEAD_PAD
    hk = NA_HEADS * NA_HEAD_DIM
    hv = NA_HEADS * V_ROWS
    return pl.pallas_call(
        _na_pre_kernel,
        grid=(B, S // tm),
        in_specs=[
            pl.BlockSpec((1, tm, D_MODEL), lambda b, i: (b, i, 0)),
            pl.BlockSpec((1, N_MOD, D_MODEL), lambda b, i: (b, 0, 0)),
            _const_spec((4, D_MODEL)),
            _const_spec((hk, D_MODEL)),
            _const_spec((D_MODEL, hk)),
            _const_spec((hv, D_MODEL)),
            _const_spec((hv, 1)),
        ],
        out_specs=[
            pl.BlockSpec((1, hq, tm), lambda b, i: (b, 0, i)),
            pl.BlockSpec((1, tm, hk), lambda b, i: (b, i, 0)),
            pl.BlockSpec((1, cpt, hv, NA_CHUNK), lambda b, i: (b, i, 0, 0)),
        ],
        out_shape=[
            jax.ShapeDtypeStruct((B, hq, S), BF16),
            jax.ShapeDtypeStruct((B, S, hk), BF16),
            jax.ShapeDtypeStruct((B, S // NA_CHUNK, hv, NA_CHUNK), BF16),
        ],
        compiler_params=_params(2),
        name="na_pre",
    )(x, mod, gains, w["wqT"], w["wk"], w["wvT"], w["ones"])


def _na_band_start(blk, n_blk):
    return jnp.clip(blk - 1, 0, n_blk - NA_BAND // NA_QROWS)


def _na_attn_kernel(qT_ref, k_ref, vT_ref, bias_ref, o_ref, *, n_blk, group):
    step = pl.program_id(2)
    tq = NA_QROWS * GRID_W
    n_keys = NA_BAND * GRID_W
    units = [(g, hh) for g in range(group) for hh in range(2)]

    def scores(g, hh):
        blk = step * group + g
        c0 = _na_band_start(blk, n_blk)
        pattern = jnp.where(blk == 0, 0, jnp.where(blk == n_blk - 1, 2, 1))
        kband = k_ref[0, pl.ds(pl.multiple_of(c0 * NA_CHUNK, NA_CHUNK), n_keys), :]
        qT = qT_ref[0, hh * HEAD_PAD:(hh + 1) * HEAD_PAD, g * tq:(g + 1) * tq]
        return c0, _dot(kband, qT) + bias_ref[pattern, hh]

    pending = {}
    for u in units[:NA_LOOKAHEAD]:
        pending[u] = scores(*u)
    outs = {}
    for i, (g, hh) in enumerate(units):
        if i + NA_LOOKAHEAD < len(units):
            nxt = units[i + NA_LOOKAHEAD]
            pending[nxt] = scores(*nxt)
        c0, s = pending.pop((g, hh))
        m = jnp.max(s, axis=0, keepdims=True)
        p = jnp.exp2((s - m).astype(BF16))
        acc = None
        for c in range(n_keys // NA_CHUNK):
            vT = vT_ref[0, c0 + c, hh * V_ROWS:(hh + 1) * V_ROWS, :]
            part = _dot(vT, p[c * NA_CHUNK:(c + 1) * NA_CHUNK])
            acc = part if acc is None else acc + part
        outs[g, hh] = acc[0:NA_HEAD_DIM] / acc[NA_HEAD_DIM:NA_HEAD_DIM + 1]
        if hh == 1:
            pair = jnp.concatenate([outs.pop((g, 0)), outs.pop((g, 1))], axis=0)
            o_ref[0, g * tq:(g + 1) * tq, :] = pair.T.astype(BF16)


def _na_attn(qT, k, vT, bias):
    B, _, S = qT.shape
    tq = NA_QROWS * GRID_W
    n_blk = S // tq
    hp = NA_HEADS // 2
    group = max(g for g in range(1, NA_GROUP + 1) if n_blk % g == 0)
    return pl.pallas_call(
        functools.partial(_na_attn_kernel, n_blk=n_blk, group=group),
        grid=(B, hp, n_blk // group),
        in_specs=[
            pl.BlockSpec((1, 2 * HEAD_PAD, group * tq), lambda b, h, i: (b, h, i)),
            pl.BlockSpec((1, S, 2 * NA_HEAD_DIM), lambda b, h, i: (b, 0, h)),
            pl.BlockSpec((1, S // NA_CHUNK, 2 * V_ROWS, NA_CHUNK), lambda b, h, i: (b, 0, h, 0)),
            pl.BlockSpec((3, 2, NA_BAND * GRID_W, tq), lambda b, h, i: (0, h, 0, 0)),
        ],
        out_specs=pl.BlockSpec((1, group * tq, 2 * NA_HEAD_DIM), lambda b, h, i: (b, i, h)),
        out_shape=jax.ShapeDtypeStruct((B, S, NA_HEADS * NA_HEAD_DIM), BF16),
        compiler_params=_params(3),
        name="na_attn",
    )(qT, k, vT, bias)


def _na_bias_tables(rpb):
    p = np.arange(3)[:, None]
    qi = np.arange(NA_QROWS)[None, :]
    qr = NA_QROWS * p + qi
    r_start = np.clip(qr - WIN_R // 2, 0, NA_BAND - WIN_R)
    kr = np.arange(NA_BAND)[None, :, None]
    valid_r = (kr >= r_start[:, None, :]) & (kr < r_start[:, None, :] + WIN_R)
    dr = np.clip(kr - qr[:, None, :] + WIN_R - 1, 0, 2 * WIN_R - 2)
    c = np.arange(GRID_W)
    c_start = np.clip(c - WIN_C // 2, 0, GRID_W - WIN_C)
    kc = c[:, None]
    valid_c = (kc >= c_start[None, :]) & (kc < c_start[None, :] + WIN_C)
    dc = np.clip(kc - c[None, :] + WIN_C - 1, 0, 2 * WIN_C - 2)
    valid = valid_r[:, :, None, :, None] & valid_c[None, None, :, None, :]
    onehot = (dc[None] == np.arange(2 * WIN_C - 1)[:, None, None]).astype(np.float32)
    cols = jnp.sum(rpb.astype(F32)[:, :, :, None, None] * onehot[None, None], axis=2)
    planes = jnp.stack([cols[:, int(d)] for d in dr.reshape(-1)], axis=1)
    planes = planes.reshape(NA_HEADS, 3, NA_BAND, NA_QROWS, GRID_W, GRID_W)
    bias = jnp.transpose(planes, (1, 0, 2, 4, 3, 5))
    bias = jnp.where(valid[:, None], bias * LOG2E, NEG_INF)
    return bias.reshape(3, NA_HEADS, NA_BAND * GRID_W, NA_QROWS * GRID_W)


def _post_ffn_kernel(x_ref, a_ref, mod_ref, gains_ref, wo_ref, wgu_ref, wd_ref, o_ref):
    x = x_ref[0]
    gate_m = mod_ref[0, 2:3, :]
    shift_f = mod_ref[0, 3:4, :]
    scale_f = mod_ref[0, 4:5, :]
    gate_f = mod_ref[0, 5:6, :]
    mix = _dot(a_ref[0], wo_ref[...])
    x = x + gate_m * _rms(mix, gains_ref[1:2, :])
    h = (_rms(x, gains_ref[2:3, :]) * (1.0 + scale_f) + shift_f).astype(BF16)
    down = None
    for lo, hi in FFN_CHUNKS:
        g = _dot(h, wgu_ref[:, lo:hi])
        u = _dot(h, wgu_ref[:, FFN_HIDDEN + lo:FFN_HIDDEN + hi])
        act = ((g / (1.0 + jnp.exp(-g))) * u).astype(BF16)
        part = _dot(act, wd_ref[lo:hi, :])
        down = part if down is None else down + part
    o_ref[0] = x + gate_f * _rms(down, gains_ref[3:4, :])


def _post_ffn(x, attn, mod, gains, wo, wgu, wd):
    B, S, _ = x.shape
    tm = TOKEN_TILE
    return pl.pallas_call(
        _post_ffn_kernel,
        grid=(B, S // tm),
        in_specs=[
            pl.BlockSpec((1, tm, D_MODEL), lambda b, i: (b, i, 0)),
            pl.BlockSpec((1, tm, D_MODEL), lambda b, i: (b, i, 0)),
            pl.BlockSpec((1, N_MOD, D_MODEL), lambda b, i: (b, 0, 0)),
            _const_spec((4, D_MODEL)),
            _const_spec((D_MODEL, D_MODEL)),
            _const_spec((D_MODEL, 2 * FFN_HIDDEN)),
            _const_spec((FFN_HIDDEN, D_MODEL)),
        ],
        out_specs=pl.BlockSpec((1, tm, D_MODEL), lambda b, i: (b, i, 0)),
        out_shape=jax.ShapeDtypeStruct((B, S, D_MODEL), F32),
        compiler_params=_params(2),
        name="post_ffn",
    )(x, attn, mod, gains, wo, wgu, wd)


def _ones_column(n_heads):
    col = np.zeros((n_heads, V_ROWS, 1), np.float32)
    col[:, V_DIM, 0] = 1.0
    return jnp.asarray(col.reshape(n_heads * V_ROWS, 1))


def _prep_mla(w_dkv, q_norm, kv_norm, w_uq, w_ukv):
    half = QK_ROPE // 2
    r0 = Q_LORA + KV_LORA
    wdkv = jnp.zeros((D_MODEL, LAT_COLS), F32)
    wdkv = wdkv.at[:, :r0 + QK_ROPE].set(w_dkv)
    wdkv = wdkv.at[:, 512:512 + half].set(-w_dkv[:, r0 + half:r0 + QK_ROPE])
    wdkv = wdkv.at[:, 512 + half:512 + QK_ROPE].set(w_dkv[:, r0:r0 + half])
    wq = w_uq.reshape(Q_LORA, MLA_HEADS, QK_NOPE + QK_ROPE)
    wq = jnp.pad(wq, ((0, 0), (0, 0), (0, HEAD_PAD - QK_NOPE - QK_ROPE)))
    wqT = wq.reshape(Q_LORA, MLA_HEADS * HEAD_PAD).T
    wkv = w_ukv.reshape(KV_LORA, MLA_HEADS, QK_NOPE + V_DIM)
    wk_nope = jnp.pad(wkv[:, :, :QK_NOPE], ((0, 0), (0, 0), (0, HEAD_PAD - QK_NOPE)))
    eye = np.zeros((KV_LORA, MLA_HEADS, HEAD_PAD), np.float32)
    for r in range(QK_ROPE):
        eye[r, :, QK_NOPE + r] = 1.0
    wk = jnp.concatenate([wk_nope, jnp.asarray(eye)], axis=0).reshape(2 * KV_LORA, MLA_HEADS * HEAD_PAD)
    wv = jnp.pad(wkv[:, :, QK_NOPE:], ((0, 0), (0, 0), (0, V_ROWS - V_DIM)))
    wvT = wv.reshape(KV_LORA, MLA_HEADS * V_ROWS).T
    return dict(wdkv=wdkv.astype(BF16), qn=q_norm.reshape(1, Q_LORA), kvn=kv_norm.reshape(1, KV_LORA),
                wqT=wqT.astype(BF16), wk=wk.astype(BF16), wvT=wvT.astype(BF16), ones=_ones_column(MLA_HEADS))


def _prep_na(w_qkv):
    hk = NA_HEADS * NA_HEAD_DIM
    wv = w_qkv[:, 2 * hk:].reshape(D_MODEL, NA_HEADS, NA_HEAD_DIM)
    wv = jnp.pad(wv, ((0, 0), (0, 0), (0, V_ROWS - NA_HEAD_DIM)))
    return dict(wqT=w_qkv[:, :hk].T.astype(BF16), wk=w_qkv[:, hk:2 * hk].astype(BF16),
                wvT=wv.reshape(D_MODEL, NA_HEADS * V_ROWS).T.astype(BF16), ones=_ones_column(NA_HEADS))


def _rope_tables(seq_len):
    half = QK_ROPE // 2
    inv_freq = 1.0 / (ROPE_THETA ** (jnp.arange(0, QK_ROPE, 2, dtype=F32) / QK_ROPE))
    ang = jnp.arange(seq_len, dtype=F32)[:, None] * inv_freq[None, :]
    cos, sin = jnp.cos(ang), jnp.sin(ang)
    pad = jnp.zeros((seq_len, 128 - 2 * half), F32)
    ck = jnp.concatenate([cos, cos, pad], axis=1)
    sk = jnp.concatenate([sin, sin, pad], axis=1)
    return ck, sk, cos.T, sin.T


def _trunk(x, mods, gains, mla_w, na_w, na_bias, post_w):
    tables = _rope_tables(x.shape[1])
    for i in range(DEPTH):
        if i % 2 == 0:
            qT, k, vT = _mla_pre(x, mods[i], gains[i], mla_w[i // 2], tables)
            attn = _mla_attn(qT, k, vT)
        else:
            qT, k, vT = _na_pre(x, mods[i], gains[i], na_w[i // 2])
            attn = _na_attn(qT, k, vT, na_bias[i // 2])
        x = _post_ffn(x, attn, mods[i], gains[i], *post_w[i])
    return x


def kernel(x_prompt, x_sample, c_prompt, c_sample, ada_w, ada_b, norm_pre_mix, norm_post_mix, norm_pre_ffn, norm_post_ffn, mla_w_dkv, mla_q_norm, mla_kv_norm, mla_w_uq, mla_w_ukv, mla_w_o, na_w_qkv, na_rpb, na_w_o, ffn_w_gu, ffn_w_down):
    bp = x_prompt.shape[0]
    bs = x_sample.shape[0]
    mod = _ada_mod(jnp.concatenate([c_prompt, c_sample], axis=0), ada_w, ada_b)
    mod = mod.reshape(DEPTH, bp + bs, N_MOD, D_MODEL)
    gains = [jnp.stack([norm_pre_mix[i], norm_post_mix[i], norm_pre_ffn[i], norm_post_ffn[i]]) for i in range(DEPTH)]
    mla_w = [_prep_mla(mla_w_dkv[j], mla_q_norm[j], mla_kv_norm[j], mla_w_uq[j], mla_w_ukv[j])
             for j in range(mla_w_dkv.shape[0])]
    na_w = [_prep_na(na_w_qkv[j]) for j in range(na_w_qkv.shape[0])]
    na_bias = [_na_bias_tables(na_rpb[j]) for j in range(na_rpb.shape[0])]
    post_w = []
    for i in range(DEPTH):
        wo = mla_w_o[i // 2] if i % 2 == 0 else na_w_o[i // 2]
        post_w.append((wo.astype(BF16), ffn_w_gu[i].astype(BF16), ffn_w_down[i].astype(BF16)))
    y_prompt = _trunk(x_prompt, [mod[i, :bp] for i in range(DEPTH)], gains, mla_w, na_w, na_bias, post_w)
    y_sample = _trunk(x_sample, [mod[i, bp:] for i in range(DEPTH)], gains, mla_w, na_w, na_bias, post_w)
    return (y_prompt, y_sample)
```

```python
import functools

import jax
import jax.numpy as jnp
import numpy as np
from jax import lax
from jax.experimental import pallas as pl
from jax.experimental.pallas import tpu as pltpu

F32 = jnp.float32
BF16 = jnp.bfloat16

D_MODEL = 1024
DEPTH = 2
N_MOD = 6
RMS_EPS = 1e-6
NEG_INF = -1e30

MLA_HEADS = 16
Q_LORA = 256
KV_LORA = 128
QK_NOPE = 64
QK_ROPE = 32
V_DIM = 64
ROPE_THETA = 10000.0
MLA_SCALE = (QK_NOPE + QK_ROPE) ** -0.5
LOG2E = 1.4426950408889634
HEAD_PAD = 128
V_ROWS = 80
LAT_COLS = 640

NA_HEADS = 16
NA_HEAD_DIM = 64
GRID_W = 64
WIN_R = 8
WIN_C = 16
NA_QROWS = 4
NA_BAND = 12
NA_SCALE = NA_HEAD_DIM ** -0.5
NA_GROUP = 8
NA_LOOKAHEAD = 4

FFN_HIDDEN = 2816
FFN_SUBTILE = 256

TOKEN_TILE = 512
MLA_TQ = 256
MLA_QBLOCKS = 4
MLA_TK = 512
MLA_LOOKAHEAD = 2
NA_CHUNK = 256

VMEM_LIMIT = 56 * 1024 * 1024

_NT = (((1,), (1,)), ((), ()))


def _dot(a, b):
    return jnp.dot(a, b, preferred_element_type=F32)


def _dot_nt(a, b):
    return lax.dot_general(a, b, _NT, preferred_element_type=F32)


def _rms(x, g):
    ms = jnp.mean(x * x, axis=-1, keepdims=True)
    return x * lax.rsqrt(ms + RMS_EPS) * g


def _const_spec(shape):
    zeros = (0,) * len(shape)
    return pl.BlockSpec(shape, lambda *_: zeros, pipeline_mode=pl.Buffered(1))


def _params(n_axes):
    return pltpu.CompilerParams(
        dimension_semantics=("arbitrary",) * n_axes, vmem_limit_bytes=VMEM_LIMIT)


def _ada_kernel(c_ref, w_ref, b_ref, o_ref):
    c = c_ref[...]
    c_act = c / (1.0 + jnp.exp(-c))
    o_ref[0] = _dot(c_act.astype(BF16), w_ref[0]) + b_ref[0]


def _ada_mod(c_all, ada_w, ada_b):
    n_rows = c_all.shape[0]
    n_out = N_MOD * D_MODEL
    tn = 1536
    return pl.pallas_call(
        _ada_kernel,
        grid=(DEPTH, n_out // tn),
        in_specs=[
            pl.BlockSpec((n_rows, D_MODEL), lambda i, j: (0, 0)),
            pl.BlockSpec((1, D_MODEL, tn), lambda i, j: (i, 0, j)),
            pl.BlockSpec((1, 1, tn), lambda i, j: (i, 0, j)),
        ],
        out_specs=pl.BlockSpec((1, n_rows, tn), lambda i, j: (i, 0, j)),
        out_shape=jax.ShapeDtypeStruct((DEPTH, n_rows, n_out), F32),
        compiler_params=_params(2),
        name="ada_mod",
    )(c_all, ada_w.astype(BF16), ada_b.reshape(DEPTH, 1, n_out))


def _mla_pre_kernel(x_ref, mod_ref, gains_ref, wdkv_ref, qn_ref, kvn_ref, wqT_ref, wk_ref,
                    wvT_ref, ones_ref, ck_ref, sk_ref, cT_ref, sT_ref, qT_out, k_out, vT_out):
    x = x_ref[0]
    shift = mod_ref[0, 0:1, :]
    scale = mod_ref[0, 1:2, :]
    h = (_rms(x, gains_ref[0:1, :]) * (1.0 + scale) + shift).astype(BF16)
    lat = _dot(h, wdkv_ref[...])
    cq = _rms(lat[:, 0:Q_LORA], qn_ref[...]).astype(BF16)
    ckv = _rms(lat[:, Q_LORA:Q_LORA + KV_LORA], kvn_ref[...]).astype(BF16)
    kr = lat[:, 384:512] * ck_ref[...] + lat[:, 512:640] * sk_ref[...]
    kin = jnp.concatenate([ckv, kr.astype(BF16)], axis=1)
    k_out[0] = _dot(kin, wk_ref[...]).astype(BF16)

    qT = _dot_nt(wqT_ref[...], cq) * (MLA_SCALE * LOG2E)
    cT = cT_ref[...]
    sT = sT_ref[...]
    zpad = jnp.zeros((HEAD_PAD - QK_NOPE - QK_ROPE, qT.shape[1]), BF16)
    half = QK_ROPE // 2
    for hd in range(MLA_HEADS):
        b0 = hd * HEAD_PAD
        x1 = qT[b0 + QK_NOPE:b0 + QK_NOPE + half]
        x2 = qT[b0 + QK_NOPE + half:b0 + QK_NOPE + QK_ROPE]
        qT_out[0, b0:b0 + QK_NOPE, :] = qT[b0:b0 + QK_NOPE].astype(BF16)
        qT_out[0, b0 + QK_NOPE:b0 + QK_NOPE + half, :] = (x1 * cT - x2 * sT).astype(BF16)
        qT_out[0, b0 + QK_NOPE + half:b0 + QK_NOPE + QK_ROPE, :] = (x2 * cT + x1 * sT).astype(BF16)
        qT_out[0, b0 + QK_NOPE + QK_ROPE:b0 + HEAD_PAD, :] = zpad

    vT = _dot_nt(wvT_ref[...], ckv) + ones_ref[...]
    vT_out[0, 0] = vT.astype(BF16)


def _mla_pre(x, mod, gains, w, tables):
    B, S, _ = x.shape
    tm = MLA_TK
    n_t = S // tm
    ck, sk, cT, sT = tables
    hq = MLA_HEADS * HEAD_PAD
    hv = MLA_HEADS * V_ROWS
    return pl.pallas_call(
        _mla_pre_kernel,
        grid=(B, n_t),
        in_specs=[
            pl.BlockSpec((1, tm, D_MODEL), lambda b, i: (b, i, 0)),
            pl.BlockSpec((1, N_MOD, D_MODEL), lambda b, i: (b, 0, 0)),
            _const_spec((4, D_MODEL)),
            _const_spec((D_MODEL, LAT_COLS)),
            _const_spec((1, Q_LORA)),
            _const_spec((1, KV_LORA)),
            _const_spec((hq, Q_LORA)),
            _const_spec((2 * KV_LORA, hq)),
            _const_spec((hv, KV_LORA)),
            _const_spec((hv, 1)),
            pl.BlockSpec((tm, 128), lambda b, i: (i, 0)),
            pl.BlockSpec((tm, 128), lambda b, i: (i, 0)),
            pl.BlockSpec((QK_ROPE // 2, tm), lambda b, i: (0, i)),
            pl.BlockSpec((QK_ROPE // 2, tm), lambda b, i: (0, i)),
        ],
        out_specs=[
            pl.BlockSpec((1, hq, tm), lambda b, i: (b, 0, i)),
            pl.BlockSpec((1, tm, hq), lambda b, i: (b, i, 0)),
            pl.BlockSpec((1, 1, hv, tm), lambda b, i: (b, i, 0, 0)),
        ],
        out_shape=[
            jax.ShapeDtypeStruct((B, hq, S), BF16),
            jax.ShapeDtypeStruct((B, S, hq), BF16),
            jax.ShapeDtypeStruct((B, n_t, hv, tm), BF16),
        ],
        compiler_params=_params(2),
        name="mla_pre",
    )(x, mod, gains, w["wdkv"], w["qn"], w["kvn"], w["wqT"], w["wk"], w["wvT"], w["ones"],
      ck, sk, cT, sT)


def _mla_attn_kernel(qT_ref, k_ref, vT_ref, o_ref, *, n_chunks):
    tq = MLA_TQ
    n_qb = qT_ref.shape[2] // tq

    def scores(qb, j, hh):
        k = k_ref[0, j * MLA_TK:(j + 1) * MLA_TK, hh * HEAD_PAD:(hh + 1) * HEAD_PAD]
        return _dot(k, qT_ref[0, hh * HEAD_PAD:(hh + 1) * HEAD_PAD, qb * tq:(qb + 1) * tq])

    steps = [(qb, j) for qb in range(n_qb) for j in range(n_chunks)]
    pending = {}
    for qb, j in steps[:MLA_LOOKAHEAD]:
        for hh in range(2):
            pending[qb, j, hh] = scores(qb, j, hh)
    res = {}
    for i, (qb, j) in enumerate(steps):
        for hh in range(2):
            if i + MLA_LOOKAHEAD < len(steps):
                nqb, nj = steps[i + MLA_LOOKAHEAD]
                pending[nqb, nj, hh] = scores(nqb, nj, hh)
            s = pending.pop((qb, j, hh))
            if j == 0:
                m, acc = jnp.full((1, tq), NEG_INF, F32), jnp.zeros((V_ROWS, tq), F32)
            else:
                m, acc = res[hh]
            m_new = jnp.maximum(m, jnp.max(s, axis=0, keepdims=True))
            alpha = jnp.exp2(m - m_new)
            p = jnp.exp2(s - m_new).astype(BF16)
            vT = vT_ref[0, j, hh * V_ROWS:(hh + 1) * V_ROWS, :]
            res[hh] = (m_new, alpha * acc + _dot(vT, p))
        if j == n_chunks - 1:
            oT = jnp.concatenate([res[hh][1][0:V_DIM] / res[hh][1][V_DIM:V_DIM + 1] for hh in range(2)], axis=0)
            o_ref[0, qb * tq:(qb + 1) * tq, :] = oT.T.astype(BF16)


def _mla_attn(qT, k, vT):
    B, _, S = qT.shape
    n_chunks = S // MLA_TK
    hp = MLA_HEADS // 2
    n_qb = max(g for g in range(1, MLA_QBLOCKS + 1) if (S // MLA_TQ) % g == 0)
    tq_step = MLA_TQ * n_qb
    return pl.pallas_call(
        functools.partial(_mla_attn_kernel, n_chunks=n_chunks),
        grid=(B, hp, S // tq_step),
        in_specs=[
            pl.BlockSpec((1, 2 * HEAD_PAD, tq_step), lambda b, h, i: (b, h, i)),
            pl.BlockSpec((1, S, 2 * HEAD_PAD), lambda b, h, i: (b, 0, h)),
            pl.BlockSpec((1, n_chunks, 2 * V_ROWS, MLA_TK), lambda b, h, i: (b, 0, h, 0)),
        ],
        out_specs=pl.BlockSpec((1, tq_step, 2 * V_DIM), lambda b, h, i: (b, i, h)),
        out_shape=jax.ShapeDtypeStruct((B, S, MLA_HEADS * V_DIM), BF16),
        compiler_params=_params(3),
        name="mla_attn",
    )(qT, k, vT)


def _na_pre_kernel(x_ref, mod_ref, gains_ref, wqT_ref, wk_ref, wvT_ref, ones_ref,
                   qT_out, k_out, vT_out):
    x = x_ref[0]
    shift = mod_ref[0, 0:1, :]
    scale = mod_ref[0, 1:2, :]
    h = (_rms(x, gains_ref[0:1, :]) * (1.0 + scale) + shift).astype(BF16)
    k_out[0] = _dot(h, wk_ref[...]).astype(BF16)
    qT = _dot_nt(wqT_ref[...], h) * (NA_SCALE * LOG2E)
    zpad = jnp.zeros((NA_HEAD_DIM, qT.shape[1]), BF16)
    for hd in range(NA_HEADS):
        lo = hd * HEAD_PAD + (hd % 2) * NA_HEAD_DIM
        zo = hd * HEAD_PAD + (1 - hd % 2) * NA_HEAD_DIM
        qT_out[0, lo:lo + NA_HEAD_DIM, :] = qT[hd * NA_HEAD_DIM:(hd + 1) * NA_HEAD_DIM].astype(BF16)
        qT_out[0, zo:zo + NA_HEAD_DIM, :] = zpad
    vT = _dot_nt(wvT_ref[...], h) + ones_ref[...]
    for c in range(vT.shape[1] // NA_CHUNK):
        vT_out[0, c] = vT[:, c * NA_CHUNK:(c + 1) * NA_CHUNK].astype(BF16)


def _na_pre(x, mod, gains, w):
    B, S, _ = x.shape
    tm = TOKEN_TILE
    cpt = tm // NA_CHUNK
    hq = NA_HEADS * HEAD_PAD
    hk = NA_HEADS * NA_HEAD_DIM
    hv = NA_HEADS * V_ROWS
    return pl.pallas_call(
        _na_pre_kernel,
        grid=(B, S // tm),
        in_specs=[
            pl.BlockSpec((1, tm, D_MODEL), lambda b, i: (b, i, 0)),
            pl.BlockSpec((1, N_MOD, D_MODEL), lambda b, i: (b, 0, 0)),
            _const_spec((4, D_MODEL)),
            _const_spec((hk, D_MODEL)),
            _const_spec((D_MODEL, hk)),
            _const_spec((hv, D_MODEL)),
            _const_spec((hv, 1)),
        ],
        out_specs=[
            pl.BlockSpec((1, hq, tm), lambda b, i: (b, 0, i)),
            pl.BlockSpec((1, tm, hk), lambda b, i: (b, i, 0)),
            pl.BlockSpec((1, cpt, hv, NA_CHUNK), lambda b, i: (b, i, 0, 0)),
        ],
        out_shape=[
            jax.ShapeDtypeStruct((B, hq, S), BF16),
            jax.ShapeDtypeStruct((B, S, hk), BF16),
            jax.ShapeDtypeStruct((B, S // NA_CHUNK, hv, NA_CHUNK), BF16),
        ],
        compiler_params=_params(2),
        name="na_pre",
    )(x, mod, gains, w["wqT"], w["wk"], w["wvT"], w["ones"])


def _na_band_start(blk, n_blk):
    return jnp.clip(blk - 1, 0, n_blk - NA_BAND // NA_QROWS)


def _na_attn_kernel(qT_ref, k_ref, vT_ref, bias_ref, o_ref, *, n_blk, group):
    step = pl.program_id(2)
    tq = NA_QROWS * GRID_W
    n_keys = NA_BAND * GRID_W
    units = [(g, hh) for g in range(group) for hh in range(2)]

    def scores(g, hh):
        blk = step * group + g
        c0 = _na_band_start(blk, n_blk)
        pattern = jnp.where(blk == 0, 0, jnp.where(blk == n_blk - 1, 2, 1))
        qT = qT_ref[0, hh * HEAD_PAD:(hh + 1) * HEAD_PAD, g * tq:(g + 1) * tq]
        tiles = []
        for c in range(n_keys // NA_CHUNK):
            kc = k_ref[0, pl.ds(pl.multiple_of((c0 + c) * NA_CHUNK, NA_CHUNK), NA_CHUNK), :]
            tiles.append(_dot(kc, qT) + bias_ref[pattern, hh, c * NA_CHUNK:(c + 1) * NA_CHUNK, :])
        return c0, tiles

    pending = {}
    for u in units[:NA_LOOKAHEAD]:
        pending[u] = scores(*u)
    outs = {}
    for i, (g, hh) in enumerate(units):
        if i + NA_LOOKAHEAD < len(units):
            nxt = units[i + NA_LOOKAHEAD]
            pending[nxt] = scores(*nxt)
        c0, tiles = pending.pop((g, hh))
        m = functools.reduce(jnp.maximum, [jnp.max(s, axis=0, keepdims=True) for s in tiles])
        acc = None
        for c, s in enumerate(tiles):
            p = jnp.exp2((s - m).astype(BF16))
            vT = vT_ref[0, c0 + c, hh * V_ROWS:(hh + 1) * V_ROWS, :]
            part = _dot(vT, p)
            acc = part if acc is None else acc + part
        outs[g, hh] = acc[0:NA_HEAD_DIM] / acc[NA_HEAD_DIM:NA_HEAD_DIM + 1]
        if hh == 1:
            pair = jnp.concatenate([outs.pop((g, 0)), outs.pop((g, 1))], axis=0)
            o_ref[0, g * tq:(g + 1) * tq, :] = pair.T.astype(BF16)


def _na_attn(qT, k, vT, bias):
    B, _, S = qT.shape
    tq = NA_QROWS * GRID_W
    n_blk = S // tq
    hp = NA_HEADS // 2
    group = max(g for g in range(1, NA_GROUP + 1) if n_blk % g == 0)
    return pl.pallas_call(
        functools.partial(_na_attn_kernel, n_blk=n_blk, group=group),
        grid=(B, hp, n_blk // group),
        in_specs=[
            pl.BlockSpec((1, 2 * HEAD_PAD, group * tq), lambda b, h, i: (b, h, i)),
            pl.BlockSpec((1, S, 2 * NA_HEAD_DIM), lambda b, h, i: (b, 0, h)),
            pl.BlockSpec((1, S // NA_CHUNK, 2 * V_ROWS, NA_CHUNK), lambda b, h, i: (b, 0, h, 0)),
            pl.BlockSpec((3, 2, NA_BAND * GRID_W, tq), lambda b, h, i: (0, h, 0, 0)),
        ],
        out_specs=pl.BlockSpec((1, group * tq, 2 * NA_HEAD_DIM), lambda b, h, i: (b, i, h)),
        out_shape=jax.ShapeDtypeStruct((B, S, NA_HEADS * NA_HEAD_DIM), BF16),
        compiler_params=_params(3),
        name="na_attn",
    )(qT, k, vT, bias)


def _na_bias_tables(rpb):
    p = np.arange(3)[:, None]
    qi = np.arange(NA_QROWS)[None, :]
    qr = NA_QROWS * p + qi
    r_start = np.clip(qr - WIN_R // 2, 0, NA_BAND - WIN_R)
    kr = np.arange(NA_BAND)[None, :, None]
    valid_r = (kr >= r_start[:, None, :]) & (kr < r_start[:, None, :] + WIN_R)
    dr = np.clip(kr - qr[:, None, :] + WIN_R - 1, 0, 2 * WIN_R - 2)
    c = np.arange(GRID_W)
    c_start = np.clip(c - WIN_C // 2, 0, GRID_W - WIN_C)
    kc = c[:, None]
    valid_c = (kc >= c_start[None, :]) & (kc < c_start[None, :] + WIN_C)
    dc = np.clip(kc - c[None, :] + WIN_C - 1, 0, 2 * WIN_C - 2)
    valid = valid_r[:, :, None, :, None] & valid_c[None, None, :, None, :]
    row_sel = (dr.reshape(-1)[:, None] == np.arange(2 * WIN_R - 1)[None, :]).astype(np.float32)
    col_sel = (dc[None] == np.arange(2 * WIN_C - 1)[:, None, None]).astype(np.float32)
    rows = jnp.sum(rpb.astype(F32)[:, None, :, :] * row_sel[None, :, :, None], axis=2)
    planes = jnp.einsum("hnd,dkq->hnkq", rows, col_sel, precision=lax.Precision.HIGHEST)
    planes = planes.reshape(NA_HEADS, 3, NA_BAND, NA_QROWS, GRID_W, GRID_W)
    bias = jnp.transpose(planes, (1, 0, 2, 4, 3, 5))
    bias = jnp.where(valid[:, None], bias * LOG2E, NEG_INF)
    return bias.reshape(3, NA_HEADS, NA_BAND * GRID_W, NA_QROWS * GRID_W)


def _post_ffn_kernel(x_ref, a_ref, mod_ref, gains_ref, wo_ref, wgu_ref, wd_ref, o_ref):
    gate_m = mod_ref[0, 2:3, :]
    shift_f = mod_ref[0, 3:4, :]
    scale_f = mod_ref[0, 4:5, :]
    gate_f = mod_ref[0, 5:6, :]
    n_sub = x_ref.shape[1] // FFN_SUBTILE
    rows = [slice(r * FFN_SUBTILE, (r + 1) * FFN_SUBTILE) for r in range(n_sub)]
    mix = [_dot(a_ref[0, r, :], wo_ref[...]) for r in rows]
    xs, gu, down = [], [], []
    for i in range(n_sub + 2):
        if i < n_sub:
            x = x_ref[0, rows[i], :] + gate_m * _rms(mix[i], gains_ref[1:2, :])
            h = (_rms(x, gains_ref[2:3, :]) * (1.0 + scale_f) + shift_f).astype(BF16)
            xs.append(x)
            gu.append((_dot(h, wgu_ref[:, 0:FFN_HIDDEN]), _dot(h, wgu_ref[:, FFN_HIDDEN:2 * FFN_HIDDEN])))
        if 1 <= i <= n_sub:
            g, u = gu[i - 1]
            act = ((g / (1.0 + jnp.exp(-g))) * u).astype(BF16)
            down.append(_dot(act, wd_ref[...]))
        if i >= 2:
            r = i - 2
            o_ref[0, rows[r], :] = xs[r] + gate_f * _rms(down[r], gains_ref[3:4, :])


def _post_ffn(x, attn, mod, gains, wo, wgu, wd):
    B, S, _ = x.shape
    tm = TOKEN_TILE
    return pl.pallas_call(
        _post_ffn_kernel,
        grid=(B, S // tm),
        in_specs=[
            pl.BlockSpec((1, tm, D_MODEL), lambda b, i: (b, i, 0)),
            pl.BlockSpec((1, tm, D_MODEL), lambda b, i: (b, i, 0)),
            pl.BlockSpec((1, N_MOD, D_MODEL), lambda b, i: (b, 0, 0)),
            _const_spec((4, D_MODEL)),
            _const_spec((D_MODEL, D_MODEL)),
            _const_spec((D_MODEL, 2 * FFN_HIDDEN)),
            _const_spec((FFN_HIDDEN, D_MODEL)),
        ],
        out_specs=pl.BlockSpec((1, tm, D_MODEL), lambda b, i: (b, i, 0)),
        out_shape=jax.ShapeDtypeStruct((B, S, D_MODEL), F32),
        compiler_params=_params(2),
        name="post_ffn",
    )(x, attn, mod, gains, wo, wgu, wd)


def _ones_column(n_heads):
    col = np.zeros((n_heads, V_ROWS, 1), np.float32)
    col[:, V_DIM, 0] = 1.0
    return jnp.asarray(col.reshape(n_heads * V_ROWS, 1))


def _prep_mla(w_dkv, q_norm, kv_norm, w_uq, w_ukv):
    half = QK_ROPE // 2
    r0 = Q_LORA + KV_LORA
    wdkv = jnp.zeros((D_MODEL, LAT_COLS), F32)
    wdkv = wdkv.at[:, :r0 + QK_ROPE].set(w_dkv)
    wdkv = wdkv.at[:, 512:512 + half].set(-w_dkv[:, r0 + half:r0 + QK_ROPE])
    wdkv = wdkv.at[:, 512 + half:512 + QK_ROPE].set(w_dkv[:, r0:r0 + half])
    wq = w_uq.reshape(Q_LORA, MLA_HEADS, QK_NOPE + QK_ROPE)
    wq = jnp.pad(wq, ((0, 0), (0, 0), (0, HEAD_PAD - QK_NOPE - QK_ROPE)))
    wqT = wq.reshape(Q_LORA, MLA_HEADS * HEAD_PAD).T
    wkv = w_ukv.reshape(KV_LORA, MLA_HEADS, QK_NOPE + V_DIM)
    wk_nope = jnp.pad(wkv[:, :, :QK_NOPE], ((0, 0), (0, 0), (0, HEAD_PAD - QK_NOPE)))
    eye = np.zeros((KV_LORA, MLA_HEADS, HEAD_PAD), np.float32)
    for r in range(QK_ROPE):
        eye[r, :, QK_NOPE + r] = 1.0
    wk = jnp.concatenate([wk_nope, jnp.asarray(eye)], axis=0).reshape(2 * KV_LORA, MLA_HEADS * HEAD_PAD)
    wv = jnp.pad(wkv[:, :, QK_NOPE:], ((0, 0), (0, 0), (0, V_ROWS - V_DIM)))
    wvT = wv.reshape(KV_LORA, MLA_HEADS * V_ROWS).T
    return dict(wdkv=wdkv.astype(BF16), qn=q_norm.reshape(1, Q_LORA), kvn=kv_norm.reshape(1, KV_LORA),
                wqT=wqT.astype(BF16), wk=wk.astype(BF16), wvT=wvT.astype(BF16), ones=_ones_column(MLA_HEADS))


def _prep_na(w_qkv):
    hk = NA_HEADS * NA_HEAD_DIM
    wv = w_qkv[:, 2 * hk:].reshape(D_MODEL, NA_HEADS, NA_HEAD_DIM)
    wv = jnp.pad(wv, ((0, 0), (0, 0), (0, V_ROWS - NA_HEAD_DIM)))
    return dict(wqT=w_qkv[:, :hk].T.astype(BF16), wk=w_qkv[:, hk:2 * hk].astype(BF16),
                wvT=wv.reshape(D_MODEL, NA_HEADS * V_ROWS).T.astype(BF16), ones=_ones_column(NA_HEADS))


def _rope_tables(seq_len):
    half = QK_ROPE // 2
    inv_freq = 1.0 / (ROPE_THETA ** (jnp.arange(0, QK_ROPE, 2, dtype=F32) / QK_ROPE))
    ang = jnp.arange(seq_len, dtype=F32)[:, None] * inv_freq[None, :]
    cos, sin = jnp.cos(ang), jnp.sin(ang)
    pad = jnp.zeros((seq_len, 128 - 2 * half), F32)
    ck = jnp.concatenate([cos, cos, pad], axis=1)
    sk = jnp.concatenate([sin, sin, pad], axis=1)
    return ck, sk, cos.T, sin.T


def _trunk(x, mods, gains, mla_w, na_w, na_bias, post_w):
    tables = _rope_tables(x.shape[1])
    for i in range(DEPTH):
        if i % 2 == 0:
            qT, k, vT = _mla_pre(x, mods[i], gains[i], mla_w[i // 2], tables)
            attn = _mla_attn(qT, k, vT)
        else:
            qT, k, vT = _na_pre(x, mods[i], gains[i], na_w[i // 2])
            attn = _na_attn(qT, k, vT, na_bias[i // 2])
        x = _post_ffn(x, attn, mods[i], gains[i], *post_w[i])
    return x


def kernel(x_prompt, x_sample, c_prompt, c_sample, ada_w, ada_b, norm_pre_mix, norm_post_mix, norm_pre_ffn, norm_post_ffn, mla_w_dkv, mla_q_norm, mla_kv_norm, mla_w_uq, mla_w_ukv, mla_w_o, na_w_qkv, na_rpb, na_w_o, ffn_w_gu, ffn_w_down):
    bp = x_prompt.shape[0]
    bs = x_sample.shape[0]
    mod = _ada_mod(jnp.concatenate([c_prompt, c_sample], axis=0), ada_w, ada_b)
    mod = mod.reshape(DEPTH, bp + bs, N_MOD, D_MODEL)
    gains = [jnp.stack([norm_pre_mix[i], norm_post_mix[i], norm_pre_ffn[i], norm_post_ffn[i]]) for i in range(DEPTH)]
    mla_w = [_prep_mla(mla_w_dkv[j], mla_q_norm[j], mla_kv_norm[j], mla_w_uq[j], mla_w_ukv[j])
             for j in range(mla_w_dkv.shape[0])]
    na_w = [_prep_na(na_w_qkv[j]) for j in range(na_w_qkv.shape[0])]
    na_bias = [_na_bias_tables(na_rpb[j]) for j in range(na_rpb.shape[0])]
    post_w = []
    for i in range(DEPTH):
        wo = mla_w_o[i // 2] if i % 2 == 0 else na_w_o[i // 2]
        post_w.append((wo.astype(BF16), ffn_w_gu[i].astype(BF16), ffn_w_down[i].astype(BF16)))
    y_prompt = _trunk(x_prompt, [mod[i, :bp] for i in range(DEPTH)], gains, mla_w, na_w, na_bias, post_w)
    y_sample = _trunk(x_sample, [mod[i, bp:] for i in range(DEPTH)], gains, mla_w, na_w, na_bias, post_w)
    return (y_prompt, y_sample)
```

```python
import functools

import jax
import jax.numpy as jnp
import numpy as np
from jax import lax
from jax.experimental import pallas as pl
from jax.experimental.pallas import tpu as pltpu

F32 = jnp.float32
BF16 = jnp.bfloat16

D_MODEL = 1024
DEPTH = 2
N_MOD = 6
RMS_EPS = 1e-6
NEG_INF = -1e30

MLA_HEADS = 16
Q_LORA = 256
KV_LORA = 128
QK_NOPE = 64
QK_ROPE = 32
V_DIM = 64
ROPE_THETA = 10000.0
MLA_SCALE = (QK_NOPE + QK_ROPE) ** -0.5
LOG2E = 1.4426950408889634
HEAD_PAD = 128
V_ROWS = 80
LAT_COLS = 640

NA_HEADS = 16
NA_HEAD_DIM = 64
GRID_W = 64
WIN_R = 8
WIN_C = 16
NA_QROWS = 4
NA_BAND = 12
NA_SCALE = NA_HEAD_DIM ** -0.5
NA_GROUP = 8
NA_LOOKAHEAD = 4

FFN_HIDDEN = 2816
FFN_SUBTILE = 256

TOKEN_TILE = 512
MLA_TQ = 256
MLA_QBLOCKS = 4
MLA_TK = 256
MLA_LOOKAHEAD = 3
NA_CHUNK = 256

VMEM_LIMIT = 56 * 1024 * 1024

_NT = (((1,), (1,)), ((), ()))


def _dot(a, b):
    return jnp.dot(a, b, preferred_element_type=F32)


def _dot_nt(a, b):
    return lax.dot_general(a, b, _NT, preferred_element_type=F32)


def _rms(x, g):
    ms = jnp.mean(x * x, axis=-1, keepdims=True)
    return x * lax.rsqrt(ms + RMS_EPS) * g


def _const_spec(shape):
    zeros = (0,) * len(shape)
    return pl.BlockSpec(shape, lambda *_: zeros, pipeline_mode=pl.Buffered(1))


def _params(n_axes):
    return pltpu.CompilerParams(
        dimension_semantics=("arbitrary",) * n_axes, vmem_limit_bytes=VMEM_LIMIT)


def _ada_kernel(c_ref, w_ref, b_ref, o_ref):
    c = c_ref[...]
    c_act = c / (1.0 + jnp.exp(-c))
    o_ref[0] = _dot(c_act.astype(BF16), w_ref[0]) + b_ref[0]


def _ada_mod(c_all, ada_w, ada_b):
    n_rows = c_all.shape[0]
    n_out = N_MOD * D_MODEL
    tn = 1536
    return pl.pallas_call(
        _ada_kernel,
        grid=(DEPTH, n_out // tn),
        in_specs=[
            pl.BlockSpec((n_rows, D_MODEL), lambda i, j: (0, 0)),
            pl.BlockSpec((1, D_MODEL, tn), lambda i, j: (i, 0, j)),
            pl.BlockSpec((1, 1, tn), lambda i, j: (i, 0, j)),
        ],
        out_specs=pl.BlockSpec((1, n_rows, tn), lambda i, j: (i, 0, j)),
        out_shape=jax.ShapeDtypeStruct((DEPTH, n_rows, n_out), F32),
        compiler_params=_params(2),
        name="ada_mod",
    )(c_all, ada_w.astype(BF16), ada_b.reshape(DEPTH, 1, n_out))


def _mla_pre_kernel(x_ref, mod_ref, gains_ref, wdkv_ref, qn_ref, kvn_ref, wqT_ref, wk_ref,
                    wvT_ref, ones_ref, ck_ref, sk_ref, cT_ref, sT_ref, qT_out, k_out, vT_out):
    x = x_ref[0]
    shift = mod_ref[0, 0:1, :]
    scale = mod_ref[0, 1:2, :]
    h = (_rms(x, gains_ref[0:1, :]) * (1.0 + scale) + shift).astype(BF16)
    lat = _dot(h, wdkv_ref[...])
    cq = _rms(lat[:, 0:Q_LORA], qn_ref[...]).astype(BF16)
    ckv = _rms(lat[:, Q_LORA:Q_LORA + KV_LORA], kvn_ref[...]).astype(BF16)
    kr = lat[:, 384:512] * ck_ref[...] + lat[:, 512:640] * sk_ref[...]
    kin = jnp.concatenate([ckv, kr.astype(BF16)], axis=1)
    k_out[0] = _dot(kin, wk_ref[...]).astype(BF16)

    qT = _dot_nt(wqT_ref[...], cq) * (MLA_SCALE * LOG2E)
    cT = cT_ref[...]
    sT = sT_ref[...]
    zpad = jnp.zeros((HEAD_PAD - QK_NOPE - QK_ROPE, qT.shape[1]), BF16)
    half = QK_ROPE // 2
    for hd in range(MLA_HEADS):
        b0 = hd * HEAD_PAD
        x1 = qT[b0 + QK_NOPE:b0 + QK_NOPE + half]
        x2 = qT[b0 + QK_NOPE + half:b0 + QK_NOPE + QK_ROPE]
        qT_out[0, b0:b0 + QK_NOPE, :] = qT[b0:b0 + QK_NOPE].astype(BF16)
        qT_out[0, b0 + QK_NOPE:b0 + QK_NOPE + half, :] = (x1 * cT - x2 * sT).astype(BF16)
        qT_out[0, b0 + QK_NOPE + half:b0 + QK_NOPE + QK_ROPE, :] = (x2 * cT + x1 * sT).astype(BF16)
        qT_out[0, b0 + QK_NOPE + QK_ROPE:b0 + HEAD_PAD, :] = zpad

    vT = _dot_nt(wvT_ref[...], ckv) + ones_ref[...]
    for c in range(vT.shape[1] // MLA_TK):
        vT_out[0, c] = vT[:, c * MLA_TK:(c + 1) * MLA_TK].astype(BF16)


def _mla_pre(x, mod, gains, w, tables):
    B, S, _ = x.shape
    tm = TOKEN_TILE
    n_t = S // tm
    cpt = tm // MLA_TK
    ck, sk, cT, sT = tables
    hq = MLA_HEADS * HEAD_PAD
    hv = MLA_HEADS * V_ROWS
    return pl.pallas_call(
        _mla_pre_kernel,
        grid=(B, n_t),
        in_specs=[
            pl.BlockSpec((1, tm, D_MODEL), lambda b, i: (b, i, 0)),
            pl.BlockSpec((1, N_MOD, D_MODEL), lambda b, i: (b, 0, 0)),
            _const_spec((4, D_MODEL)),
            _const_spec((D_MODEL, LAT_COLS)),
            _const_spec((1, Q_LORA)),
            _const_spec((1, KV_LORA)),
            _const_spec((hq, Q_LORA)),
            _const_spec((2 * KV_LORA, hq)),
            _const_spec((hv, KV_LORA)),
            _const_spec((hv, 1)),
            pl.BlockSpec((tm, 128), lambda b, i: (i, 0)),
            pl.BlockSpec((tm, 128), lambda b, i: (i, 0)),
            pl.BlockSpec((QK_ROPE // 2, tm), lambda b, i: (0, i)),
            pl.BlockSpec((QK_ROPE // 2, tm), lambda b, i: (0, i)),
        ],
        out_specs=[
            pl.BlockSpec((1, hq, tm), lambda b, i: (b, 0, i)),
            pl.BlockSpec((1, tm, hq), lambda b, i: (b, i, 0)),
            pl.BlockSpec((1, cpt, hv, MLA_TK), lambda b, i: (b, i, 0, 0)),
        ],
        out_shape=[
            jax.ShapeDtypeStruct((B, hq, S), BF16),
            jax.ShapeDtypeStruct((B, S, hq), BF16),
            jax.ShapeDtypeStruct((B, S // MLA_TK, hv, MLA_TK), BF16),
        ],
        compiler_params=_params(2),
        name="mla_pre",
    )(x, mod, gains, w["wdkv"], w["qn"], w["kvn"], w["wqT"], w["wk"], w["wvT"], w["ones"],
      ck, sk, cT, sT)


def _mla_attn_kernel(qT_ref, k_ref, vT_ref, o_ref, *, n_chunks):
    tq = MLA_TQ
    n_qb = qT_ref.shape[2] // tq

    def scores(qb, j, hh):
        k = k_ref[0, j * MLA_TK:(j + 1) * MLA_TK, hh * HEAD_PAD:(hh + 1) * HEAD_PAD]
        return _dot(k, qT_ref[0, hh * HEAD_PAD:(hh + 1) * HEAD_PAD, qb * tq:(qb + 1) * tq])

    steps = [(qb, j) for qb in range(n_qb) for j in range(n_chunks)]
    pending = {}
    for qb, j in steps[:MLA_LOOKAHEAD]:
        for hh in range(2):
            pending[qb, j, hh] = scores(qb, j, hh)
    res = {}
    for i, (qb, j) in enumerate(steps):
        for hh in range(2):
            if i + MLA_LOOKAHEAD < len(steps):
                nqb, nj = steps[i + MLA_LOOKAHEAD]
                pending[nqb, nj, hh] = scores(nqb, nj, hh)
            s = pending.pop((qb, j, hh))
            if j == 0:
                m, acc = jnp.full((1, tq), NEG_INF, F32), jnp.zeros((V_ROWS, tq), F32)
            else:
                m, acc = res[hh]
            m_new = jnp.maximum(m, jnp.max(s, axis=0, keepdims=True))
            alpha = jnp.exp2(m - m_new)
            p = jnp.exp2(s - m_new).astype(BF16)
            vT = vT_ref[0, j, hh * V_ROWS:(hh + 1) * V_ROWS, :]
            res[hh] = (m_new, alpha * acc + _dot(vT, p))
        if j == n_chunks - 1:
            oT = jnp.concatenate([res[hh][1][0:V_DIM] / res[hh][1][V_DIM:V_DIM + 1] for hh in range(2)], axis=0)
            o_ref[0, qb * tq:(qb + 1) * tq, :] = oT.T.astype(BF16)


def _mla_attn(qT, k, vT):
    B, _, S = qT.shape
    n_chunks = S // MLA_TK
    hp = MLA_HEADS // 2
    n_qb = max(g for g in range(1, MLA_QBLOCKS + 1) if (S // MLA_TQ) % g == 0)
    tq_step = MLA_TQ * n_qb
    return pl.pallas_call(
        functools.partial(_mla_attn_kernel, n_chunks=n_chunks),
        grid=(B, hp, S // tq_step),
        in_specs=[
            pl.BlockSpec((1, 2 * HEAD_PAD, tq_step), lambda b, h, i: (b, h, i)),
            pl.BlockSpec((1, S, 2 * HEAD_PAD), lambda b, h, i: (b, 0, h)),
            pl.BlockSpec((1, n_chunks, 2 * V_ROWS, MLA_TK), lambda b, h, i: (b, 0, h, 0)),
        ],
        out_specs=pl.BlockSpec((1, tq_step, 2 * V_DIM), lambda b, h, i: (b, i, h)),
        out_shape=jax.ShapeDtypeStruct((B, S, MLA_HEADS * V_DIM), BF16),
        compiler_params=_params(3),
        name="mla_attn",
    )(qT, k, vT)


def _na_pre_kernel(x_ref, mod_ref, gains_ref, wqT_ref, wk_ref, wvT_ref, ones_ref,
                   qT_out, k_out, vT_out):
    x = x_ref[0]
    shift = mod_ref[0, 0:1, :]
    scale = mod_ref[0, 1:2, :]
    h = (_rms(x, gains_ref[0:1, :]) * (1.0 + scale) + shift).astype(BF16)
    k_out[0] = _dot(h, wk_ref[...]).astype(BF16)
    qT = _dot_nt(wqT_ref[...], h) * (NA_SCALE * LOG2E)
    zpad = jnp.zeros((NA_HEAD_DIM, qT.shape[1]), BF16)
    for hd in range(NA_HEADS):
        lo = hd * HEAD_PAD + (hd % 2) * NA_HEAD_DIM
        zo = hd * HEAD_PAD + (1 - hd % 2) * NA_HEAD_DIM
        qT_out[0, lo:lo + NA_HEAD_DIM, :] = qT[hd * NA_HEAD_DIM:(hd + 1) * NA_HEAD_DIM].astype(BF16)
        qT_out[0, zo:zo + NA_HEAD_DIM, :] = zpad
    vT = _dot_nt(wvT_ref[...], h) + ones_ref[...]
    for c in range(vT.shape[1] // NA_CHUNK):
        vT_out[0, c] = vT[:, c * NA_CHUNK:(c + 1) * NA_CHUNK].astype(BF16)


def _na_pre(x, mod, gains, w):
    B, S, _ = x.shape
    tm = TOKEN_TILE
    cpt = tm // NA_CHUNK
    hq = NA_HEADS * HEAD_PAD
    hk = NA_HEADS * NA_HEAD_DIM
    hv = NA_HEADS * V_ROWS
    return pl.pallas_call(
        _na_pre_kernel,
        grid=(B, S // tm),
        in_specs=[
            pl.BlockSpec((1, tm, D_MODEL), lambda b, i: (b, i, 0)),
            pl.BlockSpec((1, N_MOD, D_MODEL), lambda b, i: (b, 0, 0)),
            _const_spec((4, D_MODEL)),
            _const_spec((hk, D_MODEL)),
            _const_spec((D_MODEL, hk)),
            _const_spec((hv, D_MODEL)),
            _const_spec((hv, 1)),
        ],
        out_specs=[
            pl.BlockSpec((1, hq, tm), lambda b, i: (b, 0, i)),
            pl.BlockSpec((1, tm, hk), lambda b, i: (b, i, 0)),
            pl.BlockSpec((1, cpt, hv, NA_CHUNK), lambda b, i: (b, i, 0, 0)),
        ],
        out_shape=[
            jax.ShapeDtypeStruct((B, hq, S), BF16),
            jax.ShapeDtypeStruct((B, S, hk), BF16),
            jax.ShapeDtypeStruct((B, S // NA_CHUNK, hv, NA_CHUNK), BF16),
        ],
        compiler_params=_params(2),
        name="na_pre",
    )(x, mod, gains, w["wqT"], w["wk"], w["wvT"], w["ones"])


def _na_band_start(blk, n_blk):
    return jnp.clip(blk - 1, 0, n_blk - NA_BAND // NA_QROWS)


def _na_attn_kernel(qT_ref, k_ref, vT_ref, bias_ref, o_ref, *, n_blk, group):
    step = pl.program_id(2)
    tq = NA_QROWS * GRID_W
    n_keys = NA_BAND * GRID_W
    units = [(g, hh) for g in range(group) for hh in range(2)]

    def scores(g, hh):
        blk = step * group + g
        c0 = _na_band_start(blk, n_blk)
        pattern = jnp.where(blk == 0, 0, jnp.where(blk == n_blk - 1, 2, 1))
        qT = qT_ref[0, hh * HEAD_PAD:(hh + 1) * HEAD_PAD, g * tq:(g + 1) * tq]
        tiles = []
        for c in range(n_keys // NA_CHUNK):
            kc = k_ref[0, pl.ds(pl.multiple_of((c0 + c) * NA_CHUNK, NA_CHUNK), NA_CHUNK), :]
            tiles.append(_dot(kc, qT) + bias_ref[pattern, hh, c * NA_CHUNK:(c + 1) * NA_CHUNK, :])
        return c0, tiles

    pending = {}
    for u in units[:NA_LOOKAHEAD]:
        pending[u] = scores(*u)
    outs = {}
    for i, (g, hh) in enumerate(units):
        if i + NA_LOOKAHEAD < len(units):
            nxt = units[i + NA_LOOKAHEAD]
            pending[nxt] = scores(*nxt)
        c0, tiles = pending.pop((g, hh))
        m = functools.reduce(jnp.maximum, [jnp.max(s, axis=0, keepdims=True) for s in tiles])
        acc = None
        for c, s in enumerate(tiles):
            p = jnp.exp2((s - m).astype(BF16))
            vT = vT_ref[0, c0 + c, hh * V_ROWS:(hh + 1) * V_ROWS, :]
            part = _dot(vT, p)
            acc = part if acc is None else acc + part
        outs[g, hh] = acc[0:NA_HEAD_DIM] / acc[NA_HEAD_DIM:NA_HEAD_DIM + 1]
        if hh == 1:
            pair = jnp.concatenate([outs.pop((g, 0)), outs.pop((g, 1))], axis=0)
            o_ref[0, g * tq:(g + 1) * tq, :] = pair.T.astype(BF16)


def _na_attn(qT, k, vT, bias):
    B, _, S = qT.shape
    tq = NA_QROWS * GRID_W
    n_blk = S // tq
    hp = NA_HEADS // 2
    group = max(g for g in range(1, NA_GROUP + 1) if n_blk % g == 0)
    return pl.pallas_call(
        functools.partial(_na_attn_kernel, n_blk=n_blk, group=group),
        grid=(B, hp, n_blk // group),
        in_specs=[
            pl.BlockSpec((1, 2 * HEAD_PAD, group * tq), lambda b, h, i: (b, h, i)),
            pl.BlockSpec((1, S, 2 * NA_HEAD_DIM), lambda b, h, i: (b, 0, h)),
            pl.BlockSpec((1, S // NA_CHUNK, 2 * V_ROWS, NA_CHUNK), lambda b, h, i: (b, 0, h, 0)),
            pl.BlockSpec((3, 2, NA_BAND * GRID_W, tq), lambda b, h, i: (0, h, 0, 0)),
        ],
        out_specs=pl.BlockSpec((1, group * tq, 2 * NA_HEAD_DIM), lambda b, h, i: (b, i, h)),
        out_shape=jax.ShapeDtypeStruct((B, S, NA_HEADS * NA_HEAD_DIM), BF16),
        compiler_params=_params(3),
        name="na_attn",
    )(qT, k, vT, bias)


def _na_bias_tables(rpb):
    p = np.arange(3)[:, None]
    qi = np.arange(NA_QROWS)[None, :]
    qr = NA_QROWS * p + qi
    r_start = np.clip(qr - WIN_R // 2, 0, NA_BAND - WIN_R)
    kr = np.arange(NA_BAND)[None, :, None]
    valid_r = (kr >= r_start[:, None, :]) & (kr < r_start[:, None, :] + WIN_R)
    dr = np.clip(kr - qr[:, None, :] + WIN_R - 1, 0, 2 * WIN_R - 2)
    c = np.arange(GRID_W)
    c_start = np.clip(c - WIN_C // 2, 0, GRID_W - WIN_C)
    kc = c[:, None]
    valid_c = (kc >= c_start[None, :]) & (kc < c_start[None, :] + WIN_C)
    dc = np.clip(kc - c[None, :] + WIN_C - 1, 0, 2 * WIN_C - 2)
    valid = valid_r[:, :, None, :, None] & valid_c[None, None, :, None, :]
    row_sel = (dr.reshape(-1)[:, None] == np.arange(2 * WIN_R - 1)[None, :]).astype(np.float32)
    col_sel = (dc[None] == np.arange(2 * WIN_C - 1)[:, None, None]).astype(np.float32)
    rows = jnp.sum(rpb.astype(F32)[:, None, :, :] * row_sel[None, :, :, None], axis=2)
    planes = jnp.einsum("hnd,dkq->hnkq", rows, col_sel, precision=lax.Precision.HIGHEST)
    planes = planes.reshape(NA_HEADS, 3, NA_BAND, NA_QROWS, GRID_W, GRID_W)
    bias = jnp.transpose(planes, (1, 0, 2, 4, 3, 5))
    bias = jnp.where(valid[:, None], bias * LOG2E, NEG_INF)
    return bias.reshape(3, NA_HEADS, NA_BAND * GRID_W, NA_QROWS * GRID_W)


def _post_ffn_kernel(x_ref, a_ref, mod_ref, gains_ref, wo_ref, wgu_ref, wd_ref, o_ref):
    gate_m = mod_ref[0, 2:3, :]
    shift_f = mod_ref[0, 3:4, :]
    scale_f = mod_ref[0, 4:5, :]
    gate_f = mod_ref[0, 5:6, :]
    n_sub = x_ref.shape[1] // FFN_SUBTILE
    rows = [slice(r * FFN_SUBTILE, (r + 1) * FFN_SUBTILE) for r in range(n_sub)]
    mix = [_dot(a_ref[0, r, :], wo_ref[...]) for r in rows]
    xs, gu, down = [], [], []
    for i in range(n_sub + 2):
        if i < n_sub:
            x = x_ref[0, rows[i], :] + gate_m * _rms(mix[i], gains_ref[1:2, :])
            h = (_rms(x, gains_ref[2:3, :]) * (1.0 + scale_f) + shift_f).astype(BF16)
            xs.append(x)
            gu.append((_dot(h, wgu_ref[:, 0:FFN_HIDDEN]), _dot(h, wgu_ref[:, FFN_HIDDEN:2 * FFN_HIDDEN])))
        if 1 <= i <= n_sub:
            g, u = gu[i - 1]
            act = ((g / (1.0 + jnp.exp(-g))) * u).astype(BF16)
            down.append(_dot(act, wd_ref[...]))
        if i >= 2:
            r = i - 2
            o_ref[0, rows[r], :] = xs[r] + gate_f * _rms(down[r], gains_ref[3:4, :])


def _post_ffn(x, attn, mod, gains, wo, wgu, wd):
    B, S, _ = x.shape
    tm = TOKEN_TILE
    return pl.pallas_call(
        _post_ffn_kernel,
        grid=(B, S // tm),
        in_specs=[
            pl.BlockSpec((1, tm, D_MODEL), lambda b, i: (b, i, 0)),
            pl.BlockSpec((1, tm, D_MODEL), lambda b, i: (b, i, 0)),
            pl.BlockSpec((1, N_MOD, D_MODEL), lambda b, i: (b, 0, 0)),
            _const_spec((4, D_MODEL)),
            _const_spec((D_MODEL, D_MODEL)),
            _const_spec((D_MODEL, 2 * FFN_HIDDEN)),
            _const_spec((FFN_HIDDEN, D_MODEL)),
        ],
        out_specs=pl.BlockSpec((1, tm, D_MODEL), lambda b, i: (b, i, 0)),
        out_shape=jax.ShapeDtypeStruct((B, S, D_MODEL), F32),
        compiler_params=_params(2),
        name="post_ffn",
    )(x, attn, mod, gains, wo, wgu, wd)


def _ones_column(n_heads):
    col = np.zeros((n_heads, V_ROWS, 1), np.float32)
    col[:, V_DIM, 0] = 1.0
    return jnp.asarray(col.reshape(n_heads * V_ROWS, 1))


def _prep_mla(w_dkv, q_norm, kv_norm, w_uq, w_ukv):
    half = QK_ROPE // 2
    r0 = Q_LORA + KV_LORA
    wdkv = jnp.zeros((D_MODEL, LAT_COLS), F32)
    wdkv = wdkv.at[:, :r0 + QK_ROPE].set(w_dkv)
    wdkv = wdkv.at[:, 512:512 + half].set(-w_dkv[:, r0 + half:r0 + QK_ROPE])
    wdkv = wdkv.at[:, 512 + half:512 + QK_ROPE].set(w_dkv[:, r0:r0 + half])
    wq = w_uq.reshape(Q_LORA, MLA_HEADS, QK_NOPE + QK_ROPE)
    wq = jnp.pad(wq, ((0, 0), (0, 0), (0, HEAD_PAD - QK_NOPE - QK_ROPE)))
    wqT = wq.reshape(Q_LORA, MLA_HEADS * HEAD_PAD).T
    wkv = w_ukv.reshape(KV_LORA, MLA_HEADS, QK_NOPE + V_DIM)
    wk_nope = jnp.pad(wkv[:, :, :QK_NOPE], ((0, 0), (0, 0), (0, HEAD_PAD - QK_NOPE)))
    eye = np.zeros((KV_LORA, MLA_HEADS, HEAD_PAD), np.float32)
    for r in range(QK_ROPE):
        eye[r, :, QK_NOPE + r] = 1.0
    wk = jnp.concatenate([wk_nope, jnp.asarray(eye)], axis=0).reshape(2 * KV_LORA, MLA_HEADS * HEAD_PAD)
    wv = jnp.pad(wkv[:, :, QK_NOPE:], ((0, 0), (0, 0), (0, V_ROWS - V_DIM)))
    wvT = wv.reshape(KV_LORA, MLA_HEADS * V_ROWS).T
    return dict(wdkv=wdkv.astype(BF16), qn=q_norm.reshape(1, Q_LORA), kvn=kv_norm.reshape(1, KV_LORA),
                wqT=wqT.astype(BF16), wk=wk.astype(BF16), wvT=wvT.astype(BF16), ones=_ones_column(MLA_HEADS))


def _prep_na(w_qkv):
    hk = NA_HEADS * NA_HEAD_DIM
    wv = w_qkv[:, 2 * hk:].reshape(D_MODEL, NA_HEADS, NA_HEAD_DIM)
    wv = jnp.pad(wv, ((0, 0), (0, 0), (0, V_ROWS - NA_HEAD_DIM)))
    return dict(wqT=w_qkv[:, :hk].T.astype(BF16), wk=w_qkv[:, hk:2 * hk].astype(BF16),
                wvT=wv.reshape(D_MODEL, NA_HEADS * V_ROWS).T.astype(BF16), ones=_ones_column(NA_HEADS))


def _rope_tables(seq_len):
    half = QK_ROPE // 2
    inv_freq = 1.0 / (ROPE_THETA ** (jnp.arange(0, QK_ROPE, 2, dtype=F32) / QK_ROPE))
    ang = jnp.arange(seq_len, dtype=F32)[:, None] * inv_freq[None, :]
    cos, sin = jnp.cos(ang), jnp.sin(ang)
    pad = jnp.zeros((seq_len, 128 - 2 * half), F32)
    ck = jnp.concatenate([cos, cos, pad], axis=1)
    sk = jnp.concatenate([sin, sin, pad], axis=1)
    return ck, sk, cos.T, sin.T


def _trunk(x, mods, gains, mla_w, na_w, na_bias, post_w):
    tables = _rope_tables(x.shape[1])
    for i in range(DEPTH):
        if i % 2 == 0:
            qT, k, vT = _mla_pre(x, mods[i], gains[i], mla_w[i // 2], tables)
            attn = _mla_attn(qT, k, vT)
        else:
            qT, k, vT = _na_pre(x, mods[i], gains[i], na_w[i // 2])
            attn = _na_attn(qT, k, vT, na_bias[i // 2])
        x = _post_ffn(x, attn, mods[i], gains[i], *post_w[i])
    return x


def kernel(x_prompt, x_sample, c_prompt, c_sample, ada_w, ada_b, norm_pre_mix, norm_post_mix, norm_pre_ffn, norm_post_ffn, mla_w_dkv, mla_q_norm, mla_kv_norm, mla_w_uq, mla_w_ukv, mla_w_o, na_w_qkv, na_rpb, na_w_o, ffn_w_gu, ffn_w_down):
    bp = x_prompt.shape[0]
    bs = x_sample.shape[0]
    mod = _ada_mod(jnp.concatenate([c_prompt, c_sample], axis=0), ada_w, ada_b)
    mod = mod.reshape(DEPTH, bp + bs, N_MOD, D_MODEL)
    gains = [jnp.stack([norm_pre_mix[i], norm_post_mix[i], norm_pre_ffn[i], norm_post_ffn[i]]) for i in range(DEPTH)]
    mla_w = [_prep_mla(mla_w_dkv[j], mla_q_norm[j], mla_kv_norm[j], mla_w_uq[j], mla_w_ukv[j])
             for j in range(mla_w_dkv.shape[0])]
    na_w = [_prep_na(na_w_qkv[j]) for j in range(na_w_qkv.shape[0])]
    na_bias = [_na_bias_tables(na_rpb[j]) for j in range(na_rpb.shape[0])]
    post_w = []
    for i in range(DEPTH):
        wo = mla_w_o[i // 2] if i % 2 == 0 else na_w_o[i // 2]
        post_w.append((wo.astype(BF16), ffn_w_gu[i].astype(BF16), ffn_w_down[i].astype(BF16)))
    y_prompt = _trunk(x_prompt, [mod[i, :bp] for i in range(DEPTH)], gains, mla_w, na_w, na_bias, post_w)
    y_sample = _trunk(x_sample, [mod[i, bp:] for i in range(DEPTH)], gains, mla_w, na_w, na_bias, post_w)
    return (y_prompt, y_sample)
```

```python
import functools

import jax
import jax.numpy as jnp
import numpy as np
from jax import lax
from jax.experimental import pallas as pl
from jax.experimental.pallas import tpu as pltpu

F32 = jnp.float32
BF16 = jnp.bfloat16

D_MODEL = 1024
DEPTH = 2
N_MOD = 6
RMS_EPS = 1e-6
NEG_INF = -1e30

MLA_HEADS = 16
Q_LORA = 256
KV_LORA = 128
QK_NOPE = 64
QK_ROPE = 32
V_DIM = 64
ROPE_THETA = 10000.0
MLA_SCALE = (QK_NOPE + QK_ROPE) ** -0.5
LOG2E = 1.4426950408889634
HEAD_PAD = 128
V_ROWS = 80
LAT_COLS = 640

NA_HEADS = 16
NA_HEAD_DIM = 64
GRID_W = 64
WIN_R = 8
WIN_C = 16
NA_QROWS = 4
NA_BAND = 12
NA_SCALE = NA_HEAD_DIM ** -0.5
NA_GROUP = 16
NA_LOOKAHEAD = 3

FFN_HIDDEN = 2816
FFN_SUBTILE = 256

TOKEN_TILE = 512
MLA_TQ = 256
MLA_QBLOCKS = 4
MLA_TK = 256
MLA_LOOKAHEAD = 3
NA_CHUNK = 256

VMEM_LIMIT = 56 * 1024 * 1024

_NT = (((1,), (1,)), ((), ()))


def _dot(a, b):
    return jnp.dot(a, b, preferred_element_type=F32)


def _dot_nt(a, b):
    return lax.dot_general(a, b, _NT, preferred_element_type=F32)


def _rms(x, g):
    ms = jnp.mean(x * x, axis=-1, keepdims=True)
    return x * lax.rsqrt(ms + RMS_EPS) * g


def _const_spec(shape):
    zeros = (0,) * len(shape)
    return pl.BlockSpec(shape, lambda *_: zeros, pipeline_mode=pl.Buffered(1))


def _params(n_axes):
    return pltpu.CompilerParams(
        dimension_semantics=("arbitrary",) * n_axes, vmem_limit_bytes=VMEM_LIMIT)


def _ada_kernel(c_ref, w_ref, b_ref, o_ref):
    c = c_ref[...]
    c_act = c / (1.0 + jnp.exp(-c))
    o_ref[0] = _dot(c_act.astype(BF16), w_ref[0]) + b_ref[0]


def _ada_mod(c_all, ada_w, ada_b):
    n_rows = c_all.shape[0]
    n_out = N_MOD * D_MODEL
    tn = 1536
    return pl.pallas_call(
        _ada_kernel,
        grid=(DEPTH, n_out // tn),
        in_specs=[
            pl.BlockSpec((n_rows, D_MODEL), lambda i, j: (0, 0)),
            pl.BlockSpec((1, D_MODEL, tn), lambda i, j: (i, 0, j)),
            pl.BlockSpec((1, 1, tn), lambda i, j: (i, 0, j)),
        ],
        out_specs=pl.BlockSpec((1, n_rows, tn), lambda i, j: (i, 0, j)),
        out_shape=jax.ShapeDtypeStruct((DEPTH, n_rows, n_out), F32),
        compiler_params=_params(2),
        name="ada_mod",
    )(c_all, ada_w.astype(BF16), ada_b.reshape(DEPTH, 1, n_out))


def _mla_pre_kernel(x_ref, mod_ref, gains_ref, wdkv_ref, qn_ref, kvn_ref, wqT_ref, wk_ref,
                    wvT_ref, ones_ref, ck_ref, sk_ref, cT_ref, sT_ref, qT_out, k_out, vT_out):
    x = x_ref[0]
    shift = mod_ref[0, 0:1, :]
    scale = mod_ref[0, 1:2, :]
    h = (_rms(x, gains_ref[0:1, :]) * (1.0 + scale) + shift).astype(BF16)
    lat = _dot(h, wdkv_ref[...])
    cq = _rms(lat[:, 0:Q_LORA], qn_ref[...]).astype(BF16)
    ckv = _rms(lat[:, Q_LORA:Q_LORA + KV_LORA], kvn_ref[...]).astype(BF16)
    kr = lat[:, 384:512] * ck_ref[...] + lat[:, 512:640] * sk_ref[...]
    kin = jnp.concatenate([ckv, kr.astype(BF16)], axis=1)
    k_out[0] = _dot(kin, wk_ref[...]).astype(BF16)

    qT = _dot_nt(wqT_ref[...], cq) * (MLA_SCALE * LOG2E)
    cT = cT_ref[...]
    sT = sT_ref[...]
    zpad = jnp.zeros((HEAD_PAD - QK_NOPE - QK_ROPE, qT.shape[1]), BF16)
    half = QK_ROPE // 2
    for hd in range(MLA_HEADS):
        b0 = hd * HEAD_PAD
        x1 = qT[b0 + QK_NOPE:b0 + QK_NOPE + half]
        x2 = qT[b0 + QK_NOPE + half:b0 + QK_NOPE + QK_ROPE]
        qT_out[0, b0:b0 + QK_NOPE, :] = qT[b0:b0 + QK_NOPE].astype(BF16)
        qT_out[0, b0 + QK_NOPE:b0 + QK_NOPE + half, :] = (x1 * cT - x2 * sT).astype(BF16)
        qT_out[0, b0 + QK_NOPE + half:b0 + QK_NOPE + QK_ROPE, :] = (x2 * cT + x1 * sT).astype(BF16)
        qT_out[0, b0 + QK_NOPE + QK_ROPE:b0 + HEAD_PAD, :] = zpad

    vT = _dot_nt(wvT_ref[...], ckv) + ones_ref[...]
    for c in range(vT.shape[1] // MLA_TK):
        vT_out[0, c] = vT[:, c * MLA_TK:(c + 1) * MLA_TK].astype(BF16)


def _mla_pre(x, mod, gains, w, tables):
    B, S, _ = x.shape
    tm = TOKEN_TILE
    n_t = S // tm
    cpt = tm // MLA_TK
    ck, sk, cT, sT = tables
    hq = MLA_HEADS * HEAD_PAD
    hv = MLA_HEADS * V_ROWS
    return pl.pallas_call(
        _mla_pre_kernel,
        grid=(B, n_t),
        in_specs=[
            pl.BlockSpec((1, tm, D_MODEL), lambda b, i: (b, i, 0)),
            pl.BlockSpec((1, N_MOD, D_MODEL), lambda b, i: (b, 0, 0)),
            _const_spec((4, D_MODEL)),
            _const_spec((D_MODEL, LAT_COLS)),
            _const_spec((1, Q_LORA)),
            _const_spec((1, KV_LORA)),
            _const_spec((hq, Q_LORA)),
            _const_spec((2 * KV_LORA, hq)),
            _const_spec((hv, KV_LORA)),
            _const_spec((hv, 1)),
            pl.BlockSpec((tm, 128), lambda b, i: (i, 0)),
            pl.BlockSpec((tm, 128), lambda b, i: (i, 0)),
            pl.BlockSpec((QK_ROPE // 2, tm), lambda b, i: (0, i)),
            pl.BlockSpec((QK_ROPE // 2, tm), lambda b, i: (0, i)),
        ],
        out_specs=[
            pl.BlockSpec((1, hq, tm), lambda b, i: (b, 0, i)),
            pl.BlockSpec((1, tm, hq), lambda b, i: (b, i, 0)),
            pl.BlockSpec((1, cpt, hv, MLA_TK), lambda b, i: (b, i, 0, 0)),
        ],
        out_shape=[
            jax.ShapeDtypeStruct((B, hq, S), BF16),
            jax.ShapeDtypeStruct((B, S, hq), BF16),
            jax.ShapeDtypeStruct((B, S // MLA_TK, hv, MLA_TK), BF16),
        ],
        compiler_params=_params(2),
        name="mla_pre",
    )(x, mod, gains, w["wdkv"], w["qn"], w["kvn"], w["wqT"], w["wk"], w["wvT"], w["ones"],
      ck, sk, cT, sT)


def _mla_attn_kernel(qT_ref, k_ref, vT_ref, o_ref, *, n_chunks):
    tq = MLA_TQ
    n_qb = qT_ref.shape[2] // tq

    def scores(qb, j, hh):
        k = k_ref[0, j * MLA_TK:(j + 1) * MLA_TK, hh * HEAD_PAD:(hh + 1) * HEAD_PAD]
        return _dot(k, qT_ref[0, hh * HEAD_PAD:(hh + 1) * HEAD_PAD, qb * tq:(qb + 1) * tq])

    steps = [(qb, j) for qb in range(n_qb) for j in range(n_chunks)]
    pending = {}
    for qb, j in steps[:MLA_LOOKAHEAD]:
        for hh in range(2):
            pending[qb, j, hh] = scores(qb, j, hh)
    res = {}
    for i, (qb, j) in enumerate(steps):
        for hh in range(2):
            if i + MLA_LOOKAHEAD < len(steps):
                nqb, nj = steps[i + MLA_LOOKAHEAD]
                pending[nqb, nj, hh] = scores(nqb, nj, hh)
            s = pending.pop((qb, j, hh))
            if j == 0:
                m, acc = jnp.full((1, tq), NEG_INF, F32), jnp.zeros((V_ROWS, tq), F32)
            else:
                m, acc = res[hh]
            m_new = jnp.maximum(m, jnp.max(s, axis=0, keepdims=True))
            alpha = jnp.exp2(m - m_new)
            p = jnp.exp2(s - m_new).astype(BF16)
            vT = vT_ref[0, j, hh * V_ROWS:(hh + 1) * V_ROWS, :]
            res[hh] = (m_new, alpha * acc + _dot(vT, p))
        if j == n_chunks - 1:
            oT = jnp.concatenate([res[hh][1][0:V_DIM] / res[hh][1][V_DIM:V_DIM + 1] for hh in range(2)], axis=0)
            o_ref[0, qb * tq:(qb + 1) * tq, :] = oT.T.astype(BF16)


def _mla_attn(qT, k, vT):
    B, _, S = qT.shape
    n_chunks = S // MLA_TK
    hp = MLA_HEADS // 2
    n_qb = max(g for g in range(1, MLA_QBLOCKS + 1) if (S // MLA_TQ) % g == 0)
    tq_step = MLA_TQ * n_qb
    return pl.pallas_call(
        functools.partial(_mla_attn_kernel, n_chunks=n_chunks),
        grid=(B, hp, S // tq_step),
        in_specs=[
            pl.BlockSpec((1, 2 * HEAD_PAD, tq_step), lambda b, h, i: (b, h, i)),
            pl.BlockSpec((1, S, 2 * HEAD_PAD), lambda b, h, i: (b, 0, h)),
            pl.BlockSpec((1, n_chunks, 2 * V_ROWS, MLA_TK), lambda b, h, i: (b, 0, h, 0)),
        ],
        out_specs=pl.BlockSpec((1, tq_step, 2 * V_DIM), lambda b, h, i: (b, i, h)),
        out_shape=jax.ShapeDtypeStruct((B, S, MLA_HEADS * V_DIM), BF16),
        compiler_params=_params(3),
        name="mla_attn",
    )(qT, k, vT)


def _na_pre_kernel(x_ref, mod_ref, gains_ref, wqT_ref, wk_ref, wvT_ref, ones_ref,
                   qT_out, k_out, vT_out):
    x = x_ref[0]
    shift = mod_ref[0, 0:1, :]
    scale = mod_ref[0, 1:2, :]
    h = (_rms(x, gains_ref[0:1, :]) * (1.0 + scale) + shift).astype(BF16)
    k_out[0] = _dot(h, wk_ref[...]).astype(BF16)
    qT = _dot_nt(wqT_ref[...], h) * (NA_SCALE * LOG2E)
    zpad = jnp.zeros((NA_HEAD_DIM, qT.shape[1]), BF16)
    for hd in range(NA_HEADS):
        lo = hd * HEAD_PAD + (hd % 2) * NA_HEAD_DIM
        zo = hd * HEAD_PAD + (1 - hd % 2) * NA_HEAD_DIM
        qT_out[0, lo:lo + NA_HEAD_DIM, :] = qT[hd * NA_HEAD_DIM:(hd + 1) * NA_HEAD_DIM].astype(BF16)
        qT_out[0, zo:zo + NA_HEAD_DIM, :] = zpad
    vT = _dot_nt(wvT_ref[...], h) + ones_ref[...]
    for c in range(vT.shape[1] // NA_CHUNK):
        vT_out[0, c] = vT[:, c * NA_CHUNK:(c + 1) * NA_CHUNK].astype(BF16)


def _na_pre(x, mod, gains, w):
    B, S, _ = x.shape
    tm = TOKEN_TILE
    cpt = tm // NA_CHUNK
    hq = NA_HEADS * HEAD_PAD
    hk = NA_HEADS * NA_HEAD_DIM
    hv = NA_HEADS * V_ROWS
    return pl.pallas_call(
        _na_pre_kernel,
        grid=(B, S // tm),
        in_specs=[
            pl.BlockSpec((1, tm, D_MODEL), lambda b, i: (b, i, 0)),
            pl.BlockSpec((1, N_MOD, D_MODEL), lambda b, i: (b, 0, 0)),
            _const_spec((4, D_MODEL)),
            _const_spec((hk, D_MODEL)),
            _const_spec((D_MODEL, hk)),
            _const_spec((hv, D_MODEL)),
            _const_spec((hv, 1)),
        ],
        out_specs=[
            pl.BlockSpec((1, hq, tm), lambda b, i: (b, 0, i)),
            pl.BlockSpec((1, tm, hk), lambda b, i: (b, i, 0)),
            pl.BlockSpec((1, cpt, hv, NA_CHUNK), lambda b, i: (b, i, 0, 0)),
        ],
        out_shape=[
            jax.ShapeDtypeStruct((B, hq, S), BF16),
            jax.ShapeDtypeStruct((B, S, hk), BF16),
            jax.ShapeDtypeStruct((B, S // NA_CHUNK, hv, NA_CHUNK), BF16),
        ],
        compiler_params=_params(2),
        name="na_pre",
    )(x, mod, gains, w["wqT"], w["wk"], w["wvT"], w["ones"])


def _na_band_start(blk, n_blk):
    return jnp.clip(blk - 1, 0, n_blk - NA_BAND // NA_QROWS)


def _na_attn_kernel(qT_ref, k_ref, vT_ref, bias_ref, o_ref, *, n_blk, group):
    step = pl.program_id(2)
    tq = NA_QROWS * GRID_W
    n_keys = NA_BAND * GRID_W
    n_chunks = n_keys // NA_CHUNK
    band0 = [_na_band_start(step * group + g, n_blk) for g in range(group)]
    pattern = [jnp.where(step * group + g == 0, 0, jnp.where(step * group + g == n_blk - 1, 2, 1))
               for g in range(group)]

    def scores(g, c, hh):
        kc = k_ref[0, pl.ds(pl.multiple_of((band0[g] + c) * NA_CHUNK, NA_CHUNK), NA_CHUNK), :]
        qT = qT_ref[0, hh * HEAD_PAD:(hh + 1) * HEAD_PAD, g * tq:(g + 1) * tq]
        return _dot(kc, qT) + bias_ref[pattern[g], hh, c * NA_CHUNK:(c + 1) * NA_CHUNK, :]

    steps = [(g, c) for g in range(group) for c in range(n_chunks)]
    pending = {}
    for g, c in steps[:NA_LOOKAHEAD]:
        for hh in range(2):
            pending[g, c, hh] = scores(g, c, hh)
    res = {}
    for i, (g, c) in enumerate(steps):
        for hh in range(2):
            if i + NA_LOOKAHEAD < len(steps):
                ng, nc = steps[i + NA_LOOKAHEAD]
                pending[ng, nc, hh] = scores(ng, nc, hh)
            s = pending.pop((g, c, hh))
            if c == 0:
                m, acc = jnp.full((1, tq), NEG_INF, F32), jnp.zeros((V_ROWS, tq), F32)
            else:
                m, acc = res[hh]
            m_new = jnp.maximum(m, jnp.max(s, axis=0, keepdims=True))
            alpha = jnp.exp2(m - m_new)
            p = jnp.exp2((s - m_new).astype(BF16))
            vT = vT_ref[0, band0[g] + c, hh * V_ROWS:(hh + 1) * V_ROWS, :]
            res[hh] = (m_new, alpha * acc + _dot(vT, p))
        if c == n_chunks - 1:
            pair = jnp.concatenate([res[hh][1][0:NA_HEAD_DIM] / res[hh][1][NA_HEAD_DIM:NA_HEAD_DIM + 1]
                                    for hh in range(2)], axis=0)
            o_ref[0, g * tq:(g + 1) * tq, :] = pair.T.astype(BF16)


def _na_attn(qT, k, vT, bias):
    B, _, S = qT.shape
    tq = NA_QROWS * GRID_W
    n_blk = S // tq
    hp = NA_HEADS // 2
    group = max(g for g in range(1, NA_GROUP + 1) if n_blk % g == 0)
    return pl.pallas_call(
        functools.partial(_na_attn_kernel, n_blk=n_blk, group=group),
        grid=(B, hp, n_blk // group),
        in_specs=[
            pl.BlockSpec((1, 2 * HEAD_PAD, group * tq), lambda b, h, i: (b, h, i)),
            pl.BlockSpec((1, S, 2 * NA_HEAD_DIM), lambda b, h, i: (b, 0, h)),
            pl.BlockSpec((1, S // NA_CHUNK, 2 * V_ROWS, NA_CHUNK), lambda b, h, i: (b, 0, h, 0)),
            pl.BlockSpec((3, 2, NA_BAND * GRID_W, tq), lambda b, h, i: (0, h, 0, 0)),
        ],
        out_specs=pl.BlockSpec((1, group * tq, 2 * NA_HEAD_DIM), lambda b, h, i: (b, i, h)),
        out_shape=jax.ShapeDtypeStruct((B, S, NA_HEADS * NA_HEAD_DIM), BF16),
        compiler_params=_params(3),
        name="na_attn",
    )(qT, k, vT, bias)


def _na_bias_tables(rpb):
    p = np.arange(3)[:, None]
    qi = np.arange(NA_QROWS)[None, :]
    qr = NA_QROWS * p + qi
    r_start = np.clip(qr - WIN_R // 2, 0, NA_BAND - WIN_R)
    kr = np.arange(NA_BAND)[None, :, None]
    valid_r = (kr >= r_start[:, None, :]) & (kr < r_start[:, None, :] + WIN_R)
    dr = np.clip(kr - qr[:, None, :] + WIN_R - 1, 0, 2 * WIN_R - 2)
    c = np.arange(GRID_W)
    c_start = np.clip(c - WIN_C // 2, 0, GRID_W - WIN_C)
    kc = c[:, None]
    valid_c = (kc >= c_start[None, :]) & (kc < c_start[None, :] + WIN_C)
    dc = np.clip(kc - c[None, :] + WIN_C - 1, 0, 2 * WIN_C - 2)
    valid = valid_r[:, :, None, :, None] & valid_c[None, None, :, None, :]
    row_sel = (dr.reshape(-1)[:, None] == np.arange(2 * WIN_R - 1)[None, :]).astype(np.float32)
    col_sel = (dc[None] == np.arange(2 * WIN_C - 1)[:, None, None]).astype(np.float32)
    rows = jnp.sum(rpb.astype(F32)[:, None, :, :] * row_sel[None, :, :, None], axis=2)
    planes = jnp.einsum("hnd,dkq->hnkq", rows, col_sel, precision=lax.Precision.HIGHEST)
    planes = planes.reshape(NA_HEADS, 3, NA_BAND, NA_QROWS, GRID_W, GRID_W)
    bias = jnp.transpose(planes, (1, 0, 2, 4, 3, 5))
    bias = jnp.where(valid[:, None], bias * LOG2E, NEG_INF)
    return bias.reshape(3, NA_HEADS, NA_BAND * GRID_W, NA_QROWS * GRID_W)


def _post_ffn_kernel(x_ref, a_ref, mod_ref, gains_ref, wo_ref, wgu_ref, wd_ref, o_ref):
    gate_m = mod_ref[0, 2:3, :]
    shift_f = mod_ref[0, 3:4, :]
    scale_f = mod_ref[0, 4:5, :]
    gate_f = mod_ref[0, 5:6, :]
    n_sub = x_ref.shape[1] // FFN_SUBTILE
    rows = [slice(r * FFN_SUBTILE, (r + 1) * FFN_SUBTILE) for r in range(n_sub)]
    mix = [_dot(a_ref[0, r, :], wo_ref[...]) for r in rows]
    xs, gu, down = [], [], []
    for i in range(n_sub + 2):
        if i < n_sub:
            x = x_ref[0, rows[i], :] + gate_m * _rms(mix[i], gains_ref[1:2, :])
            h = (_rms(x, gains_ref[2:3, :]) * (1.0 + scale_f) + shift_f).astype(BF16)
            xs.append(x)
            gu.append((_dot(h, wgu_ref[:, 0:FFN_HIDDEN]), _dot(h, wgu_ref[:, FFN_HIDDEN:2 * FFN_HIDDEN])))
        if 1 <= i <= n_sub:
            g, u = gu[i - 1]
            act = ((g / (1.0 + jnp.exp(-g))) * u).astype(BF16)
            down.append(_dot(act, wd_ref[...]))
        if i >= 2:
            r = i - 2
            o_ref[0, rows[r], :] = xs[r] + gate_f * _rms(down[r], gains_ref[3:4, :])


def _post_ffn(x, attn, mod, gains, wo, wgu, wd):
    B, S, _ = x.shape
    tm = TOKEN_TILE
    return pl.pallas_call(
        _post_ffn_kernel,
        grid=(B, S // tm),
        in_specs=[
            pl.BlockSpec((1, tm, D_MODEL), lambda b, i: (b, i, 0)),
            pl.BlockSpec((1, tm, D_MODEL), lambda b, i: (b, i, 0)),
            pl.BlockSpec((1, N_MOD, D_MODEL), lambda b, i: (b, 0, 0)),
            _const_spec((4, D_MODEL)),
            _const_spec((D_MODEL, D_MODEL)),
            _const_spec((D_MODEL, 2 * FFN_HIDDEN)),
            _const_spec((FFN_HIDDEN, D_MODEL)),
        ],
        out_specs=pl.BlockSpec((1, tm, D_MODEL), lambda b, i: (b, i, 0)),
        out_shape=jax.ShapeDtypeStruct((B, S, D_MODEL), F32),
        compiler_params=_params(2),
        name="post_ffn",
    )(x, attn, mod, gains, wo, wgu, wd)


def _ones_column(n_heads):
    col = np.zeros((n_heads, V_ROWS, 1), np.float32)
    col[:, V_DIM, 0] = 1.0
    return jnp.asarray(col.reshape(n_heads * V_ROWS, 1))


def _prep_mla(w_dkv, q_norm, kv_norm, w_uq, w_ukv):
    half = QK_ROPE // 2
    r0 = Q_LORA + KV_LORA
    wdkv = jnp.zeros((D_MODEL, LAT_COLS), F32)
    wdkv = wdkv.at[:, :r0 + QK_ROPE].set(w_dkv)
    wdkv = wdkv.at[:, 512:512 + half].set(-w_dkv[:, r0 + half:r0 + QK_ROPE])
    wdkv = wdkv.at[:, 512 + half:512 + QK_ROPE].set(w_dkv[:, r0:r0 + half])
    wq = w_uq.reshape(Q_LORA, MLA_HEADS, QK_NOPE + QK_ROPE)
    wq = jnp.pad(wq, ((0, 0), (0, 0), (0, HEAD_PAD - QK_NOPE - QK_ROPE)))
    wqT = wq.reshape(Q_LORA, MLA_HEADS * HEAD_PAD).T
    wkv = w_ukv.reshape(KV_LORA, MLA_HEADS, QK_NOPE + V_DIM)
    wk_nope = jnp.pad(wkv[:, :, :QK_NOPE], ((0, 0), (0, 0), (0, HEAD_PAD - QK_NOPE)))
    eye = np.zeros((KV_LORA, MLA_HEADS, HEAD_PAD), np.float32)
    for r in range(QK_ROPE):
        eye[r, :, QK_NOPE + r] = 1.0
    wk = jnp.concatenate([wk_nope, jnp.asarray(eye)], axis=0).reshape(2 * KV_LORA, MLA_HEADS * HEAD_PAD)
    wv = jnp.pad(wkv[:, :, QK_NOPE:], ((0, 0), (0, 0), (0, V_ROWS - V_DIM)))
    wvT = wv.reshape(KV_LORA, MLA_HEADS * V_ROWS).T
    return dict(wdkv=wdkv.astype(BF16), qn=q_norm.reshape(1, Q_LORA), kvn=kv_norm.reshape(1, KV_LORA),
                wqT=wqT.astype(BF16), wk=wk.astype(BF16), wvT=wvT.astype(BF16), ones=_ones_column(MLA_HEADS))


def _prep_na(w_qkv):
    hk = NA_HEADS * NA_HEAD_DIM
    wv = w_qkv[:, 2 * hk:].reshape(D_MODEL, NA_HEADS, NA_HEAD_DIM)
    wv = jnp.pad(wv, ((0, 0), (0, 0), (0, V_ROWS - NA_HEAD_DIM)))
    return dict(wqT=w_qkv[:, :hk].T.astype(BF16), wk=w_qkv[:, hk:2 * hk].astype(BF16),
                wvT=wv.reshape(D_MODEL, NA_HEADS * V_ROWS).T.astype(BF16), ones=_ones_column(NA_HEADS))


def _rope_tables(seq_len):
    half = QK_ROPE // 2
    inv_freq = 1.0 / (ROPE_THETA ** (jnp.arange(0, QK_ROPE, 2, dtype=F32) / QK_ROPE))
    ang = jnp.arange(seq_len, dtype=F32)[:, None] * inv_freq[None, :]
    cos, sin = jnp.cos(ang), jnp.sin(ang)
    pad = jnp.zeros((seq_len, 128 - 2 * half), F32)
    ck = jnp.concatenate([cos, cos, pad], axis=1)
    sk = jnp.concatenate([sin, sin, pad], axis=1)
    return ck, sk, cos.T, sin.T


def _trunk(x, mods, gains, mla_w, na_w, na_bias, post_w):
    tables = _rope_tables(x.shape[1])
    for i in range(DEPTH):
        if i % 2 == 0:
            qT, k, vT = _mla_pre(x, mods[i], gains[i], mla_w[i // 2], tables)
            attn = _mla_attn(qT, k, vT)
        else:
            qT, k, vT = _na_pre(x, mods[i], gains[i], na_w[i // 2])
            attn = _na_attn(qT, k, vT, na_bias[i // 2])
        x = _post_ffn(x, attn, mods[i], gains[i], *post_w[i])
    return x


def kernel(x_prompt, x_sample, c_prompt, c_sample, ada_w, ada_b, norm_pre_mix, norm_post_mix, norm_pre_ffn, norm_post_ffn, mla_w_dkv, mla_q_norm, mla_kv_norm, mla_w_uq, mla_w_ukv, mla_w_o, na_w_qkv, na_rpb, na_w_o, ffn_w_gu, ffn_w_down):
    bp = x_prompt.shape[0]
    bs = x_sample.shape[0]
    mod = _ada_mod(jnp.concatenate([c_prompt, c_sample], axis=0), ada_w, ada_b)
    mod = mod.reshape(DEPTH, bp + bs, N_MOD, D_MODEL)
    gains = [jnp.stack([norm_pre_mix[i], norm_post_mix[i], norm_pre_ffn[i], norm_post_ffn[i]]) for i in range(DEPTH)]
    mla_w = [_prep_mla(mla_w_dkv[j], mla_q_norm[j], mla_kv_norm[j], mla_w_uq[j], mla_w_ukv[j])
             for j in range(mla_w_dkv.shape[0])]
    na_w = [_prep_na(na_w_qkv[j]) for j in range(na_w_qkv.shape[0])]
    na_bias = [_na_bias_tables(na_rpb[j]) for j in range(na_rpb.shape[0])]
    post_w = []
    for i in range(DEPTH):
        wo = mla_w_o[i // 2] if i % 2 == 0 else na_w_o[i // 2]
        post_w.append((wo.astype(BF16), ffn_w_gu[i].astype(BF16), ffn_w_down[i].astype(BF16)))
    y_prompt = _trunk(x_prompt, [mod[i, :bp] for i in range(DEPTH)], gains, mla_w, na_w, na_bias, post_w)
    y_sample = _trunk(x_sample, [mod[i, bp:] for i in range(DEPTH)], gains, mla_w, na_w, na_bias, post_w)
    return (y_prompt, y_sample)
```

```python
import functools

import jax
import jax.numpy as jnp
import numpy as np
from jax import lax
from jax.experimental import pallas as pl
from jax.experimental.pallas import tpu as pltpu

F32 = jnp.float32
BF16 = jnp.bfloat16

D_MODEL = 1024
DEPTH = 2
N_MOD = 6
RMS_EPS = 1e-6
NEG_INF = -1e30

MLA_HEADS = 16
Q_LORA = 256
KV_LORA = 128
QK_NOPE = 64
QK_ROPE = 32
V_DIM = 64
ROPE_THETA = 10000.0
MLA_SCALE = (QK_NOPE + QK_ROPE) ** -0.5
LOG2E = 1.4426950408889634
HEAD_PAD = 128
V_ROWS = 80
LAT_COLS = 640

NA_HEADS = 16
NA_HEAD_DIM = 64
GRID_W = 64
WIN_R = 8
WIN_C = 16
NA_QROWS = 4
NA_BAND = 12
NA_SCALE = NA_HEAD_DIM ** -0.5
NA_GROUP = 16
NA_LOOKAHEAD = 3

FFN_HIDDEN = 2816
FFN_SUBTILE = 256

TOKEN_TILE = 512
MLA_TQ = 256
MLA_QBLOCKS = 8
MLA_TK = 256
MLA_LOOKAHEAD = 3
NA_CHUNK = 256

VMEM_LIMIT = 56 * 1024 * 1024

_NT = (((1,), (1,)), ((), ()))


def _dot(a, b):
    return jnp.dot(a, b, preferred_element_type=F32)


def _dot_nt(a, b):
    return lax.dot_general(a, b, _NT, preferred_element_type=F32)


def _rms(x, g):
    ms = jnp.mean(x * x, axis=-1, keepdims=True)
    return x * lax.rsqrt(ms + RMS_EPS) * g


def _const_spec(shape):
    zeros = (0,) * len(shape)
    return pl.BlockSpec(shape, lambda *_: zeros, pipeline_mode=pl.Buffered(1))


def _params(n_axes):
    return pltpu.CompilerParams(
        dimension_semantics=("arbitrary",) * n_axes, vmem_limit_bytes=VMEM_LIMIT)


def _ada_kernel(c_ref, w_ref, b_ref, o_ref):
    c = c_ref[...]
    c_act = c / (1.0 + jnp.exp(-c))
    o_ref[0] = _dot(c_act.astype(BF16), w_ref[0]) + b_ref[0]


def _ada_mod(c_all, ada_w, ada_b):
    n_rows = c_all.shape[0]
    n_out = N_MOD * D_MODEL
    tn = 1536
    return pl.pallas_call(
        _ada_kernel,
        grid=(DEPTH, n_out // tn),
        in_specs=[
            pl.BlockSpec((n_rows, D_MODEL), lambda i, j: (0, 0)),
            pl.BlockSpec((1, D_MODEL, tn), lambda i, j: (i, 0, j)),
            pl.BlockSpec((1, 1, tn), lambda i, j: (i, 0, j)),
        ],
        out_specs=pl.BlockSpec((1, n_rows, tn), lambda i, j: (i, 0, j)),
        out_shape=jax.ShapeDtypeStruct((DEPTH, n_rows, n_out), F32),
        compiler_params=_params(2),
        name="ada_mod",
    )(c_all, ada_w.astype(BF16), ada_b.reshape(DEPTH, 1, n_out))


def _mla_pre_kernel(x_ref, mod_ref, gains_ref, wdkv_ref, qn_ref, kvn_ref, wqT_ref, wk_ref,
                    wvT_ref, ones_ref, ck_ref, sk_ref, cT_ref, sT_ref, qT_out, k_out, vT_out):
    x = x_ref[0]
    shift = mod_ref[0, 0:1, :]
    scale = mod_ref[0, 1:2, :]
    h = (_rms(x, gains_ref[0:1, :]) * (1.0 + scale) + shift).astype(BF16)
    lat = _dot(h, wdkv_ref[...])
    cq = _rms(lat[:, 0:Q_LORA], qn_ref[...]).astype(BF16)
    ckv = _rms(lat[:, Q_LORA:Q_LORA + KV_LORA], kvn_ref[...]).astype(BF16)
    kr = lat[:, 384:512] * ck_ref[...] + lat[:, 512:640] * sk_ref[...]
    kin = jnp.concatenate([ckv, kr.astype(BF16)], axis=1)
    k_out[0] = _dot(kin, wk_ref[...]).astype(BF16)

    qT = _dot_nt(wqT_ref[...], cq) * (MLA_SCALE * LOG2E)
    cT = cT_ref[...]
    sT = sT_ref[...]
    zpad = jnp.zeros((HEAD_PAD - QK_NOPE - QK_ROPE, qT.shape[1]), BF16)
    half = QK_ROPE // 2
    for hd in range(MLA_HEADS):
        b0 = hd * HEAD_PAD
        x1 = qT[b0 + QK_NOPE:b0 + QK_NOPE + half]
        x2 = qT[b0 + QK_NOPE + half:b0 + QK_NOPE + QK_ROPE]
        qT_out[0, b0:b0 + QK_NOPE, :] = qT[b0:b0 + QK_NOPE].astype(BF16)
        qT_out[0, b0 + QK_NOPE:b0 + QK_NOPE + half, :] = (x1 * cT - x2 * sT).astype(BF16)
        qT_out[0, b0 + QK_NOPE + half:b0 + QK_NOPE + QK_ROPE, :] = (x2 * cT + x1 * sT).astype(BF16)
        qT_out[0, b0 + QK_NOPE + QK_ROPE:b0 + HEAD_PAD, :] = zpad

    vT = _dot_nt(wvT_ref[...], ckv) + ones_ref[...]
    for c in range(vT.shape[1] // MLA_TK):
        vT_out[0, c] = vT[:, c * MLA_TK:(c + 1) * MLA_TK].astype(BF16)


def _mla_pre(x, mod, gains, w, tables):
    B, S, _ = x.shape
    tm = TOKEN_TILE
    n_t = S // tm
    cpt = tm // MLA_TK
    ck, sk, cT, sT = tables
    hq = MLA_HEADS * HEAD_PAD
    hv = MLA_HEADS * V_ROWS
    return pl.pallas_call(
        _mla_pre_kernel,
        grid=(B, n_t),
        in_specs=[
            pl.BlockSpec((1, tm, D_MODEL), lambda b, i: (b, i, 0)),
            pl.BlockSpec((1, N_MOD, D_MODEL), lambda b, i: (b, 0, 0)),
            _const_spec((4, D_MODEL)),
            _const_spec((D_MODEL, LAT_COLS)),
            _const_spec((1, Q_LORA)),
            _const_spec((1, KV_LORA)),
            _const_spec((hq, Q_LORA)),
            _const_spec((2 * KV_LORA, hq)),
            _const_spec((hv, KV_LORA)),
            _const_spec((hv, 1)),
            pl.BlockSpec((tm, 128), lambda b, i: (i, 0)),
            pl.BlockSpec((tm, 128), lambda b, i: (i, 0)),
            pl.BlockSpec((QK_ROPE // 2, tm), lambda b, i: (0, i)),
            pl.BlockSpec((QK_ROPE // 2, tm), lambda b, i: (0, i)),
        ],
        out_specs=[
            pl.BlockSpec((1, hq, tm), lambda b, i: (b, 0, i)),
            pl.BlockSpec((1, tm, hq), lambda b, i: (b, i, 0)),
            pl.BlockSpec((1, cpt, hv, MLA_TK), lambda b, i: (b, i, 0, 0)),
        ],
        out_shape=[
            jax.ShapeDtypeStruct((B, hq, S), BF16),
            jax.ShapeDtypeStruct((B, S, hq), BF16),
            jax.ShapeDtypeStruct((B, S // MLA_TK, hv, MLA_TK), BF16),
        ],
        compiler_params=_params(2),
        name="mla_pre",
    )(x, mod, gains, w["wdkv"], w["qn"], w["kvn"], w["wqT"], w["wk"], w["wvT"], w["ones"],
      ck, sk, cT, sT)


def _mla_attn_kernel(qT_ref, k_ref, vT_ref, o_ref, *, n_chunks):
    tq = MLA_TQ
    n_qb = qT_ref.shape[2] // tq

    def scores(qb, j, hh):
        k = k_ref[0, j * MLA_TK:(j + 1) * MLA_TK, hh * HEAD_PAD:(hh + 1) * HEAD_PAD]
        return _dot(k, qT_ref[0, hh * HEAD_PAD:(hh + 1) * HEAD_PAD, qb * tq:(qb + 1) * tq])

    steps = [(qb, j) for qb in range(n_qb) for j in range(n_chunks)]
    pending = {}
    for qb, j in steps[:MLA_LOOKAHEAD]:
        for hh in range(2):
            pending[qb, j, hh] = scores(qb, j, hh)
    res = {}
    for i, (qb, j) in enumerate(steps):
        for hh in range(2):
            if i + MLA_LOOKAHEAD < len(steps):
                nqb, nj = steps[i + MLA_LOOKAHEAD]
                pending[nqb, nj, hh] = scores(nqb, nj, hh)
            s = pending.pop((qb, j, hh))
            if j == 0:
                m, acc = jnp.full((1, tq), NEG_INF, F32), jnp.zeros((V_ROWS, tq), F32)
            else:
                m, acc = res[hh]
            m_new = jnp.maximum(m, jnp.max(s, axis=0, keepdims=True))
            alpha = jnp.exp2(m - m_new)
            p = jnp.exp2(s - m_new).astype(BF16)
            vT = vT_ref[0, j, hh * V_ROWS:(hh + 1) * V_ROWS, :]
            res[hh] = (m_new, alpha * acc + _dot(vT, p))
        if j == n_chunks - 1:
            oT = jnp.concatenate([res[hh][1][0:V_DIM] / res[hh][1][V_DIM:V_DIM + 1] for hh in range(2)], axis=0)
            o_ref[0, qb * tq:(qb + 1) * tq, :] = oT.T.astype(BF16)


def _mla_attn(qT, k, vT):
    B, _, S = qT.shape
    n_chunks = S // MLA_TK
    hp = MLA_HEADS // 2
    n_qb = max(g for g in range(1, MLA_QBLOCKS + 1) if (S // MLA_TQ) % g == 0)
    tq_step = MLA_TQ * n_qb
    return pl.pallas_call(
        functools.partial(_mla_attn_kernel, n_chunks=n_chunks),
        grid=(B, hp, S // tq_step),
        in_specs=[
            pl.BlockSpec((1, 2 * HEAD_PAD, tq_step), lambda b, h, i: (b, h, i)),
            pl.BlockSpec((1, S, 2 * HEAD_PAD), lambda b, h, i: (b, 0, h)),
            pl.BlockSpec((1, n_chunks, 2 * V_ROWS, MLA_TK), lambda b, h, i: (b, 0, h, 0)),
        ],
        out_specs=pl.BlockSpec((1, tq_step, 2 * V_DIM), lambda b, h, i: (b, i, h)),
        out_shape=jax.ShapeDtypeStruct((B, S, MLA_HEADS * V_DIM), BF16),
        compiler_params=_params(3),
        name="mla_attn",
    )(qT, k, vT)


def _na_pre_kernel(x_ref, mod_ref, gains_ref, wqT_ref, wk_ref, wvT_ref, ones_ref,
                   qT_out, k_out, vT_out):
    x = x_ref[0]
    shift = mod_ref[0, 0:1, :]
    scale = mod_ref[0, 1:2, :]
    h = (_rms(x, gains_ref[0:1, :]) * (1.0 + scale) + shift).astype(BF16)
    k_out[0] = _dot(h, wk_ref[...]).astype(BF16)
    qT = _dot_nt(wqT_ref[...], h) * (NA_SCALE * LOG2E)
    zpad = jnp.zeros((NA_HEAD_DIM, qT.shape[1]), BF16)
    for hd in range(NA_HEADS):
        lo = hd * HEAD_PAD + (hd % 2) * NA_HEAD_DIM
        zo = hd * HEAD_PAD + (1 - hd % 2) * NA_HEAD_DIM
        qT_out[0, lo:lo + NA_HEAD_DIM, :] = qT[hd * NA_HEAD_DIM:(hd + 1) * NA_HEAD_DIM].astype(BF16)
        qT_out[0, zo:zo + NA_HEAD_DIM, :] = zpad
    vT = _dot_nt(wvT_ref[...], h) + ones_ref[...]
    for c in range(vT.shape[1] // NA_CHUNK):
        vT_out[0, c] = vT[:, c * NA_CHUNK:(c + 1) * NA_CHUNK].astype(BF16)


def _na_pre(x, mod, gains, w):
    B, S, _ = x.shape
    tm = TOKEN_TILE
    cpt = tm // NA_CHUNK
    hq = NA_HEADS * HEAD_PAD
    hk = NA_HEADS * NA_HEAD_DIM
    hv = NA_HEADS * V_ROWS
    return pl.pallas_call(
        _na_pre_kernel,
        grid=(B, S // tm),
        in_specs=[
            pl.BlockSpec((1, tm, D_MODEL), lambda b, i: (b, i, 0)),
            pl.BlockSpec((1, N_MOD, D_MODEL), lambda b, i: (b, 0, 0)),
            _const_spec((4, D_MODEL)),
            _const_spec((hk, D_MODEL)),
            _const_spec((D_MODEL, hk)),
            _const_spec((hv, D_MODEL)),
            _const_spec((hv, 1)),
        ],
        out_specs=[
            pl.BlockSpec((1, hq, tm), lambda b, i: (b, 0, i)),
            pl.BlockSpec((1, tm, hk), lambda b, i: (b, i, 0)),
            pl.BlockSpec((1, cpt, hv, NA_CHUNK), lambda b, i: (b, i, 0, 0)),
        ],
        out_shape=[
            jax.ShapeDtypeStruct((B, hq, S), BF16),
            jax.ShapeDtypeStruct((B, S, hk), BF16),
            jax.ShapeDtypeStruct((B, S // NA_CHUNK, hv, NA_CHUNK), BF16),
        ],
        compiler_params=_params(2),
        name="na_pre",
    )(x, mod, gains, w["wqT"], w["wk"], w["wvT"], w["ones"])


def _na_band_start(blk, n_blk):
    return jnp.clip(blk - 1, 0, n_blk - NA_BAND // NA_QROWS)


def _na_attn_kernel(qT_ref, k_ref, vT_ref, bias_ref, o_ref, *, n_blk, group):
    step = pl.program_id(2)
    tq = NA_QROWS * GRID_W
    n_keys = NA_BAND * GRID_W
    n_chunks = n_keys // NA_CHUNK
    band0 = [_na_band_start(step * group + g, n_blk) for g in range(group)]
    pattern = [jnp.where(step * group + g == 0, 0, jnp.where(step * group + g == n_blk - 1, 2, 1))
               for g in range(group)]

    def scores(g, c, hh):
        kc = k_ref[0, pl.ds(pl.multiple_of((band0[g] + c) * NA_CHUNK, NA_CHUNK), NA_CHUNK), :]
        qT = qT_ref[0, hh * HEAD_PAD:(hh + 1) * HEAD_PAD, g * tq:(g + 1) * tq]
        return _dot(kc, qT) + bias_ref[pattern[g], hh, c * NA_CHUNK:(c + 1) * NA_CHUNK, :]

    steps = [(g, c) for g in range(group) for c in range(n_chunks)]
    pending = {}
    for g, c in steps[:NA_LOOKAHEAD]:
        for hh in range(2):
            pending[g, c, hh] = scores(g, c, hh)
    res = {}
    for i, (g, c) in enumerate(steps):
        for hh in range(2):
            if i + NA_LOOKAHEAD < len(steps):
                ng, nc = steps[i + NA_LOOKAHEAD]
                pending[ng, nc, hh] = scores(ng, nc, hh)
            s = pending.pop((g, c, hh))
            if c == 0:
                m, acc = jnp.full((1, tq), NEG_INF, F32), jnp.zeros((V_ROWS, tq), F32)
            else:
                m, acc = res[hh]
            m_new = jnp.maximum(m, jnp.max(s, axis=0, keepdims=True))
            alpha = jnp.exp2(m - m_new)
            p = jnp.exp2((s - m_new).astype(BF16))
            vT = vT_ref[0, band0[g] + c, hh * V_ROWS:(hh + 1) * V_ROWS, :]
            res[hh] = (m_new, alpha * acc + _dot(vT, p))
        if c == n_chunks - 1:
            pair = jnp.concatenate([res[hh][1][0:NA_HEAD_DIM] / res[hh][1][NA_HEAD_DIM:NA_HEAD_DIM + 1]
                                    for hh in range(2)], axis=0)
            o_ref[0, g * tq:(g + 1) * tq, :] = pair.T.astype(BF16)


def _na_attn(qT, k, vT, bias):
    B, _, S = qT.shape
    tq = NA_QROWS * GRID_W
    n_blk = S // tq
    hp = NA_HEADS // 2
    group = max(g for g in range(1, NA_GROUP + 1) if n_blk % g == 0)
    return pl.pallas_call(
        functools.partial(_na_attn_kernel, n_blk=n_blk, group=group),
        grid=(B, hp, n_blk // group),
        in_specs=[
            pl.BlockSpec((1, 2 * HEAD_PAD, group * tq), lambda b, h, i: (b, h, i)),
            pl.BlockSpec((1, S, 2 * NA_HEAD_DIM), lambda b, h, i: (b, 0, h)),
            pl.BlockSpec((1, S // NA_CHUNK, 2 * V_ROWS, NA_CHUNK), lambda b, h, i: (b, 0, h, 0)),
            pl.BlockSpec((3, 2, NA_BAND * GRID_W, tq), lambda b, h, i: (0, h, 0, 0)),
        ],
        out_specs=pl.BlockSpec((1, group * tq, 2 * NA_HEAD_DIM), lambda b, h, i: (b, i, h)),
        out_shape=jax.ShapeDtypeStruct((B, S, NA_HEADS * NA_HEAD_DIM), BF16),
        compiler_params=_params(3),
        name="na_attn",
    )(qT, k, vT, bias)


def _na_bias_tables(rpb):
    p = np.arange(3)[:, None]
    qi = np.arange(NA_QROWS)[None, :]
    qr = NA_QROWS * p + qi
    r_start = np.clip(qr - WIN_R // 2, 0, NA_BAND - WIN_R)
    kr = np.arange(NA_BAND)[None, :, None]
    valid_r = (kr >= r_start[:, None, :]) & (kr < r_start[:, None, :] + WIN_R)
    dr = np.clip(kr - qr[:, None, :] + WIN_R - 1, 0, 2 * WIN_R - 2)
    c = np.arange(GRID_W)
    c_start = np.clip(c - WIN_C // 2, 0, GRID_W - WIN_C)
    kc = c[:, None]
    valid_c = (kc >= c_start[None, :]) & (kc < c_start[None, :] + WIN_C)
    dc = np.clip(kc - c[None, :] + WIN_C - 1, 0, 2 * WIN_C - 2)
    valid = valid_r[:, :, None, :, None] & valid_c[None, None, :, None, :]
    row_sel = (dr.reshape(-1)[:, None] == np.arange(2 * WIN_R - 1)[None, :]).astype(np.float32)
    col_sel = (dc[None] == np.arange(2 * WIN_C - 1)[:, None, None]).astype(np.float32)
    rows = jnp.sum(rpb.astype(F32)[:, None, :, :] * row_sel[None, :, :, None], axis=2)
    planes = jnp.einsum("hnd,dkq->hnkq", rows, col_sel, precision=lax.Precision.HIGHEST)
    planes = planes.reshape(NA_HEADS, 3, NA_BAND, NA_QROWS, GRID_W, GRID_W)
    bias = jnp.transpose(planes, (1, 0, 2, 4, 3, 5))
    bias = jnp.where(valid[:, None], bias * LOG2E, NEG_INF)
    return bias.reshape(3, NA_HEADS, NA_BAND * GRID_W, NA_QROWS * GRID_W)


def _post_ffn_kernel(x_ref, a_ref, mod_ref, gains_ref, wo_ref, wgu_ref, wd_ref, o_ref):
    gate_m = mod_ref[0, 2:3, :]
    shift_f = mod_ref[0, 3:4, :]
    scale_f = mod_ref[0, 4:5, :]
    gate_f = mod_ref[0, 5:6, :]
    n_sub = x_ref.shape[1] // FFN_SUBTILE
    rows = [slice(r * FFN_SUBTILE, (r + 1) * FFN_SUBTILE) for r in range(n_sub)]
    mix = [_dot(a_ref[0, r, :], wo_ref[...]) for r in rows]
    xs, gu, down = [], [], []
    for i in range(n_sub + 2):
        if i < n_sub:
            x = x_ref[0, rows[i], :] + gate_m * _rms(mix[i], gains_ref[1:2, :])
            h = (_rms(x, gains_ref[2:3, :]) * (1.0 + scale_f) + shift_f).astype(BF16)
            xs.append(x)
            gu.append((_dot(h, wgu_ref[:, 0:FFN_HIDDEN]), _dot(h, wgu_ref[:, FFN_HIDDEN:2 * FFN_HIDDEN])))
        if 1 <= i <= n_sub:
            g, u = gu[i - 1]
            act = ((g / (1.0 + jnp.exp(-g))) * u).astype(BF16)
            down.append(_dot(act, wd_ref[...]))
        if i >= 2:
            r = i - 2
            o_ref[0, rows[r], :] = xs[r] + gate_f * _rms(down[r], gains_ref[3:4, :])


def _post_ffn(x, attn, mod, gains, wo, wgu, wd):
    B, S, _ = x.shape
    tm = TOKEN_TILE
    return pl.pallas_call(
        _post_ffn_kernel,
        grid=(B, S // tm),
        in_specs=[
            pl.BlockSpec((1, tm, D_MODEL), lambda b, i: (b, i, 0)),
            pl.BlockSpec((1, tm, D_MODEL), lambda b, i: (b, i, 0)),
            pl.BlockSpec((1, N_MOD, D_MODEL), lambda b, i: (b, 0, 0)),
            _const_spec((4, D_MODEL)),
            _const_spec((D_MODEL, D_MODEL)),
            _const_spec((D_MODEL, 2 * FFN_HIDDEN)),
            _const_spec((FFN_HIDDEN, D_MODEL)),
        ],
        out_specs=pl.BlockSpec((1, tm, D_MODEL), lambda b, i: (b, i, 0)),
        out_shape=jax.ShapeDtypeStruct((B, S, D_MODEL), F32),
        compiler_params=_params(2),
        name="post_ffn",
    )(x, attn, mod, gains, wo, wgu, wd)


def _ones_column(n_heads):
    col = np.zeros((n_heads, V_ROWS, 1), np.float32)
    col[:, V_DIM, 0] = 1.0
    return jnp.asarray(col.reshape(n_heads * V_ROWS, 1))


def _prep_mla(w_dkv, q_norm, kv_norm, w_uq, w_ukv):
    half = QK_ROPE // 2
    r0 = Q_LORA + KV_LORA
    wdkv = jnp.zeros((D_MODEL, LAT_COLS), F32)
    wdkv = wdkv.at[:, :r0 + QK_ROPE].set(w_dkv)
    wdkv = wdkv.at[:, 512:512 + half].set(-w_dkv[:, r0 + half:r0 + QK_ROPE])
    wdkv = wdkv.at[:, 512 + half:512 + QK_ROPE].set(w_dkv[:, r0:r0 + half])
    wq = w_uq.reshape(Q_LORA, MLA_HEADS, QK_NOPE + QK_ROPE)
    wq = jnp.pad(wq, ((0, 0), (0, 0), (0, HEAD_PAD - QK_NOPE - QK_ROPE)))
    wqT = wq.reshape(Q_LORA, MLA_HEADS * HEAD_PAD).T
    wkv = w_ukv.reshape(KV_LORA, MLA_HEADS, QK_NOPE + V_DIM)
    wk_nope = jnp.pad(wkv[:, :, :QK_NOPE], ((0, 0), (0, 0), (0, HEAD_PAD - QK_NOPE)))
    eye = np.zeros((KV_LORA, MLA_HEADS, HEAD_PAD), np.float32)
    for r in range(QK_ROPE):
        eye[r, :, QK_NOPE + r] = 1.0
    wk = jnp.concatenate([wk_nope, jnp.asarray(eye)], axis=0).reshape(2 * KV_LORA, MLA_HEADS * HEAD_PAD)
    wv = jnp.pad(wkv[:, :, QK_NOPE:], ((0, 0), (0, 0), (0, V_ROWS - V_DIM)))
    wvT = wv.reshape(KV_LORA, MLA_HEADS * V_ROWS).T
    return dict(wdkv=wdkv.astype(BF16), qn=q_norm.reshape(1, Q_LORA), kvn=kv_norm.reshape(1, KV_LORA),
                wqT=wqT.astype(BF16), wk=wk.astype(BF16), wvT=wvT.astype(BF16), ones=_ones_column(MLA_HEADS))


def _prep_na(w_qkv):
    hk = NA_HEADS * NA_HEAD_DIM
    wv = w_qkv[:, 2 * hk:].reshape(D_MODEL, NA_HEADS, NA_HEAD_DIM)
    wv = jnp.pad(wv, ((0, 0), (0, 0), (0, V_ROWS - NA_HEAD_DIM)))
    return dict(wqT=w_qkv[:, :hk].T.astype(BF16), wk=w_qkv[:, hk:2 * hk].astype(BF16),
                wvT=wv.reshape(D_MODEL, NA_HEADS * V_ROWS).T.astype(BF16), ones=_ones_column(NA_HEADS))


def _rope_tables(seq_len):
    half = QK_ROPE // 2
    inv_freq = 1.0 / (ROPE_THETA ** (jnp.arange(0, QK_ROPE, 2, dtype=F32) / QK_ROPE))
    ang = jnp.arange(seq_len, dtype=F32)[:, None] * inv_freq[None, :]
    cos, sin = jnp.cos(ang), jnp.sin(ang)
    pad = jnp.zeros((seq_len, 128 - 2 * half), F32)
    ck = jnp.concatenate([cos, cos, pad], axis=1)
    sk = jnp.concatenate([sin, sin, pad], axis=1)
    return ck, sk, cos.T, sin.T


def _trunk(x, mods, gains, mla_w, na_w, na_bias, post_w):
    tables = _rope_tables(x.shape[1])
    for i in range(DEPTH):
        if i % 2 == 0:
            qT, k, vT = _mla_pre(x, mods[i], gains[i], mla_w[i // 2], tables)
            attn = _mla_attn(qT, k, vT)
        else:
            qT, k, vT = _na_pre(x, mods[i], gains[i], na_w[i // 2])
            attn = _na_attn(qT, k, vT, na_bias[i // 2])
        x = _post_ffn(x, attn, mods[i], gains[i], *post_w[i])
    return x


def kernel(x_prompt, x_sample, c_prompt, c_sample, ada_w, ada_b, norm_pre_mix, norm_post_mix, norm_pre_ffn, norm_post_ffn, mla_w_dkv, mla_q_norm, mla_kv_norm, mla_w_uq, mla_w_ukv, mla_w_o, na_w_qkv, na_rpb, na_w_o, ffn_w_gu, ffn_w_down):
    bp = x_prompt.shape[0]
    bs = x_sample.shape[0]
    mod = _ada_mod(jnp.concatenate([c_prompt, c_sample], axis=0), ada_w, ada_b)
    mod = mod.reshape(DEPTH, bp + bs, N_MOD, D_MODEL)
    gains = [jnp.stack([norm_pre_mix[i], norm_post_mix[i], norm_pre_ffn[i], norm_post_ffn[i]]) for i in range(DEPTH)]
    mla_w = [_prep_mla(mla_w_dkv[j], mla_q_norm[j], mla_kv_norm[j], mla_w_uq[j], mla_w_ukv[j])
             for j in range(mla_w_dkv.shape[0])]
    na_w = [_prep_na(na_w_qkv[j]) for j in range(na_w_qkv.shape[0])]
    na_bias = [_na_bias_tables(na_rpb[j]) for j in range(na_rpb.shape[0])]
    post_w = []
    for i in range(DEPTH):
        wo = mla_w_o[i // 2] if i % 2 == 0 else na_w_o[i // 2]
        post_w.append((wo.astype(BF16), ffn_w_gu[i].astype(BF16), ffn_w_down[i].astype(BF16)))
    y_prompt = _trunk(x_prompt, [mod[i, :bp] for i in range(DEPTH)], gains, mla_w, na_w, na_bias, post_w)
    y_sample = _trunk(x_sample, [mod[i, bp:] for i in range(DEPTH)], gains, mla_w, na_w, na_bias, post_w)
    return (y_prompt, y_sample)
```

```python
import functools

import jax
import jax.numpy as jnp
import numpy as np
from jax import lax
from jax.experimental import pallas as pl
from jax.experimental.pallas import tpu as pltpu

F32 = jnp.float32
BF16 = jnp.bfloat16

D_MODEL = 1024
DEPTH = 2
N_MOD = 6
RMS_EPS = 1e-6
NEG_INF = -1e30

MLA_HEADS = 16
Q_LORA = 256
KV_LORA = 128
QK_NOPE = 64
QK_ROPE = 32
V_DIM = 64
ROPE_THETA = 10000.0
MLA_SCALE = (QK_NOPE + QK_ROPE) ** -0.5
LOG2E = 1.4426950408889634
HEAD_PAD = 128
V_ROWS = 80
LAT_COLS = 640

NA_HEADS = 16
NA_HEAD_DIM = 64
GRID_W = 64
WIN_R = 8
WIN_C = 16
NA_QROWS = 4
NA_BAND = 12
NA_SCALE = NA_HEAD_DIM ** -0.5
NA_GROUP = 16
NA_LOOKAHEAD = 3

FFN_HIDDEN = 2816
FFN_SUBTILE = 256

TOKEN_TILE = 512
PRE_TILE = 1024
MLA_TQ = 256
MLA_QBLOCKS = 8
MLA_TK = 256
MLA_LOOKAHEAD = 3
NA_CHUNK = 256

VMEM_LIMIT = 56 * 1024 * 1024

_NT = (((1,), (1,)), ((), ()))


def _dot(a, b):
    return jnp.dot(a, b, preferred_element_type=F32)


def _dot_nt(a, b):
    return lax.dot_general(a, b, _NT, preferred_element_type=F32)


def _rms(x, g):
    ms = jnp.mean(x * x, axis=-1, keepdims=True)
    return x * lax.rsqrt(ms + RMS_EPS) * g


def _const_spec(shape):
    zeros = (0,) * len(shape)
    return pl.BlockSpec(shape, lambda *_: zeros, pipeline_mode=pl.Buffered(1))


def _params(n_axes):
    return pltpu.CompilerParams(
        dimension_semantics=("arbitrary",) * n_axes, vmem_limit_bytes=VMEM_LIMIT)


def _ada_kernel(c_ref, w_ref, b_ref, o_ref):
    c = c_ref[...]
    c_act = c / (1.0 + jnp.exp(-c))
    o_ref[0] = _dot(c_act.astype(BF16), w_ref[0]) + b_ref[0]


def _ada_mod(c_all, ada_w, ada_b):
    n_rows = c_all.shape[0]
    n_out = N_MOD * D_MODEL
    tn = 1536
    return pl.pallas_call(
        _ada_kernel,
        grid=(DEPTH, n_out // tn),
        in_specs=[
            pl.BlockSpec((n_rows, D_MODEL), lambda i, j: (0, 0)),
            pl.BlockSpec((1, D_MODEL, tn), lambda i, j: (i, 0, j)),
            pl.BlockSpec((1, 1, tn), lambda i, j: (i, 0, j)),
        ],
        out_specs=pl.BlockSpec((1, n_rows, tn), lambda i, j: (i, 0, j)),
        out_shape=jax.ShapeDtypeStruct((DEPTH, n_rows, n_out), F32),
        compiler_params=_params(2),
        name="ada_mod",
    )(c_all, ada_w.astype(BF16), ada_b.reshape(DEPTH, 1, n_out))


def _mla_pre_kernel(x_ref, mod_ref, gains_ref, wdkv_ref, qn_ref, kvn_ref, wqT_ref, wk_ref,
                    wvT_ref, ones_ref, ck_ref, sk_ref, cT_ref, sT_ref, qT_out, k_out, vT_out):
    shift = mod_ref[0, 0:1, :]
    scale = mod_ref[0, 1:2, :]
    half = QK_ROPE // 2
    subs = [slice(c * MLA_TK, (c + 1) * MLA_TK) for c in range(x_ref.shape[1] // MLA_TK)]
    lats = []
    for r in subs:
        h = (_rms(x_ref[0, r, :], gains_ref[0:1, :]) * (1.0 + scale) + shift).astype(BF16)
        lats.append(_dot(h, wdkv_ref[...]))
    for c, (r, lat) in enumerate(zip(subs, lats)):
        cq = _rms(lat[:, 0:Q_LORA], qn_ref[...]).astype(BF16)
        ckv = _rms(lat[:, Q_LORA:Q_LORA + KV_LORA], kvn_ref[...]).astype(BF16)
        kr = lat[:, 384:512] * ck_ref[r, :] + lat[:, 512:640] * sk_ref[r, :]
        kin = jnp.concatenate([ckv, kr.astype(BF16)], axis=1)
        k_out[0, r, :] = _dot(kin, wk_ref[...]).astype(BF16)

        qT = _dot_nt(wqT_ref[...], cq) * (MLA_SCALE * LOG2E)
        cT = cT_ref[:, r]
        sT = sT_ref[:, r]
        zpad = jnp.zeros((HEAD_PAD - QK_NOPE - QK_ROPE, qT.shape[1]), BF16)
        for hd in range(MLA_HEADS):
            b0 = hd * HEAD_PAD
            x1 = qT[b0 + QK_NOPE:b0 + QK_NOPE + half]
            x2 = qT[b0 + QK_NOPE + half:b0 + QK_NOPE + QK_ROPE]
            qT_out[0, b0:b0 + QK_NOPE, r] = qT[b0:b0 + QK_NOPE].astype(BF16)
            qT_out[0, b0 + QK_NOPE:b0 + QK_NOPE + half, r] = (x1 * cT - x2 * sT).astype(BF16)
            qT_out[0, b0 + QK_NOPE + half:b0 + QK_NOPE + QK_ROPE, r] = (x2 * cT + x1 * sT).astype(BF16)
            qT_out[0, b0 + QK_NOPE + QK_ROPE:b0 + HEAD_PAD, r] = zpad

        vT = _dot_nt(wvT_ref[...], ckv) + ones_ref[...]
        vT_out[0, c] = vT.astype(BF16)


def _mla_pre(x, mod, gains, w, tables):
    B, S, _ = x.shape
    tm = PRE_TILE if S % PRE_TILE == 0 else TOKEN_TILE
    n_t = S // tm
    cpt = tm // MLA_TK
    ck, sk, cT, sT = tables
    hq = MLA_HEADS * HEAD_PAD
    hv = MLA_HEADS * V_ROWS
    return pl.pallas_call(
        _mla_pre_kernel,
        grid=(B, n_t),
        in_specs=[
            pl.BlockSpec((1, tm, D_MODEL), lambda b, i: (b, i, 0)),
            pl.BlockSpec((1, N_MOD, D_MODEL), lambda b, i: (b, 0, 0)),
            _const_spec((4, D_MODEL)),
            _const_spec((D_MODEL, LAT_COLS)),
            _const_spec((1, Q_LORA)),
            _const_spec((1, KV_LORA)),
            _const_spec((hq, Q_LORA)),
            _const_spec((2 * KV_LORA, hq)),
            _const_spec((hv, KV_LORA)),
            _const_spec((hv, 1)),
            pl.BlockSpec((tm, 128), lambda b, i: (i, 0)),
            pl.BlockSpec((tm, 128), lambda b, i: (i, 0)),
            pl.BlockSpec((QK_ROPE // 2, tm), lambda b, i: (0, i)),
            pl.BlockSpec((QK_ROPE // 2, tm), lambda b, i: (0, i)),
        ],
        out_specs=[
            pl.BlockSpec((1, hq, tm), lambda b, i: (b, 0, i)),
            pl.BlockSpec((1, tm, hq), lambda b, i: (b, i, 0)),
            pl.BlockSpec((1, cpt, hv, MLA_TK), lambda b, i: (b, i, 0, 0)),
        ],
        out_shape=[
            jax.ShapeDtypeStruct((B, hq, S), BF16),
            jax.ShapeDtypeStruct((B, S, hq), BF16),
            jax.ShapeDtypeStruct((B, S // MLA_TK, hv, MLA_TK), BF16),
        ],
        compiler_params=_params(2),
        name="mla_pre",
    )(x, mod, gains, w["wdkv"], w["qn"], w["kvn"], w["wqT"], w["wk"], w["wvT"], w["ones"],
      ck, sk, cT, sT)


def _mla_attn_kernel(qT_ref, k_ref, vT_ref, o_ref, *, n_chunks):
    tq = MLA_TQ
    n_qb = qT_ref.shape[2] // tq

    def scores(qb, j, hh):
        k = k_ref[0, j * MLA_TK:(j + 1) * MLA_TK, hh * HEAD_PAD:(hh + 1) * HEAD_PAD]
        return _dot(k, qT_ref[0, hh * HEAD_PAD:(hh + 1) * HEAD_PAD, qb * tq:(qb + 1) * tq])

    steps = [(qb, j) for qb in range(n_qb) for j in range(n_chunks)]
    pending = {}
    for qb, j in steps[:MLA_LOOKAHEAD]:
        for hh in range(2):
            pending[qb, j, hh] = scores(qb, j, hh)
    res = {}
    for i, (qb, j) in enumerate(steps):
        for hh in range(2):
            if i + MLA_LOOKAHEAD < len(steps):
                nqb, nj = steps[i + MLA_LOOKAHEAD]
                pending[nqb, nj, hh] = scores(nqb, nj, hh)
            s = pending.pop((qb, j, hh))
            if j == 0:
                m, acc = jnp.full((1, tq), NEG_INF, F32), jnp.zeros((V_ROWS, tq), F32)
            else:
                m, acc = res[hh]
            m_new = jnp.maximum(m, jnp.max(s, axis=0, keepdims=True))
            alpha = jnp.exp2(m - m_new)
            p = jnp.exp2(s - m_new).astype(BF16)
            vT = vT_ref[0, j, hh * V_ROWS:(hh + 1) * V_ROWS, :]
            res[hh] = (m_new, alpha * acc + _dot(vT, p))
        if j == n_chunks - 1:
            oT = jnp.concatenate([res[hh][1][0:V_DIM] / res[hh][1][V_DIM:V_DIM + 1] for hh in range(2)], axis=0)
            o_ref[0, qb * tq:(qb + 1) * tq, :] = oT.T.astype(BF16)


def _mla_attn(qT, k, vT):
    B, _, S = qT.shape
    n_chunks = S // MLA_TK
    hp = MLA_HEADS // 2
    n_qb = max(g for g in range(1, MLA_QBLOCKS + 1) if (S // MLA_TQ) % g == 0)
    tq_step = MLA_TQ * n_qb
    return pl.pallas_call(
        functools.partial(_mla_attn_kernel, n_chunks=n_chunks),
        grid=(B, hp, S // tq_step),
        in_specs=[
            pl.BlockSpec((1, 2 * HEAD_PAD, tq_step), lambda b, h, i: (b, h, i)),
            pl.BlockSpec((1, S, 2 * HEAD_PAD), lambda b, h, i: (b, 0, h)),
            pl.BlockSpec((1, n_chunks, 2 * V_ROWS, MLA_TK), lambda b, h, i: (b, 0, h, 0)),
        ],
        out_specs=pl.BlockSpec((1, tq_step, 2 * V_DIM), lambda b, h, i: (b, i, h)),
        out_shape=jax.ShapeDtypeStruct((B, S, MLA_HEADS * V_DIM), BF16),
        compiler_params=_params(3),
        name="mla_attn",
    )(qT, k, vT)


def _na_pre_kernel(x_ref, mod_ref, gains_ref, wqT_ref, wk_ref, wvT_ref, ones_ref,
                   qT_out, k_out, vT_out):
    shift = mod_ref[0, 0:1, :]
    scale = mod_ref[0, 1:2, :]
    for c in range(x_ref.shape[1] // NA_CHUNK):
        r = slice(c * NA_CHUNK, (c + 1) * NA_CHUNK)
        h = (_rms(x_ref[0, r, :], gains_ref[0:1, :]) * (1.0 + scale) + shift).astype(BF16)
        k_out[0, r, :] = _dot(h, wk_ref[...]).astype(BF16)
        qT = _dot_nt(wqT_ref[...], h) * (NA_SCALE * LOG2E)
        zpad = jnp.zeros((NA_HEAD_DIM, qT.shape[1]), BF16)
        for hd in range(NA_HEADS):
            lo = hd * HEAD_PAD + (hd % 2) * NA_HEAD_DIM
            zo = hd * HEAD_PAD + (1 - hd % 2) * NA_HEAD_DIM
            qT_out[0, lo:lo + NA_HEAD_DIM, r] = qT[hd * NA_HEAD_DIM:(hd + 1) * NA_HEAD_DIM].astype(BF16)
            qT_out[0, zo:zo + NA_HEAD_DIM, r] = zpad
        vT = _dot_nt(wvT_ref[...], h) + ones_ref[...]
        vT_out[0, c] = vT.astype(BF16)


def _na_pre(x, mod, gains, w):
    B, S, _ = x.shape
    tm = PRE_TILE if S % PRE_TILE == 0 else TOKEN_TILE
    cpt = tm // NA_CHUNK
    hq = NA_HEADS * HEAD_PAD
    hk = NA_HEADS * NA_HEAD_DIM
    hv = NA_HEADS * V_ROWS
    return pl.pallas_call(
        _na_pre_kernel,
        grid=(B, S // tm),
        in_specs=[
            pl.BlockSpec((1, tm, D_MODEL), lambda b, i: (b, i, 0)),
            pl.BlockSpec((1, N_MOD, D_MODEL), lambda b, i: (b, 0, 0)),
            _const_spec((4, D_MODEL)),
            _const_spec((hk, D_MODEL)),
            _const_spec((D_MODEL, hk)),
            _const_spec((hv, D_MODEL)),
            _const_spec((hv, 1)),
        ],
        out_specs=[
            pl.BlockSpec((1, hq, tm), lambda b, i: (b, 0, i)),
            pl.BlockSpec((1, tm, hk), lambda b, i: (b, i, 0)),
            pl.BlockSpec((1, cpt, hv, NA_CHUNK), lambda b, i: (b, i, 0, 0)),
        ],
        out_shape=[
            jax.ShapeDtypeStruct((B, hq, S), BF16),
            jax.ShapeDtypeStruct((B, S, hk), BF16),
            jax.ShapeDtypeStruct((B, S // NA_CHUNK, hv, NA_CHUNK), BF16),
        ],
        compiler_params=_params(2),
        name="na_pre",
    )(x, mod, gains, w["wqT"], w["wk"], w["wvT"], w["ones"])


def _na_band_start(blk, n_blk):
    return jnp.clip(blk - 1, 0, n_blk - NA_BAND // NA_QROWS)


def _na_attn_kernel(qT_ref, k_ref, vT_ref, bias_ref, o_ref, *, n_blk, group):
    step = pl.program_id(2)
    tq = NA_QROWS * GRID_W
    n_keys = NA_BAND * GRID_W
    n_chunks = n_keys // NA_CHUNK
    band0 = [_na_band_start(step * group + g, n_blk) for g in range(group)]
    pattern = [jnp.where(step * group + g == 0, 0, jnp.where(step * group + g == n_blk - 1, 2, 1))
               for g in range(group)]

    def scores(g, c, hh):
        kc = k_ref[0, pl.ds(pl.multiple_of((band0[g] + c) * NA_CHUNK, NA_CHUNK), NA_CHUNK), :]
        qT = qT_ref[0, hh * HEAD_PAD:(hh + 1) * HEAD_PAD, g * tq:(g + 1) * tq]
        return _dot(kc, qT) + bias_ref[pattern[g], hh, c * NA_CHUNK:(c + 1) * NA_CHUNK, :]

    steps = [(g, c) for g in range(group) for c in range(n_chunks)]
    pending = {}
    for g, c in steps[:NA_LOOKAHEAD]:
        for hh in range(2):
            pending[g, c, hh] = scores(g, c, hh)
    res = {}
    for i, (g, c) in enumerate(steps):
        for hh in range(2):
            if i + NA_LOOKAHEAD < len(steps):
                ng, nc = steps[i + NA_LOOKAHEAD]
                pending[ng, nc, hh] = scores(ng, nc, hh)
            s = pending.pop((g, c, hh))
            if c == 0:
                m, acc = jnp.full((1, tq), NEG_INF, F32), jnp.zeros((V_ROWS, tq), F32)
            else:
                m, acc = res[hh]
            m_new = jnp.maximum(m, jnp.max(s, axis=0, keepdims=True))
            alpha = jnp.exp2(m - m_new)
            p = jnp.exp2((s - m_new).astype(BF16))
            vT = vT_ref[0, band0[g] + c, hh * V_ROWS:(hh + 1) * V_ROWS, :]
            res[hh] = (m_new, alpha * acc + _dot(vT, p))
        if c == n_chunks - 1:
            pair = jnp.concatenate([res[hh][1][0:NA_HEAD_DIM] / res[hh][1][NA_HEAD_DIM:NA_HEAD_DIM + 1]
                                    for hh in range(2)], axis=0)
            o_ref[0, g * tq:(g + 1) * tq, :] = pair.T.astype(BF16)


def _na_attn(qT, k, vT, bias):
    B, _, S = qT.shape
    tq = NA_QROWS * GRID_W
    n_blk = S // tq
    hp = NA_HEADS // 2
    group = max(g for g in range(1, NA_GROUP + 1) if n_blk % g == 0)
    return pl.pallas_call(
        functools.partial(_na_attn_kernel, n_blk=n_blk, group=group),
        grid=(B, hp, n_blk // group),
        in_specs=[
            pl.BlockSpec((1, 2 * HEAD_PAD, group * tq), lambda b, h, i: (b, h, i)),
            pl.BlockSpec((1, S, 2 * NA_HEAD_DIM), lambda b, h, i: (b, 0, h)),
            pl.BlockSpec((1, S // NA_CHUNK, 2 * V_ROWS, NA_CHUNK), lambda b, h, i: (b, 0, h, 0)),
            pl.BlockSpec((3, 2, NA_BAND * GRID_W, tq), lambda b, h, i: (0, h, 0, 0)),
        ],
        out_specs=pl.BlockSpec((1, group * tq, 2 * NA_HEAD_DIM), lambda b, h, i: (b, i, h)),
        out_shape=jax.ShapeDtypeStruct((B, S, NA_HEADS * NA_HEAD_DIM), BF16),
        compiler_params=_params(3),
        name="na_attn",
    )(qT, k, vT, bias)


def _na_bias_tables(rpb):
    p = np.arange(3)[:, None]
    qi = np.arange(NA_QROWS)[None, :]
    qr = NA_QROWS * p + qi
    r_start = np.clip(qr - WIN_R // 2, 0, NA_BAND - WIN_R)
    kr = np.arange(NA_BAND)[None, :, None]
    valid_r = (kr >= r_start[:, None, :]) & (kr < r_start[:, None, :] + WIN_R)
    dr = np.clip(kr - qr[:, None, :] + WIN_R - 1, 0, 2 * WIN_R - 2)
    c = np.arange(GRID_W)
    c_start = np.clip(c - WIN_C // 2, 0, GRID_W - WIN_C)
    kc = c[:, None]
    valid_c = (kc >= c_start[None, :]) & (kc < c_start[None, :] + WIN_C)
    dc = np.clip(kc - c[None, :] + WIN_C - 1, 0, 2 * WIN_C - 2)
    valid = valid_r[:, :, None, :, None] & valid_c[None, None, :, None, :]
    row_sel = (dr.reshape(-1)[:, None] == np.arange(2 * WIN_R - 1)[None, :]).astype(np.float32)
    col_sel = (dc[None] == np.arange(2 * WIN_C - 1)[:, None, None]).astype(np.float32)
    rows = jnp.sum(rpb.astype(F32)[:, None, :, :] * row_sel[None, :, :, None], axis=2)
    planes = jnp.einsum("hnd,dkq->hnkq", rows, col_sel, precision=lax.Precision.HIGHEST)
    planes = planes.reshape(NA_HEADS, 3, NA_BAND, NA_QROWS, GRID_W, GRID_W)
    bias = jnp.transpose(planes, (1, 0, 2, 4, 3, 5))
    bias = jnp.where(valid[:, None], bias * LOG2E, NEG_INF)
    return bias.reshape(3, NA_HEADS, NA_BAND * GRID_W, NA_QROWS * GRID_W)


def _post_ffn_kernel(x_ref, a_ref, mod_ref, gains_ref, wo_ref, wgu_ref, wd_ref, o_ref):
    gate_m = mod_ref[0, 2:3, :]
    shift_f = mod_ref[0, 3:4, :]
    scale_f = mod_ref[0, 4:5, :]
    gate_f = mod_ref[0, 5:6, :]
    n_sub = x_ref.shape[1] // FFN_SUBTILE
    rows = [slice(r * FFN_SUBTILE, (r + 1) * FFN_SUBTILE) for r in range(n_sub)]
    mix = [_dot(a_ref[0, r, :], wo_ref[...]) for r in rows]
    xs, gu, down = [], [], []
    for i in range(n_sub + 2):
        if i < n_sub:
            x = x_ref[0, rows[i], :] + gate_m * _rms(mix[i], gains_ref[1:2, :])
            h = (_rms(x, gains_ref[2:3, :]) * (1.0 + scale_f) + shift_f).astype(BF16)
            xs.append(x)
            gu.append((_dot(h, wgu_ref[:, 0:FFN_HIDDEN]), _dot(h, wgu_ref[:, FFN_HIDDEN:2 * FFN_HIDDEN])))
        if 1 <= i <= n_sub:
            g, u = gu[i - 1]
            act = ((g / (1.0 + jnp.exp(-g))) * u).astype(BF16)
            down.append(_dot(act, wd_ref[...]))
        if i >= 2:
            r = i - 2
            o_ref[0, rows[r], :] = xs[r] + gate_f * _rms(down[r], gains_ref[3:4, :])


def _post_ffn(x, attn, mod, gains, wo, wgu, wd):
    B, S, _ = x.shape
    tm = TOKEN_TILE
    return pl.pallas_call(
        _post_ffn_kernel,
        grid=(B, S // tm),
        in_specs=[
            pl.BlockSpec((1, tm, D_MODEL), lambda b, i: (b, i, 0)),
            pl.BlockSpec((1, tm, D_MODEL), lambda b, i: (b, i, 0)),
            pl.BlockSpec((1, N_MOD, D_MODEL), lambda b, i: (b, 0, 0)),
            _const_spec((4, D_MODEL)),
            _const_spec((D_MODEL, D_MODEL)),
            _const_spec((D_MODEL, 2 * FFN_HIDDEN)),
            _const_spec((FFN_HIDDEN, D_MODEL)),
        ],
        out_specs=pl.BlockSpec((1, tm, D_MODEL), lambda b, i: (b, i, 0)),
        out_shape=jax.ShapeDtypeStruct((B, S, D_MODEL), F32),
        compiler_params=_params(2),
        name="post_ffn",
    )(x, attn, mod, gains, wo, wgu, wd)


def _ones_column(n_heads):
    col = np.zeros((n_heads, V_ROWS, 1), np.float32)
    col[:, V_DIM, 0] = 1.0
    return jnp.asarray(col.reshape(n_heads * V_ROWS, 1))


def _prep_mla(w_dkv, q_norm, kv_norm, w_uq, w_ukv):
    half = QK_ROPE // 2
    r0 = Q_LORA + KV_LORA
    wdkv = jnp.zeros((D_MODEL, LAT_COLS), F32)
    wdkv = wdkv.at[:, :r0 + QK_ROPE].set(w_dkv)
    wdkv = wdkv.at[:, 512:512 + half].set(-w_dkv[:, r0 + half:r0 + QK_ROPE])
    wdkv = wdkv.at[:, 512 + half:512 + QK_ROPE].set(w_dkv[:, r0:r0 + half])
    wq = w_uq.reshape(Q_LORA, MLA_HEADS, QK_NOPE + QK_ROPE)
    wq = jnp.pad(wq, ((0, 0), (0, 0), (0, HEAD_PAD - QK_NOPE - QK_ROPE)))
    wqT = wq.reshape(Q_LORA, MLA_HEADS * HEAD_PAD).T
    wkv = w_ukv.reshape(KV_LORA, MLA_HEADS, QK_NOPE + V_DIM)
    wk_nope = jnp.pad(wkv[:, :, :QK_NOPE], ((0, 0), (0, 0), (0, HEAD_PAD - QK_NOPE)))
    eye = np.zeros((KV_LORA, MLA_HEADS, HEAD_PAD), np.float32)
    for r in range(QK_ROPE):
        eye[r, :, QK_NOPE + r] = 1.0
    wk = jnp.concatenate([wk_nope, jnp.asarray(eye)], axis=0).reshape(2 * KV_LORA, MLA_HEADS * HEAD_PAD)
    wv = jnp.pad(wkv[:, :, QK_NOPE:], ((0, 0), (0, 0), (0, V_ROWS - V_DIM)))
    wvT = wv.reshape(KV_LORA, MLA_HEADS * V_ROWS).T
    return dict(wdkv=wdkv.astype(BF16), qn=q_norm.reshape(1, Q_LORA), kvn=kv_norm.reshape(1, KV_LORA),
                wqT=wqT.astype(BF16), wk=wk.astype(BF16), wvT=wvT.astype(BF16), ones=_ones_column(MLA_HEADS))


def _prep_na(w_qkv):
    hk = NA_HEADS * NA_HEAD_DIM
    wv = w_qkv[:, 2 * hk:].reshape(D_MODEL, NA_HEADS, NA_HEAD_DIM)
    wv = jnp.pad(wv, ((0, 0), (0, 0), (0, V_ROWS - NA_HEAD_DIM)))
    return dict(wqT=w_qkv[:, :hk].T.astype(BF16), wk=w_qkv[:, hk:2 * hk].astype(BF16),
                wvT=wv.reshape(D_MODEL, NA_HEADS * V_ROWS).T.astype(BF16), ones=_ones_column(NA_HEADS))


def _rope_tables(seq_len):
    half = QK_ROPE // 2
    inv_freq = 1.0 / (ROPE_THETA ** (jnp.arange(0, QK_ROPE, 2, dtype=F32) / QK_ROPE))
    ang = jnp.arange(seq_len, dtype=F32)[:, None] * inv_freq[None, :]
    cos, sin = jnp.cos(ang), jnp.sin(ang)
    pad = jnp.zeros((seq_len, 128 - 2 * half), F32)
    ck = jnp.concatenate([cos, cos, pad], axis=1)
    sk = jnp.concatenate([sin, sin, pad], axis=1)
    return ck, sk, cos.T, sin.T


def _trunk(x, mods, gains, mla_w, na_w, na_bias, post_w):
    tables = _rope_tables(x.shape[1])
    for i in range(DEPTH):
        if i % 2 == 0:
            qT, k, vT = _mla_pre(x, mods[i], gains[i], mla_w[i // 2], tables)
            attn = _mla_attn(qT, k, vT)
        else:
            qT, k, vT = _na_pre(x, mods[i], gains[i], na_w[i // 2])
            attn = _na_attn(qT, k, vT, na_bias[i // 2])
        x = _post_ffn(x, attn, mods[i], gains[i], *post_w[i])
    return x


def kernel(x_prompt, x_sample, c_prompt, c_sample, ada_w, ada_b, norm_pre_mix, norm_post_mix, norm_pre_ffn, norm_post_ffn, mla_w_dkv, mla_q_norm, mla_kv_norm, mla_w_uq, mla_w_ukv, mla_w_o, na_w_qkv, na_rpb, na_w_o, ffn_w_gu, ffn_w_down):
    bp = x_prompt.shape[0]
    bs = x_sample.shape[0]
    mod = _ada_mod(jnp.concatenate([c_prompt, c_sample], axis=0), ada_w, ada_b)
    mod = mod.reshape(DEPTH, bp + bs, N_MOD, D_MODEL)
    gains = [jnp.stack([norm_pre_mix[i], norm_post_mix[i], norm_pre_ffn[i], norm_post_ffn[i]]) for i in range(DEPTH)]
    mla_w = [_prep_mla(mla_w_dkv[j], mla_q_norm[j], mla_kv_norm[j], mla_w_uq[j], mla_w_ukv[j])
             for j in range(mla_w_dkv.shape[0])]
    na_w = [_prep_na(na_w_qkv[j]) for j in range(na_w_qkv.shape[0])]
    na_bias = [_na_bias_tables(na_rpb[j]) for j in range(na_rpb.shape[0])]
    post_w = []
    for i in range(DEPTH):
        wo = mla_w_o[i // 2] if i % 2 == 0 else na_w_o[i // 2]
        post_w.append((wo.astype(BF16), ffn_w_gu[i].astype(BF16), ffn_w_down[i].astype(BF16)))
    y_prompt = _trunk(x_prompt, [mod[i, :bp] for i in range(DEPTH)], gains, mla_w, na_w, na_bias, post_w)
    y_sample = _trunk(x_sample, [mod[i, bp:] for i in range(DEPTH)], gains, mla_w, na_w, na_bias, post_w)
    return (y_prompt, y_sample)
```

```python
import functools

import jax
import jax.numpy as jnp
import numpy as np
from jax import lax
from jax.experimental import pallas as pl
from jax.experimental.pallas import tpu as pltpu

F32 = jnp.float32
BF16 = jnp.bfloat16

D_MODEL = 1024
DEPTH = 2
N_MOD = 6
RMS_EPS = 1e-6
NEG_INF = -1e30

MLA_HEADS = 16
Q_LORA = 256
KV_LORA = 128
QK_NOPE = 64
QK_ROPE = 32
V_DIM = 64
ROPE_THETA = 10000.0
MLA_SCALE = (QK_NOPE + QK_ROPE) ** -0.5
LOG2E = 1.4426950408889634
HEAD_PAD = 128
V_ROWS = 80
LAT_COLS = 640

NA_HEADS = 16
NA_HEAD_DIM = 64
GRID_W = 64
WIN_R = 8
WIN_C = 16
NA_QROWS = 4
NA_BAND = 12
NA_SCALE = NA_HEAD_DIM ** -0.5
NA_GROUP = 16
NA_LOOKAHEAD = 3

FFN_HIDDEN = 2816
FFN_SUBTILE = 256

TOKEN_TILE = 512
PRE_TILE = 1024
MLA_TQ = 256
MLA_QBLOCKS = 8
MLA_TK = 256
MLA_LOOKAHEAD = 3
NA_CHUNK = 256

VMEM_LIMIT = 56 * 1024 * 1024

_NT = (((1,), (1,)), ((), ()))


def _dot(a, b):
    return jnp.dot(a, b, preferred_element_type=F32)


def _dot_nt(a, b):
    return lax.dot_general(a, b, _NT, preferred_element_type=F32)


def _rms(x, g):
    ms = jnp.mean(x * x, axis=-1, keepdims=True)
    return x * lax.rsqrt(ms + RMS_EPS) * g


def _const_spec(shape):
    zeros = (0,) * len(shape)
    return pl.BlockSpec(shape, lambda *_: zeros, pipeline_mode=pl.Buffered(1))


def _params(n_axes):
    return pltpu.CompilerParams(
        dimension_semantics=("arbitrary",) * n_axes, vmem_limit_bytes=VMEM_LIMIT)


def _ada_kernel(c_ref, w_ref, b_ref, o_ref):
    c = c_ref[...]
    c_act = c / (1.0 + jnp.exp(-c))
    o_ref[0] = _dot(c_act.astype(BF16), w_ref[0]) + b_ref[0]


def _ada_mod(c_all, ada_w, ada_b):
    n_rows = c_all.shape[0]
    n_out = N_MOD * D_MODEL
    tn = 1536
    return pl.pallas_call(
        _ada_kernel,
        grid=(DEPTH, n_out // tn),
        in_specs=[
            pl.BlockSpec((n_rows, D_MODEL), lambda i, j: (0, 0)),
            pl.BlockSpec((1, D_MODEL, tn), lambda i, j: (i, 0, j)),
            pl.BlockSpec((1, 1, tn), lambda i, j: (i, 0, j)),
        ],
        out_specs=pl.BlockSpec((1, n_rows, tn), lambda i, j: (i, 0, j)),
        out_shape=jax.ShapeDtypeStruct((DEPTH, n_rows, n_out), F32),
        compiler_params=_params(2),
        name="ada_mod",
    )(c_all, ada_w.astype(BF16), ada_b.reshape(DEPTH, 1, n_out))


def _mla_pre_kernel(x_ref, mod_ref, gains_ref, wdkv_ref, qn_ref, kvn_ref, wqT_ref, wk_ref,
                    wvT_ref, ones_ref, ck_ref, sk_ref, cT_ref, sT_ref, qT_out, k_out, vT_out):
    shift = mod_ref[0, 0:1, :]
    scale = mod_ref[0, 1:2, :]
    half = QK_ROPE // 2
    subs = [slice(c * MLA_TK, (c + 1) * MLA_TK) for c in range(x_ref.shape[1] // MLA_TK)]
    lats = []
    for r in subs:
        h = (_rms(x_ref[0, r, :], gains_ref[0:1, :]) * (1.0 + scale) + shift).astype(BF16)
        lats.append(_dot(h, wdkv_ref[...]))
    for c, (r, lat) in enumerate(zip(subs, lats)):
        cq = _rms(lat[:, 0:Q_LORA], qn_ref[...]).astype(BF16)
        ckv = _rms(lat[:, Q_LORA:Q_LORA + KV_LORA], kvn_ref[...]).astype(BF16)
        kr = lat[:, 384:512] * ck_ref[r, :] + lat[:, 512:640] * sk_ref[r, :]
        kin = jnp.concatenate([ckv, kr.astype(BF16)], axis=1)
        k_out[0, r, :] = _dot(kin, wk_ref[...]).astype(BF16)

        qT = _dot_nt(wqT_ref[...], cq) * (MLA_SCALE * LOG2E)
        cT = cT_ref[:, r]
        sT = sT_ref[:, r]
        zpad = jnp.zeros((HEAD_PAD - QK_NOPE - QK_ROPE, qT.shape[1]), BF16)
        for hd in range(MLA_HEADS):
            b0 = hd * HEAD_PAD
            x1 = qT[b0 + QK_NOPE:b0 + QK_NOPE + half]
            x2 = qT[b0 + QK_NOPE + half:b0 + QK_NOPE + QK_ROPE]
            qT_out[0, b0:b0 + QK_NOPE, r] = qT[b0:b0 + QK_NOPE].astype(BF16)
            qT_out[0, b0 + QK_NOPE:b0 + QK_NOPE + half, r] = (x1 * cT - x2 * sT).astype(BF16)
            qT_out[0, b0 + QK_NOPE + half:b0 + QK_NOPE + QK_ROPE, r] = (x2 * cT + x1 * sT).astype(BF16)
            qT_out[0, b0 + QK_NOPE + QK_ROPE:b0 + HEAD_PAD, r] = zpad

        vT = _dot_nt(wvT_ref[...], ckv) + ones_ref[...]
        vT_out[0, c] = vT.astype(BF16)


def _mla_pre(x, mod, gains, w, tables):
    B, S, _ = x.shape
    tm = PRE_TILE if S % PRE_TILE == 0 else TOKEN_TILE
    n_t = S // tm
    cpt = tm // MLA_TK
    ck, sk, cT, sT = tables
    hq = MLA_HEADS * HEAD_PAD
    hv = MLA_HEADS * V_ROWS
    return pl.pallas_call(
        _mla_pre_kernel,
        grid=(B, n_t),
        in_specs=[
            pl.BlockSpec((1, tm, D_MODEL), lambda b, i: (b, i, 0)),
            pl.BlockSpec((1, N_MOD, D_MODEL), lambda b, i: (b, 0, 0)),
            _const_spec((4, D_MODEL)),
            _const_spec((D_MODEL, LAT_COLS)),
            _const_spec((1, Q_LORA)),
            _const_spec((1, KV_LORA)),
            _const_spec((hq, Q_LORA)),
            _const_spec((2 * KV_LORA, hq)),
            _const_spec((hv, KV_LORA)),
            _const_spec((hv, 1)),
            pl.BlockSpec((tm, 128), lambda b, i: (i, 0)),
            pl.BlockSpec((tm, 128), lambda b, i: (i, 0)),
            pl.BlockSpec((QK_ROPE // 2, tm), lambda b, i: (0, i)),
            pl.BlockSpec((QK_ROPE // 2, tm), lambda b, i: (0, i)),
        ],
        out_specs=[
            pl.BlockSpec((1, hq, tm), lambda b, i: (b, 0, i)),
            pl.BlockSpec((1, tm, hq), lambda b, i: (b, i, 0)),
            pl.BlockSpec((1, cpt, hv, MLA_TK), lambda b, i: (b, i, 0, 0)),
        ],
        out_shape=[
            jax.ShapeDtypeStruct((B, hq, S), BF16),
            jax.ShapeDtypeStruct((B, S, hq), BF16),
            jax.ShapeDtypeStruct((B, S // MLA_TK, hv, MLA_TK), BF16),
        ],
        compiler_params=_params(2),
        name="mla_pre",
    )(x, mod, gains, w["wdkv"], w["qn"], w["kvn"], w["wqT"], w["wk"], w["wvT"], w["ones"],
      ck, sk, cT, sT)


def _mla_attn_kernel(qT_ref, k_ref, vT_ref, o_ref, *, n_chunks):
    tq = MLA_TQ
    n_qb = qT_ref.shape[2] // tq

    def scores(qb, j, hh):
        k = k_ref[0, j * MLA_TK:(j + 1) * MLA_TK, hh * HEAD_PAD:(hh + 1) * HEAD_PAD]
        return _dot(k, qT_ref[0, hh * HEAD_PAD:(hh + 1) * HEAD_PAD, qb * tq:(qb + 1) * tq])

    steps = [(qb, j) for qb in range(n_qb) for j in range(n_chunks)]
    pending = {}
    for qb, j in steps[:MLA_LOOKAHEAD]:
        for hh in range(2):
            pending[qb, j, hh] = scores(qb, j, hh)
    res = {}
    for i, (qb, j) in enumerate(steps):
        for hh in range(2):
            if i + MLA_LOOKAHEAD < len(steps):
                nqb, nj = steps[i + MLA_LOOKAHEAD]
                pending[nqb, nj, hh] = scores(nqb, nj, hh)
            s = pending.pop((qb, j, hh))
            if j == 0:
                m, acc = jnp.full((1, tq), NEG_INF, F32), jnp.zeros((V_ROWS, tq), F32)
            else:
                m, acc = res[hh]
            m_new = jnp.maximum(m, jnp.max(s, axis=0, keepdims=True))
            alpha = jnp.exp2(m - m_new)
            p = jnp.exp2(s - m_new).astype(BF16)
            vT = vT_ref[0, j, hh * V_ROWS:(hh + 1) * V_ROWS, :]
            res[hh] = (m_new, alpha * acc + _dot(vT, p))
        if j == n_chunks - 1:
            oT = jnp.concatenate([res[hh][1][0:V_DIM] / res[hh][1][V_DIM:V_DIM + 1] for hh in range(2)], axis=0)
            o_ref[0, qb * tq:(qb + 1) * tq, :] = oT.T.astype(BF16)


def _mla_attn(qT, k, vT):
    B, _, S = qT.shape
    n_chunks = S // MLA_TK
    hp = MLA_HEADS // 2
    n_qb = max(g for g in range(1, MLA_QBLOCKS + 1) if (S // MLA_TQ) % g == 0)
    tq_step = MLA_TQ * n_qb
    return pl.pallas_call(
        functools.partial(_mla_attn_kernel, n_chunks=n_chunks),
        grid=(B, hp, S // tq_step),
        in_specs=[
            pl.BlockSpec((1, 2 * HEAD_PAD, tq_step), lambda b, h, i: (b, h, i)),
            pl.BlockSpec((1, S, 2 * HEAD_PAD), lambda b, h, i: (b, 0, h)),
            pl.BlockSpec((1, n_chunks, 2 * V_ROWS, MLA_TK), lambda b, h, i: (b, 0, h, 0)),
        ],
        out_specs=pl.BlockSpec((1, tq_step, 2 * V_DIM), lambda b, h, i: (b, i, h)),
        out_shape=jax.ShapeDtypeStruct((B, S, MLA_HEADS * V_DIM), BF16),
        compiler_params=_params(3),
        name="mla_attn",
    )(qT, k, vT)


def _na_pre_kernel(x_ref, mod_ref, gains_ref, wqT_ref, wk_ref, wvT_ref, ones_ref,
                   qT_out, k_out, vT_out):
    shift = mod_ref[0, 0:1, :]
    scale = mod_ref[0, 1:2, :]
    for c in range(x_ref.shape[1] // NA_CHUNK):
        r = slice(c * NA_CHUNK, (c + 1) * NA_CHUNK)
        h = (_rms(x_ref[0, r, :], gains_ref[0:1, :]) * (1.0 + scale) + shift).astype(BF16)
        k_out[0, r, :] = _dot(h, wk_ref[...]).astype(BF16)
        qT = _dot_nt(wqT_ref[...], h) * (NA_SCALE * LOG2E)
        zpad = jnp.zeros((NA_HEAD_DIM, qT.shape[1]), BF16)
        for hd in range(NA_HEADS):
            lo = hd * HEAD_PAD + (hd % 2) * NA_HEAD_DIM
            zo = hd * HEAD_PAD + (1 - hd % 2) * NA_HEAD_DIM
            qT_out[0, lo:lo + NA_HEAD_DIM, r] = qT[hd * NA_HEAD_DIM:(hd + 1) * NA_HEAD_DIM].astype(BF16)
            qT_out[0, zo:zo + NA_HEAD_DIM, r] = zpad
        vT = _dot_nt(wvT_ref[...], h) + ones_ref[...]
        vT_out[0, c] = vT.astype(BF16)


def _na_pre(x, mod, gains, w):
    B, S, _ = x.shape
    tm = PRE_TILE if S % PRE_TILE == 0 else TOKEN_TILE
    cpt = tm // NA_CHUNK
    hq = NA_HEADS * HEAD_PAD
    hk = NA_HEADS * NA_HEAD_DIM
    hv = NA_HEADS * V_ROWS
    return pl.pallas_call(
        _na_pre_kernel,
        grid=(B, S // tm),
        in_specs=[
            pl.BlockSpec((1, tm, D_MODEL), lambda b, i: (b, i, 0)),
            pl.BlockSpec((1, N_MOD, D_MODEL), lambda b, i: (b, 0, 0)),
            _const_spec((4, D_MODEL)),
            _const_spec((hk, D_MODEL)),
            _const_spec((D_MODEL, hk)),
            _const_spec((hv, D_MODEL)),
            _const_spec((hv, 1)),
        ],
        out_specs=[
            pl.BlockSpec((1, hq, tm), lambda b, i: (b, 0, i)),
            pl.BlockSpec((1, tm, hk), lambda b, i: (b, i, 0)),
            pl.BlockSpec((1, cpt, hv, NA_CHUNK), lambda b, i: (b, i, 0, 0)),
        ],
        out_shape=[
            jax.ShapeDtypeStruct((B, hq, S), BF16),
            jax.ShapeDtypeStruct((B, S, hk), BF16),
            jax.ShapeDtypeStruct((B, S // NA_CHUNK, hv, NA_CHUNK), BF16),
        ],
        compiler_params=_params(2),
        name="na_pre",
    )(x, mod, gains, w["wqT"], w["wk"], w["wvT"], w["ones"])


def _na_band_start(blk, n_blk):
    return jnp.clip(blk - 1, 0, n_blk - NA_BAND // NA_QROWS)


def _na_attn_kernel(qT_ref, k_ref, vT_ref, bias_ref, o_ref, *, n_blk, group):
    step = pl.program_id(2)
    tq = NA_QROWS * GRID_W
    n_keys = NA_BAND * GRID_W
    n_chunks = n_keys // NA_CHUNK
    band0 = [_na_band_start(step * group + g, n_blk) for g in range(group)]
    pattern = [jnp.where(step * group + g == 0, 0, jnp.where(step * group + g == n_blk - 1, 2, 1))
               for g in range(group)]

    def scores(g, c, hh):
        kc = k_ref[0, pl.ds(pl.multiple_of((band0[g] + c) * NA_CHUNK, NA_CHUNK), NA_CHUNK), :]
        qT = qT_ref[0, hh * HEAD_PAD:(hh + 1) * HEAD_PAD, g * tq:(g + 1) * tq]
        return _dot(kc, qT) + bias_ref[hh, pattern[g], c * NA_CHUNK:(c + 1) * NA_CHUNK, :]

    steps = [(g, c) for g in range(group) for c in range(n_chunks)]
    pending = {}
    for g, c in steps[:NA_LOOKAHEAD]:
        for hh in range(2):
            pending[g, c, hh] = scores(g, c, hh)
    res = {}
    for i, (g, c) in enumerate(steps):
        for hh in range(2):
            if i + NA_LOOKAHEAD < len(steps):
                ng, nc = steps[i + NA_LOOKAHEAD]
                pending[ng, nc, hh] = scores(ng, nc, hh)
            s = pending.pop((g, c, hh))
            if c == 0:
                m, acc = jnp.full((1, tq), NEG_INF, F32), jnp.zeros((V_ROWS, tq), F32)
            else:
                m, acc = res[hh]
            m_new = jnp.maximum(m, jnp.max(s, axis=0, keepdims=True))
            alpha = jnp.exp2(m - m_new)
            p = jnp.exp2((s - m_new).astype(BF16))
            vT = vT_ref[0, band0[g] + c, hh * V_ROWS:(hh + 1) * V_ROWS, :]
            res[hh] = (m_new, alpha * acc + _dot(vT, p))
        if c == n_chunks - 1:
            pair = jnp.concatenate([res[hh][1][0:NA_HEAD_DIM] / res[hh][1][NA_HEAD_DIM:NA_HEAD_DIM + 1]
                                    for hh in range(2)], axis=0)
            o_ref[0, g * tq:(g + 1) * tq, :] = pair.T.astype(BF16)


def _na_attn(qT, k, vT, bias):
    B, _, S = qT.shape
    tq = NA_QROWS * GRID_W
    n_blk = S // tq
    hp = NA_HEADS // 2
    group = max(g for g in range(1, NA_GROUP + 1) if n_blk % g == 0)
    return pl.pallas_call(
        functools.partial(_na_attn_kernel, n_blk=n_blk, group=group),
        grid=(B, hp, n_blk // group),
        in_specs=[
            pl.BlockSpec((1, 2 * HEAD_PAD, group * tq), lambda b, h, i: (b, h, i)),
            pl.BlockSpec((1, S, 2 * NA_HEAD_DIM), lambda b, h, i: (b, 0, h)),
            pl.BlockSpec((1, S // NA_CHUNK, 2 * V_ROWS, NA_CHUNK), lambda b, h, i: (b, 0, h, 0)),
            pl.BlockSpec((2, 3, NA_BAND * GRID_W, tq), lambda b, h, i: (h, 0, 0, 0)),
        ],
        out_specs=pl.BlockSpec((1, group * tq, 2 * NA_HEAD_DIM), lambda b, h, i: (b, i, h)),
        out_shape=jax.ShapeDtypeStruct((B, S, NA_HEADS * NA_HEAD_DIM), BF16),
        compiler_params=_params(3),
        name="na_attn",
    )(qT, k, vT, bias)


def _na_bias_tables(rpb):
    p = np.arange(3)[:, None]
    qi = np.arange(NA_QROWS)[None, :]
    qr = NA_QROWS * p + qi
    r_start = np.clip(qr - WIN_R // 2, 0, NA_BAND - WIN_R)
    kr = np.arange(NA_BAND)[None, :, None]
    valid_r = (kr >= r_start[:, None, :]) & (kr < r_start[:, None, :] + WIN_R)
    dr = np.clip(kr - qr[:, None, :] + WIN_R - 1, 0, 2 * WIN_R - 2)
    c = np.arange(GRID_W)
    c_start = np.clip(c - WIN_C // 2, 0, GRID_W - WIN_C)
    kc = c[:, None]
    valid_c = (kc >= c_start[None, :]) & (kc < c_start[None, :] + WIN_C)
    dc = np.clip(kc - c[None, :] + WIN_C - 1, 0, 2 * WIN_C - 2)
    valid = valid_r[:, :, None, :, None] & valid_c[None, None, :, None, :]
    row_sel = (dr.reshape(-1)[:, None] == np.arange(2 * WIN_R - 1)[None, :]).astype(np.float32)
    col_sel = (dc[None] == np.arange(2 * WIN_C - 1)[:, None, None]).astype(np.float32)
    rows = jnp.sum(rpb.astype(F32)[:, None, :, :] * row_sel[None, :, :, None], axis=2)
    rows = rows.reshape(NA_HEADS * 3 * NA_BAND, NA_QROWS * (2 * WIN_C - 1))
    col_sel4 = (jnp.eye(NA_QROWS, dtype=F32)[:, None, None, :, None] * col_sel[None, :, :, None, :]
                ).reshape(NA_QROWS * (2 * WIN_C - 1), GRID_W, NA_QROWS * GRID_W)
    bias = jnp.einsum("nj,jkq->nkq", rows, col_sel4, precision=lax.Precision.HIGHEST)
    bias = bias.reshape(NA_HEADS, 3, NA_BAND * GRID_W, NA_QROWS * GRID_W)
    valid = valid.reshape(3, NA_BAND * GRID_W, NA_QROWS * GRID_W)
    return jnp.where(valid[None], bias * LOG2E, NEG_INF)


def _post_ffn_kernel(x_ref, a_ref, mod_ref, gains_ref, wo_ref, wgu_ref, wd_ref, o_ref):
    gate_m = mod_ref[0, 2:3, :]
    shift_f = mod_ref[0, 3:4, :]
    scale_f = mod_ref[0, 4:5, :]
    gate_f = mod_ref[0, 5:6, :]
    n_sub = x_ref.shape[1] // FFN_SUBTILE
    rows = [slice(r * FFN_SUBTILE, (r + 1) * FFN_SUBTILE) for r in range(n_sub)]
    mix = [_dot(a_ref[0, r, :], wo_ref[...]) for r in rows]
    xs, gu, down = [], [], []
    for i in range(n_sub + 2):
        if i < n_sub:
            x = x_ref[0, rows[i], :] + gate_m * _rms(mix[i], gains_ref[1:2, :])
            h = (_rms(x, gains_ref[2:3, :]) * (1.0 + scale_f) + shift_f).astype(BF16)
            xs.append(x)
            gu.append((_dot(h, wgu_ref[:, 0:FFN_HIDDEN]), _dot(h, wgu_ref[:, FFN_HIDDEN:2 * FFN_HIDDEN])))
        if 1 <= i <= n_sub:
            g, u = gu[i - 1]
            act = ((g / (1.0 + jnp.exp(-g))) * u).astype(BF16)
            down.append(_dot(act, wd_ref[...]))
        if i >= 2:
            r = i - 2
            o_ref[0, rows[r], :] = xs[r] + gate_f * _rms(down[r], gains_ref[3:4, :])


def _post_ffn(x, attn, mod, gains, wo, wgu, wd):
    B, S, _ = x.shape
    tm = TOKEN_TILE
    return pl.pallas_call(
        _post_ffn_kernel,
        grid=(B, S // tm),
        in_specs=[
            pl.BlockSpec((1, tm, D_MODEL), lambda b, i: (b, i, 0)),
            pl.BlockSpec((1, tm, D_MODEL), lambda b, i: (b, i, 0)),
            pl.BlockSpec((1, N_MOD, D_MODEL), lambda b, i: (b, 0, 0)),
            _const_spec((4, D_MODEL)),
            _const_spec((D_MODEL, D_MODEL)),
            _const_spec((D_MODEL, 2 * FFN_HIDDEN)),
            _const_spec((FFN_HIDDEN, D_MODEL)),
        ],
        out_specs=pl.BlockSpec((1, tm, D_MODEL), lambda b, i: (b, i, 0)),
        out_shape=jax.ShapeDtypeStruct((B, S, D_MODEL), F32),
        compiler_params=_params(2),
        name="post_ffn",
    )(x, attn, mod, gains, wo, wgu, wd)


def _ones_column(n_heads):
    col = np.zeros((n_heads, V_ROWS, 1), np.float32)
    col[:, V_DIM, 0] = 1.0
    return jnp.asarray(col.reshape(n_heads * V_ROWS, 1))


def _prep_mla(w_dkv, q_norm, kv_norm, w_uq, w_ukv):
    half = QK_ROPE // 2
    r0 = Q_LORA + KV_LORA
    wdkv = jnp.zeros((D_MODEL, LAT_COLS), F32)
    wdkv = wdkv.at[:, :r0 + QK_ROPE].set(w_dkv)
    wdkv = wdkv.at[:, 512:512 + half].set(-w_dkv[:, r0 + half:r0 + QK_ROPE])
    wdkv = wdkv.at[:, 512 + half:512 + QK_ROPE].set(w_dkv[:, r0:r0 + half])
    wq = w_uq.reshape(Q_LORA, MLA_HEADS, QK_NOPE + QK_ROPE)
    wq = jnp.pad(wq, ((0, 0), (0, 0), (0, HEAD_PAD - QK_NOPE - QK_ROPE)))
    wqT = wq.reshape(Q_LORA, MLA_HEADS * HEAD_PAD).T
    wkv = w_ukv.reshape(KV_LORA, MLA_HEADS, QK_NOPE + V_DIM)
    wk_nope = jnp.pad(wkv[:, :, :QK_NOPE], ((0, 0), (0, 0), (0, HEAD_PAD - QK_NOPE)))
    eye = np.zeros((KV_LORA, MLA_HEADS, HEAD_PAD), np.float32)
    for r in range(QK_ROPE):
        eye[r, :, QK_NOPE + r] = 1.0
    wk = jnp.concatenate([wk_nope, jnp.asarray(eye)], axis=0).reshape(2 * KV_LORA, MLA_HEADS * HEAD_PAD)
    wv = jnp.pad(wkv[:, :, QK_NOPE:], ((0, 0), (0, 0), (0, V_ROWS - V_DIM)))
    wvT = wv.reshape(KV_LORA, MLA_HEADS * V_ROWS).T
    return dict(wdkv=wdkv.astype(BF16), qn=q_norm.reshape(1, Q_LORA), kvn=kv_norm.reshape(1, KV_LORA),
                wqT=wqT.astype(BF16), wk=wk.astype(BF16), wvT=wvT.astype(BF16), ones=_ones_column(MLA_HEADS))


def _prep_na(w_qkv):
    hk = NA_HEADS * NA_HEAD_DIM
    wv = w_qkv[:, 2 * hk:].reshape(D_MODEL, NA_HEADS, NA_HEAD_DIM)
    wv = jnp.pad(wv, ((0, 0), (0, 0), (0, V_ROWS - NA_HEAD_DIM)))
    return dict(wqT=w_qkv[:, :hk].T.astype(BF16), wk=w_qkv[:, hk:2 * hk].astype(BF16),
                wvT=wv.reshape(D_MODEL, NA_HEADS * V_ROWS).T.astype(BF16), ones=_ones_column(NA_HEADS))


def _rope_tables(seq_len):
    half = QK_ROPE // 2
    inv_freq = 1.0 / (ROPE_THETA ** (jnp.arange(0, QK_ROPE, 2, dtype=F32) / QK_ROPE))
    ang = jnp.arange(seq_len, dtype=F32)[:, None] * inv_freq[None, :]
    cos, sin = jnp.cos(ang), jnp.sin(ang)
    pad = jnp.zeros((seq_len, 128 - 2 * half), F32)
    ck = jnp.concatenate([cos, cos, pad], axis=1)
    sk = jnp.concatenate([sin, sin, pad], axis=1)
    return ck, sk, cos.T, sin.T


def _trunk(x, mods, gains, mla_w, na_w, na_bias, post_w):
    tables = _rope_tables(x.shape[1])
    for i in range(DEPTH):
        if i % 2 == 0:
            qT, k, vT = _mla_pre(x, mods[i], gains[i], mla_w[i // 2], tables)
            attn = _mla_attn(qT, k, vT)
        else:
            qT, k, vT = _na_pre(x, mods[i], gains[i], na_w[i // 2])
            attn = _na_attn(qT, k, vT, na_bias[i // 2])
        x = _post_ffn(x, attn, mods[i], gains[i], *post_w[i])
    return x


def kernel(x_prompt, x_sample, c_prompt, c_sample, ada_w, ada_b, norm_pre_mix, norm_post_mix, norm_pre_ffn, norm_post_ffn, mla_w_dkv, mla_q_norm, mla_kv_norm, mla_w_uq, mla_w_ukv, mla_w_o, na_w_qkv, na_rpb, na_w_o, ffn_w_gu, ffn_w_down):
    bp = x_prompt.shape[0]
    bs = x_sample.shape[0]
    mod = _ada_mod(jnp.concatenate([c_prompt, c_sample], axis=0), ada_w, ada_b)
    mod = mod.reshape(DEPTH, bp + bs, N_MOD, D_MODEL)
    gains = [jnp.stack([norm_pre_mix[i], norm_post_mix[i], norm_pre_ffn[i], norm_post_ffn[i]]) for i in range(DEPTH)]
    mla_w = [_prep_mla(mla_w_dkv[j], mla_q_norm[j], mla_kv_norm[j], mla_w_uq[j], mla_w_ukv[j])
             for j in range(mla_w_dkv.shape[0])]
    na_w = [_prep_na(na_w_qkv[j]) for j in range(na_w_qkv.shape[0])]
    na_bias = [_na_bias_tables(na_rpb[j]) for j in range(na_rpb.shape[0])]
    post_w = []
    for i in range(DEPTH):
        wo = mla_w_o[i // 2] if i % 2 == 0 else na_w_o[i // 2]
        post_w.append((wo.astype(BF16), ffn_w_gu[i].astype(BF16), ffn_w_down[i].astype(BF16)))
    y_prompt = _trunk(x_prompt, [mod[i, :bp] for i in range(DEPTH)], gains, mla_w, na_w, na_bias, post_w)
    y_sample = _trunk(x_sample, [mod[i, bp:] for i in range(DEPTH)], gains, mla_w, na_w, na_bias, post_w)
    return (y_prompt, y_sample)
```

```python
import functools

import jax
import jax.numpy as jnp
import numpy as np
from jax import lax
from jax.experimental import pallas as pl
from jax.experimental.pallas import tpu as pltpu

F32 = jnp.float32
BF16 = jnp.bfloat16

D_MODEL = 1024
DEPTH = 2
N_MOD = 6
RMS_EPS = 1e-6
NEG_INF = -1e30

MLA_HEADS = 16
Q_LORA = 256
KV_LORA = 128
QK_NOPE = 64
QK_ROPE = 32
V_DIM = 64
ROPE_THETA = 10000.0
MLA_SCALE = (QK_NOPE + QK_ROPE) ** -0.5
LOG2E = 1.4426950408889634
HEAD_PAD = 128
V_ROWS = 80
LAT_COLS = 640

NA_HEADS = 16
NA_HEAD_DIM = 64
GRID_W = 64
WIN_R = 8
WIN_C = 16
NA_QROWS = 4
NA_BAND = 12
NA_SCALE = NA_HEAD_DIM ** -0.5
NA_GROUP = 16
NA_LOOKAHEAD = 3

FFN_HIDDEN = 2816
FFN_SUBTILE = 256

TOKEN_TILE = 512
PRE_TILE = 1024
MLA_TQ = 256
MLA_QBLOCKS = 8
MLA_TK = 256
MLA_LOOKAHEAD = 3
NA_CHUNK = 256

VMEM_LIMIT = 56 * 1024 * 1024

_NT = (((1,), (1,)), ((), ()))


def _dot(a, b):
    return jnp.dot(a, b, preferred_element_type=F32)


def _dot_nt(a, b):
    return lax.dot_general(a, b, _NT, preferred_element_type=F32)


def _rms(x, g):
    ms = jnp.mean(x * x, axis=-1, keepdims=True)
    return x * lax.rsqrt(ms + RMS_EPS) * g


def _const_spec(shape):
    zeros = (0,) * len(shape)
    return pl.BlockSpec(shape, lambda *_: zeros, pipeline_mode=pl.Buffered(1))


def _params(n_axes):
    return pltpu.CompilerParams(
        dimension_semantics=("arbitrary",) * n_axes, vmem_limit_bytes=VMEM_LIMIT)


def _ada_kernel(c_ref, w_ref, b_ref, o_ref):
    c = c_ref[...]
    c_act = c / (1.0 + jnp.exp(-c))
    o_ref[0] = _dot(c_act.astype(BF16), w_ref[0]) + b_ref[0]


def _ada_mod(c_all, ada_w, ada_b):
    n_rows = c_all.shape[0]
    n_out = N_MOD * D_MODEL
    tn = 1536
    return pl.pallas_call(
        _ada_kernel,
        grid=(DEPTH, n_out // tn),
        in_specs=[
            pl.BlockSpec((n_rows, D_MODEL), lambda i, j: (0, 0)),
            pl.BlockSpec((1, D_MODEL, tn), lambda i, j: (i, 0, j)),
            pl.BlockSpec((1, 1, tn), lambda i, j: (i, 0, j)),
        ],
        out_specs=pl.BlockSpec((1, n_rows, tn), lambda i, j: (i, 0, j)),
        out_shape=jax.ShapeDtypeStruct((DEPTH, n_rows, n_out), F32),
        compiler_params=_params(2),
        name="ada_mod",
    )(c_all, ada_w.astype(BF16), ada_b.reshape(DEPTH, 1, n_out))


def _mla_pre_kernel(x_ref, mod_ref, gains_ref, wdkv_ref, qn_ref, kvn_ref, wqT_ref, wk_ref,
                    wvT_ref, ones_ref, ck_ref, sk_ref, cT_ref, sT_ref, qT_out, k_out, vT_out):
    shift = mod_ref[0, 0:1, :]
    scale = mod_ref[0, 1:2, :]
    half = QK_ROPE // 2
    subs = [slice(c * MLA_TK, (c + 1) * MLA_TK) for c in range(x_ref.shape[1] // MLA_TK)]
    lats = []
    for r in subs:
        h = (_rms(x_ref[0, r, :], gains_ref[0:1, :]) * (1.0 + scale) + shift).astype(BF16)
        lats.append(_dot(h, wdkv_ref[...]))
    for c, (r, lat) in enumerate(zip(subs, lats)):
        cq = _rms(lat[:, 0:Q_LORA], qn_ref[...]).astype(BF16)
        ckv = _rms(lat[:, Q_LORA:Q_LORA + KV_LORA], kvn_ref[...]).astype(BF16)
        kr = lat[:, 384:512] * ck_ref[r, :] + lat[:, 512:640] * sk_ref[r, :]
        kin = jnp.concatenate([ckv, kr.astype(BF16)], axis=1)
        k_out[0, r, :] = _dot(kin, wk_ref[...]).astype(BF16)

        qT = _dot_nt(wqT_ref[...], cq) * (MLA_SCALE * LOG2E)
        cT = cT_ref[:, r]
        sT = sT_ref[:, r]
        zpad = jnp.zeros((HEAD_PAD - QK_NOPE - QK_ROPE, qT.shape[1]), BF16)
        for hd in range(MLA_HEADS):
            b0 = hd * HEAD_PAD
            x1 = qT[b0 + QK_NOPE:b0 + QK_NOPE + half]
            x2 = qT[b0 + QK_NOPE + half:b0 + QK_NOPE + QK_ROPE]
            qT_out[0, b0:b0 + QK_NOPE, r] = qT[b0:b0 + QK_NOPE].astype(BF16)
            qT_out[0, b0 + QK_NOPE:b0 + QK_NOPE + half, r] = (x1 * cT - x2 * sT).astype(BF16)
            qT_out[0, b0 + QK_NOPE + half:b0 + QK_NOPE + QK_ROPE, r] = (x2 * cT + x1 * sT).astype(BF16)
            qT_out[0, b0 + QK_NOPE + QK_ROPE:b0 + HEAD_PAD, r] = zpad

        vT = _dot_nt(wvT_ref[...], ckv) + ones_ref[...]
        vT_out[0, c] = vT.astype(BF16)


def _mla_pre(x, mod, gains, w, tables):
    B, S, _ = x.shape
    tm = PRE_TILE if S % PRE_TILE == 0 else TOKEN_TILE
    n_t = S // tm
    cpt = tm // MLA_TK
    ck, sk, cT, sT = tables
    hq = MLA_HEADS * HEAD_PAD
    hv = MLA_HEADS * V_ROWS
    return pl.pallas_call(
        _mla_pre_kernel,
        grid=(B, n_t),
        in_specs=[
            pl.BlockSpec((1, tm, D_MODEL), lambda b, i: (b, i, 0)),
            pl.BlockSpec((1, N_MOD, D_MODEL), lambda b, i: (b, 0, 0)),
            _const_spec((4, D_MODEL)),
            _const_spec((D_MODEL, LAT_COLS)),
            _const_spec((1, Q_LORA)),
            _const_spec((1, KV_LORA)),
            _const_spec((hq, Q_LORA)),
            _const_spec((2 * KV_LORA, hq)),
            _const_spec((hv, KV_LORA)),
            _const_spec((hv, 1)),
            pl.BlockSpec((tm, 128), lambda b, i: (i, 0)),
            pl.BlockSpec((tm, 128), lambda b, i: (i, 0)),
            pl.BlockSpec((QK_ROPE // 2, tm), lambda b, i: (0, i)),
            pl.BlockSpec((QK_ROPE // 2, tm), lambda b, i: (0, i)),
        ],
        out_specs=[
            pl.BlockSpec((1, hq, tm), lambda b, i: (b, 0, i)),
            pl.BlockSpec((1, tm, hq), lambda b, i: (b, i, 0)),
            pl.BlockSpec((1, cpt, hv, MLA_TK), lambda b, i: (b, i, 0, 0)),
        ],
        out_shape=[
            jax.ShapeDtypeStruct((B, hq, S), BF16),
            jax.ShapeDtypeStruct((B, S, hq), BF16),
            jax.ShapeDtypeStruct((B, S // MLA_TK, hv, MLA_TK), BF16),
        ],
        compiler_params=_params(2),
        name="mla_pre",
    )(x, mod, gains, w["wdkv"], w["qn"], w["kvn"], w["wqT"], w["wk"], w["wvT"], w["ones"],
      ck, sk, cT, sT)


def _mla_attn_kernel(qT_ref, k_ref, vT_ref, o_ref, *, n_chunks):
    tq = MLA_TQ
    n_qb = qT_ref.shape[2] // tq

    def scores(qb, j, hh):
        k = k_ref[0, j * MLA_TK:(j + 1) * MLA_TK, hh * HEAD_PAD:(hh + 1) * HEAD_PAD]
        return _dot(k, qT_ref[0, hh * HEAD_PAD:(hh + 1) * HEAD_PAD, qb * tq:(qb + 1) * tq])

    steps = [(qb, j) for qb in range(n_qb) for j in range(n_chunks)]
    pending = {}
    for qb, j in steps[:MLA_LOOKAHEAD]:
        for hh in range(2):
            pending[qb, j, hh] = scores(qb, j, hh)
    res = {}
    for i, (qb, j) in enumerate(steps):
        for hh in range(2):
            if i + MLA_LOOKAHEAD < len(steps):
                nqb, nj = steps[i + MLA_LOOKAHEAD]
                pending[nqb, nj, hh] = scores(nqb, nj, hh)
            s = pending.pop((qb, j, hh))
            if j == 0:
                m, acc = jnp.full((1, tq), NEG_INF, F32), jnp.zeros((V_ROWS, tq), F32)
            else:
                m, acc = res[hh]
            m_new = jnp.maximum(m, jnp.max(s, axis=0, keepdims=True))
            alpha = jnp.exp2(m - m_new)
            p = jnp.exp2(s - m_new).astype(BF16)
            vT = vT_ref[0, j, hh * V_ROWS:(hh + 1) * V_ROWS, :]
            res[hh] = (m_new, alpha * acc + _dot(vT, p))
        if j == n_chunks - 1:
            oT = jnp.concatenate([res[hh][1][0:V_DIM] / res[hh][1][V_DIM:V_DIM + 1] for hh in range(2)], axis=0)
            o_ref[0, qb * tq:(qb + 1) * tq, :] = oT.T.astype(BF16)


def _mla_attn(qT, k, vT):
    B, _, S = qT.shape
    n_chunks = S // MLA_TK
    hp = MLA_HEADS // 2
    n_qb = max(g for g in range(1, MLA_QBLOCKS + 1) if (S // MLA_TQ) % g == 0)
    tq_step = MLA_TQ * n_qb
    return pl.pallas_call(
        functools.partial(_mla_attn_kernel, n_chunks=n_chunks),
        grid=(B, hp, S // tq_step),
        in_specs=[
            pl.BlockSpec((1, 2 * HEAD_PAD, tq_step), lambda b, h, i: (b, h, i)),
            pl.BlockSpec((1, S, 2 * HEAD_PAD), lambda b, h, i: (b, 0, h)),
            pl.BlockSpec((1, n_chunks, 2 * V_ROWS, MLA_TK), lambda b, h, i: (b, 0, h, 0)),
        ],
        out_specs=pl.BlockSpec((1, tq_step, 2 * V_DIM), lambda b, h, i: (b, i, h)),
        out_shape=jax.ShapeDtypeStruct((B, S, MLA_HEADS * V_DIM), BF16),
        compiler_params=_params(3),
        name="mla_attn",
    )(qT, k, vT)


def _na_pre_kernel(x_ref, mod_ref, gains_ref, wqT_ref, wk_ref, wvT_ref, ones_ref,
                   qT_out, k_out, vT_out):
    shift = mod_ref[0, 0:1, :]
    scale = mod_ref[0, 1:2, :]
    for c in range(x_ref.shape[1] // NA_CHUNK):
        r = slice(c * NA_CHUNK, (c + 1) * NA_CHUNK)
        h = (_rms(x_ref[0, r, :], gains_ref[0:1, :]) * (1.0 + scale) + shift).astype(BF16)
        k_out[0, r, :] = _dot(h, wk_ref[...]).astype(BF16)
        qT = _dot_nt(wqT_ref[...], h) * (NA_SCALE * LOG2E)
        zpad = jnp.zeros((NA_HEAD_DIM, qT.shape[1]), BF16)
        for hd in range(NA_HEADS):
            lo = hd * HEAD_PAD + (hd % 2) * NA_HEAD_DIM
            zo = hd * HEAD_PAD + (1 - hd % 2) * NA_HEAD_DIM
            qT_out[0, lo:lo + NA_HEAD_DIM, r] = qT[hd * NA_HEAD_DIM:(hd + 1) * NA_HEAD_DIM].astype(BF16)
            qT_out[0, zo:zo + NA_HEAD_DIM, r] = zpad
        vT = _dot_nt(wvT_ref[...], h) + ones_ref[...]
        vT_out[0, c] = vT.astype(BF16)


def _na_pre(x, mod, gains, w):
    B, S, _ = x.shape
    tm = PRE_TILE if S % PRE_TILE == 0 else TOKEN_TILE
    cpt = tm // NA_CHUNK
    hq = NA_HEADS * HEAD_PAD
    hk = NA_HEADS * NA_HEAD_DIM
    hv = NA_HEADS * V_ROWS
    return pl.pallas_call(
        _na_pre_kernel,
        grid=(B, S // tm),
        in_specs=[
            pl.BlockSpec((1, tm, D_MODEL), lambda b, i: (b, i, 0)),
            pl.BlockSpec((1, N_MOD, D_MODEL), lambda b, i: (b, 0, 0)),
            _const_spec((4, D_MODEL)),
            _const_spec((hk, D_MODEL)),
            _const_spec((D_MODEL, hk)),
            _const_spec((hv, D_MODEL)),
            _const_spec((hv, 1)),
        ],
        out_specs=[
            pl.BlockSpec((1, hq, tm), lambda b, i: (b, 0, i)),
            pl.BlockSpec((1, tm, hk), lambda b, i: (b, i, 0)),
            pl.BlockSpec((1, cpt, hv, NA_CHUNK), lambda b, i: (b, i, 0, 0)),
        ],
        out_shape=[
            jax.ShapeDtypeStruct((B, hq, S), BF16),
            jax.ShapeDtypeStruct((B, S, hk), BF16),
            jax.ShapeDtypeStruct((B, S // NA_CHUNK, hv, NA_CHUNK), BF16),
        ],
        compiler_params=_params(2),
        name="na_pre",
    )(x, mod, gains, w["wqT"], w["wk"], w["wvT"], w["ones"])


def _na_band_start(blk, n_blk):
    return jnp.clip(blk - 1, 0, n_blk - NA_BAND // NA_QROWS)


def _na_attn_kernel(qT_ref, k_ref, vT_ref, bias_ref, o_ref, *, n_blk, group):
    step = pl.program_id(2)
    tq = NA_QROWS * GRID_W
    n_keys = NA_BAND * GRID_W
    n_chunks = n_keys // NA_CHUNK
    band0 = [_na_band_start(step * group + g, n_blk) for g in range(group)]
    pattern = [jnp.where(step * group + g == 0, 0, jnp.where(step * group + g == n_blk - 1, 2, 1))
               for g in range(group)]

    def scores(g, c, hh):
        kc = k_ref[0, pl.ds(pl.multiple_of((band0[g] + c) * NA_CHUNK, NA_CHUNK), NA_CHUNK), :]
        qT = qT_ref[0, hh * HEAD_PAD:(hh + 1) * HEAD_PAD, g * tq:(g + 1) * tq]
        return _dot(kc, qT) + bias_ref[hh, pattern[g], c * NA_CHUNK:(c + 1) * NA_CHUNK, :]

    steps = [(g, c) for g in range(group) for c in range(n_chunks)]
    pending = {}
    for g, c in steps[:NA_LOOKAHEAD]:
        for hh in range(2):
            pending[g, c, hh] = scores(g, c, hh)
    res = {}
    for i, (g, c) in enumerate(steps):
        for hh in range(2):
            if i + NA_LOOKAHEAD < len(steps):
                ng, nc = steps[i + NA_LOOKAHEAD]
                pending[ng, nc, hh] = scores(ng, nc, hh)
            s = pending.pop((g, c, hh))
            if c == 0:
                m, acc = jnp.full((1, tq), NEG_INF, F32), jnp.zeros((V_ROWS, tq), F32)
            else:
                m, acc = res[hh]
            m_new = jnp.maximum(m, jnp.max(s, axis=0, keepdims=True))
            alpha = jnp.exp2(m - m_new)
            p = jnp.exp2((s - m_new).astype(BF16))
            vT = vT_ref[0, band0[g] + c, hh * V_ROWS:(hh + 1) * V_ROWS, :]
            res[hh] = (m_new, alpha * acc + _dot(vT, p))
        if c == n_chunks - 1:
            pair = jnp.concatenate([res[hh][1][0:NA_HEAD_DIM] / res[hh][1][NA_HEAD_DIM:NA_HEAD_DIM + 1]
                                    for hh in range(2)], axis=0)
            o_ref[0, g * tq:(g + 1) * tq, :] = pair.T.astype(BF16)


def _na_attn(qT, k, vT, bias):
    B, _, S = qT.shape
    tq = NA_QROWS * GRID_W
    n_blk = S // tq
    hp = NA_HEADS // 2
    group = max(g for g in range(1, NA_GROUP + 1) if n_blk % g == 0)
    return pl.pallas_call(
        functools.partial(_na_attn_kernel, n_blk=n_blk, group=group),
        grid=(B, hp, n_blk // group),
        in_specs=[
            pl.BlockSpec((1, 2 * HEAD_PAD, group * tq), lambda b, h, i: (b, h, i)),
            pl.BlockSpec((1, S, 2 * NA_HEAD_DIM), lambda b, h, i: (b, 0, h)),
            pl.BlockSpec((1, S // NA_CHUNK, 2 * V_ROWS, NA_CHUNK), lambda b, h, i: (b, 0, h, 0)),
            pl.BlockSpec((2, 3, NA_BAND * GRID_W, tq), lambda b, h, i: (h, 0, 0, 0)),
        ],
        out_specs=pl.BlockSpec((1, group * tq, 2 * NA_HEAD_DIM), lambda b, h, i: (b, i, h)),
        out_shape=jax.ShapeDtypeStruct((B, S, NA_HEADS * NA_HEAD_DIM), BF16),
        compiler_params=_params(3),
        name="na_attn",
    )(qT, k, vT, bias)


def _na_bias_tables(rpb):
    p = np.arange(3)[:, None]
    qi = np.arange(NA_QROWS)[None, :]
    qr = NA_QROWS * p + qi
    r_start = np.clip(qr - WIN_R // 2, 0, NA_BAND - WIN_R)
    kr = np.arange(NA_BAND)[None, :, None]
    valid_r = (kr >= r_start[:, None, :]) & (kr < r_start[:, None, :] + WIN_R)
    dr = np.clip(kr - qr[:, None, :] + WIN_R - 1, 0, 2 * WIN_R - 2)
    c = np.arange(GRID_W)
    c_start = np.clip(c - WIN_C // 2, 0, GRID_W - WIN_C)
    kc = c[:, None]
    valid_c = (kc >= c_start[None, :]) & (kc < c_start[None, :] + WIN_C)
    dc = np.clip(kc - c[None, :] + WIN_C - 1, 0, 2 * WIN_C - 2)
    valid = valid_r[:, :, None, :, None] & valid_c[None, None, :, None, :]
    row_sel = (dr.reshape(-1)[:, None] == np.arange(2 * WIN_R - 1)[None, :]).astype(np.float32)
    col_sel = (dc[None] == np.arange(2 * WIN_C - 1)[:, None, None]).astype(np.float32)
    rows = jnp.sum(rpb.astype(F32)[:, None, :, :] * row_sel[None, :, :, None], axis=2)
    rows = rows.reshape(NA_HEADS * 3 * NA_BAND, NA_QROWS * (2 * WIN_C - 1))
    col_sel4 = (np.eye(NA_QROWS, dtype=np.float32)[:, None, None, :, None] * col_sel[None, :, :, None, :]
                ).reshape(NA_QROWS * (2 * WIN_C - 1), GRID_W, NA_QROWS * GRID_W)
    bias = jnp.einsum("nj,jkq->nkq", rows, col_sel4, precision=lax.Precision.HIGHEST)
    bias = bias.reshape(NA_HEADS, 3, NA_BAND * GRID_W, NA_QROWS * GRID_W)
    valid = valid.reshape(3, NA_BAND * GRID_W, NA_QROWS * GRID_W)
    return jnp.where(valid[None], bias * LOG2E, NEG_INF)


def _post_ffn_kernel(x_ref, a_ref, mod_ref, gains_ref, wo_ref, wgu_ref, wd_ref, o_ref):
    gate_m = mod_ref[0, 2:3, :]
    shift_f = mod_ref[0, 3:4, :]
    scale_f = mod_ref[0, 4:5, :]
    gate_f = mod_ref[0, 5:6, :]
    n_sub = x_ref.shape[1] // FFN_SUBTILE
    rows = [slice(r * FFN_SUBTILE, (r + 1) * FFN_SUBTILE) for r in range(n_sub)]
    mix = [_dot(a_ref[0, r, :], wo_ref[...]) for r in rows]
    xs, gu, down = [], [], []
    for i in range(n_sub + 2):
        if i < n_sub:
            x = x_ref[0, rows[i], :] + gate_m * _rms(mix[i], gains_ref[1:2, :])
            h = (_rms(x, gains_ref[2:3, :]) * (1.0 + scale_f) + shift_f).astype(BF16)
            xs.append(x)
            gu.append((_dot(h, wgu_ref[:, 0:FFN_HIDDEN]), _dot(h, wgu_ref[:, FFN_HIDDEN:2 * FFN_HIDDEN])))
        if 1 <= i <= n_sub:
            g, u = gu[i - 1]
            act = ((g / (1.0 + jnp.exp(-g))) * u).astype(BF16)
            down.append(_dot(act, wd_ref[...]))
        if i >= 2:
            r = i - 2
            o_ref[0, rows[r], :] = xs[r] + gate_f * _rms(down[r], gains_ref[3:4, :])


def _post_ffn(x, attn, mod, gains, wo, wgu, wd):
    B, S, _ = x.shape
    tm = TOKEN_TILE
    return pl.pallas_call(
        _post_ffn_kernel,
        grid=(B, S // tm),
        in_specs=[
            pl.BlockSpec((1, tm, D_MODEL), lambda b, i: (b, i, 0)),
            pl.BlockSpec((1, tm, D_MODEL), lambda b, i: (b, i, 0)),
            pl.BlockSpec((1, N_MOD, D_MODEL), lambda b, i: (b, 0, 0)),
            _const_spec((4, D_MODEL)),
            _const_spec((D_MODEL, D_MODEL)),
            _const_spec((D_MODEL, 2 * FFN_HIDDEN)),
            _const_spec((FFN_HIDDEN, D_MODEL)),
        ],
        out_specs=pl.BlockSpec((1, tm, D_MODEL), lambda b, i: (b, i, 0)),
        out_shape=jax.ShapeDtypeStruct((B, S, D_MODEL), F32),
        compiler_params=_params(2),
        name="post_ffn",
    )(x, attn, mod, gains, wo, wgu, wd)


def _ones_column(n_heads):
    col = np.zeros((n_heads, V_ROWS, 1), np.float32)
    col[:, V_DIM, 0] = 1.0
    return jnp.asarray(col.reshape(n_heads * V_ROWS, 1))


def _prep_mla(w_dkv, q_norm, kv_norm, w_uq, w_ukv):
    half = QK_ROPE // 2
    r0 = Q_LORA + KV_LORA
    wdkv = jnp.zeros((D_MODEL, LAT_COLS), F32)
    wdkv = wdkv.at[:, :r0 + QK_ROPE].set(w_dkv)
    wdkv = wdkv.at[:, 512:512 + half].set(-w_dkv[:, r0 + half:r0 + QK_ROPE])
    wdkv = wdkv.at[:, 512 + half:512 + QK_ROPE].set(w_dkv[:, r0:r0 + half])
    wq = w_uq.reshape(Q_LORA, MLA_HEADS, QK_NOPE + QK_ROPE)
    wq = jnp.pad(wq, ((0, 0), (0, 0), (0, HEAD_PAD - QK_NOPE - QK_ROPE)))
    wqT = wq.reshape(Q_LORA, MLA_HEADS * HEAD_PAD).T
    wkv = w_ukv.reshape(KV_LORA, MLA_HEADS, QK_NOPE + V_DIM)
    wk_nope = jnp.pad(wkv[:, :, :QK_NOPE], ((0, 0), (0, 0), (0, HEAD_PAD - QK_NOPE)))
    eye = np.zeros((KV_LORA, MLA_HEADS, HEAD_PAD), np.float32)
    for r in range(QK_ROPE):
        eye[r, :, QK_NOPE + r] = 1.0
    wk = jnp.concatenate([wk_nope, jnp.asarray(eye)], axis=0).reshape(2 * KV_LORA, MLA_HEADS * HEAD_PAD)
    wv = jnp.pad(wkv[:, :, QK_NOPE:], ((0, 0), (0, 0), (0, V_ROWS - V_DIM)))
    wvT = wv.reshape(KV_LORA, MLA_HEADS * V_ROWS).T
    return dict(wdkv=wdkv.astype(BF16), qn=q_norm.reshape(1, Q_LORA), kvn=kv_norm.reshape(1, KV_LORA),
                wqT=wqT.astype(BF16), wk=wk.astype(BF16), wvT=wvT.astype(BF16), ones=_ones_column(MLA_HEADS))


def _prep_na(w_qkv):
    hk = NA_HEADS * NA_HEAD_DIM
    wv = w_qkv[:, 2 * hk:].reshape(D_MODEL, NA_HEADS, NA_HEAD_DIM)
    wv = jnp.pad(wv, ((0, 0), (0, 0), (0, V_ROWS - NA_HEAD_DIM)))
    return dict(wqT=w_qkv[:, :hk].T.astype(BF16), wk=w_qkv[:, hk:2 * hk].astype(BF16),
                wvT=wv.reshape(D_MODEL, NA_HEADS * V_ROWS).T.astype(BF16), ones=_ones_column(NA_HEADS))


def _rope_tables(seq_len):
    half = QK_ROPE // 2
    inv_freq = 1.0 / (ROPE_THETA ** (jnp.arange(0, QK_ROPE, 2, dtype=F32) / QK_ROPE))
    ang = jnp.arange(seq_len, dtype=F32)[:, None] * inv_freq[None, :]
    cos, sin = jnp.cos(ang), jnp.sin(ang)
    pad = jnp.zeros((seq_len, 128 - 2 * half), F32)
    ck = jnp.concatenate([cos, cos, pad], axis=1)
    sk = jnp.concatenate([sin, sin, pad], axis=1)
    return ck, sk, cos.T, sin.T


def _trunk(x, mods, gains, mla_w, na_w, na_bias, post_w):
    tables = _rope_tables(x.shape[1])
    for i in range(DEPTH):
        if i % 2 == 0:
            qT, k, vT = _mla_pre(x, mods[i], gains[i], mla_w[i // 2], tables)
            attn = _mla_attn(qT, k, vT)
        else:
            qT, k, vT = _na_pre(x, mods[i], gains[i], na_w[i // 2])
            attn = _na_attn(qT, k, vT, na_bias[i // 2])
        x = _post_ffn(x, attn, mods[i], gains[i], *post_w[i])
    return x


def kernel(x_prompt, x_sample, c_prompt, c_sample, ada_w, ada_b, norm_pre_mix, norm_post_mix, norm_pre_ffn, norm_post_ffn, mla_w_dkv, mla_q_norm, mla_kv_norm, mla_w_uq, mla_w_ukv, mla_w_o, na_w_qkv, na_rpb, na_w_o, ffn_w_gu, ffn_w_down):
    bp = x_prompt.shape[0]
    bs = x_sample.shape[0]
    mod = _ada_mod(jnp.concatenate([c_prompt, c_sample], axis=0), ada_w, ada_b)
    mod = mod.reshape(DEPTH, bp + bs, N_MOD, D_MODEL)
    gains = [jnp.stack([norm_pre_mix[i], norm_post_mix[i], norm_pre_ffn[i], norm_post_ffn[i]]) for i in range(DEPTH)]
    mla_w = [_prep_mla(mla_w_dkv[j], mla_q_norm[j], mla_kv_norm[j], mla_w_uq[j], mla_w_ukv[j])
             for j in range(mla_w_dkv.shape[0])]
    na_w = [_prep_na(na_w_qkv[j]) for j in range(na_w_qkv.shape[0])]
    na_bias = [_na_bias_tables(na_rpb[j]) for j in range(na_rpb.shape[0])]
    post_w = []
    for i in range(DEPTH):
        wo = mla_w_o[i // 2] if i % 2 == 0 else na_w_o[i // 2]
        post_w.append((wo.astype(BF16), ffn_w_gu[i].astype(BF16), ffn_w_down[i].astype(BF16)))
    y_prompt = _trunk(x_prompt, [mod[i, :bp] for i in range(DEPTH)], gains, mla_w, na_w, na_bias, post_w)
    y_sample = _trunk(x_sample, [mod[i, bp:] for i in range(DEPTH)], gains, mla_w, na_w, na_bias, post_w)
    return (y_prompt, y_sample)
```

```python
import functools

import jax
import jax.numpy as jnp
import numpy as np
from jax import lax
from jax.experimental import pallas as pl
from jax.experimental.pallas import tpu as pltpu

F32 = jnp.float32
BF16 = jnp.bfloat16

D_MODEL = 1024
DEPTH = 2
N_MOD = 6
RMS_EPS = 1e-6
NEG_INF = -1e30

MLA_HEADS = 16
Q_LORA = 256
KV_LORA = 128
QK_NOPE = 64
QK_ROPE = 32
V_DIM = 64
ROPE_THETA = 10000.0
MLA_SCALE = (QK_NOPE + QK_ROPE) ** -0.5
LOG2E = 1.4426950408889634
HEAD_PAD = 128
V_ROWS = 80
LAT_COLS = 640

NA_HEADS = 16
NA_HEAD_DIM = 64
GRID_W = 64
WIN_R = 8
WIN_C = 16
NA_QROWS = 4
NA_BAND = 12
NA_SCALE = NA_HEAD_DIM ** -0.5
NA_GROUP = 16
NA_LOOKAHEAD = 3

FFN_HIDDEN = 2816
FFN_SUBTILE = 256

TOKEN_TILE = 512
PRE_TILE = 1024
MLA_TQ = 256
MLA_QBLOCKS = 8
MLA_TK = 256
MLA_LOOKAHEAD = 3
NA_CHUNK = 256

VMEM_LIMIT = 56 * 1024 * 1024

_NT = (((1,), (1,)), ((), ()))


def _dot(a, b):
    return jnp.dot(a, b, preferred_element_type=F32)


def _dot_nt(a, b):
    return lax.dot_general(a, b, _NT, preferred_element_type=F32)


def _rms(x, g):
    ms = jnp.mean(x * x, axis=-1, keepdims=True)
    return x * lax.rsqrt(ms + RMS_EPS) * g


def _const_spec(shape):
    zeros = (0,) * len(shape)
    return pl.BlockSpec(shape, lambda *_: zeros, pipeline_mode=pl.Buffered(1))


def _params(n_axes):
    return pltpu.CompilerParams(
        dimension_semantics=("arbitrary",) * n_axes, vmem_limit_bytes=VMEM_LIMIT)


def _ada_kernel(c_ref, w_ref, b_ref, o_ref):
    c = c_ref[...]
    c_act = c / (1.0 + jnp.exp(-c))
    o_ref[0] = _dot(c_act.astype(BF16), w_ref[0]) + b_ref[0]


def _ada_mod(c_all, ada_w, ada_b):
    n_rows = c_all.shape[0]
    n_out = N_MOD * D_MODEL
    tn = 1536
    return pl.pallas_call(
        _ada_kernel,
        grid=(DEPTH, n_out // tn),
        in_specs=[
            pl.BlockSpec((n_rows, D_MODEL), lambda i, j: (0, 0)),
            pl.BlockSpec((1, D_MODEL, tn), lambda i, j: (i, 0, j)),
            pl.BlockSpec((1, 1, tn), lambda i, j: (i, 0, j)),
        ],
        out_specs=pl.BlockSpec((1, n_rows, tn), lambda i, j: (i, 0, j)),
        out_shape=jax.ShapeDtypeStruct((DEPTH, n_rows, n_out), F32),
        compiler_params=_params(2),
        name="ada_mod",
    )(c_all, ada_w.astype(BF16), ada_b.reshape(DEPTH, 1, n_out))


def _mla_pre_kernel(x_ref, mod_ref, gains_ref, wdkv_ref, qn_ref, kvn_ref, wqT_ref, wk_ref,
                    wvT_ref, ones_ref, ck_ref, sk_ref, cT_ref, sT_ref, qT_out, k_out, vT_out):
    shift = mod_ref[0, 0:1, :]
    scale = mod_ref[0, 1:2, :]
    half = QK_ROPE // 2
    subs = [slice(c * MLA_TK, (c + 1) * MLA_TK) for c in range(x_ref.shape[1] // MLA_TK)]
    lats = []
    for r in subs:
        h = (_rms(x_ref[0, r, :], gains_ref[0:1, :]) * (1.0 + scale) + shift).astype(BF16)
        lats.append(_dot(h, wdkv_ref[...]))
    for c, (r, lat) in enumerate(zip(subs, lats)):
        cq = _rms(lat[:, 0:Q_LORA], qn_ref[...]).astype(BF16)
        ckv = _rms(lat[:, Q_LORA:Q_LORA + KV_LORA], kvn_ref[...]).astype(BF16)
        kr = lat[:, 384:512] * ck_ref[r, :] + lat[:, 512:640] * sk_ref[r, :]
        kin = jnp.concatenate([ckv, kr.astype(BF16)], axis=1)
        k_out[0, r, :] = _dot(kin, wk_ref[...]).astype(BF16)

        qT = _dot_nt(wqT_ref[...], cq) * (MLA_SCALE * LOG2E)
        cT = cT_ref[:, r]
        sT = sT_ref[:, r]
        zpad = jnp.zeros((HEAD_PAD - QK_NOPE - QK_ROPE, qT.shape[1]), BF16)
        for hd in range(MLA_HEADS):
            b0 = hd * HEAD_PAD
            x1 = qT[b0 + QK_NOPE:b0 + QK_NOPE + half]
            x2 = qT[b0 + QK_NOPE + half:b0 + QK_NOPE + QK_ROPE]
            qT_out[0, b0:b0 + QK_NOPE, r] = qT[b0:b0 + QK_NOPE].astype(BF16)
            qT_out[0, b0 + QK_NOPE:b0 + QK_NOPE + half, r] = (x1 * cT - x2 * sT).astype(BF16)
            qT_out[0, b0 + QK_NOPE + half:b0 + QK_NOPE + QK_ROPE, r] = (x2 * cT + x1 * sT).astype(BF16)
            qT_out[0, b0 + QK_NOPE + QK_ROPE:b0 + HEAD_PAD, r] = zpad

        vT = _dot_nt(wvT_ref[...], ckv) + ones_ref[...]
        vT_out[0, c] = vT.astype(BF16)


def _mla_pre(x, mod, gains, w, tables):
    B, S, _ = x.shape
    tm = PRE_TILE if S % PRE_TILE == 0 else TOKEN_TILE
    n_t = S // tm
    cpt = tm // MLA_TK
    ck, sk, cT, sT = tables
    hq = MLA_HEADS * HEAD_PAD
    hv = MLA_HEADS * V_ROWS
    return pl.pallas_call(
        _mla_pre_kernel,
        grid=(B, n_t),
        in_specs=[
            pl.BlockSpec((1, tm, D_MODEL), lambda b, i: (b, i, 0)),
            pl.BlockSpec((1, N_MOD, D_MODEL), lambda b, i: (b, 0, 0)),
            _const_spec((4, D_MODEL)),
            _const_spec((D_MODEL, LAT_COLS)),
            _const_spec((1, Q_LORA)),
            _const_spec((1, KV_LORA)),
            _const_spec((hq, Q_LORA)),
            _const_spec((2 * KV_LORA, hq)),
            _const_spec((hv, KV_LORA)),
            _const_spec((hv, 1)),
            pl.BlockSpec((tm, 128), lambda b, i: (i, 0)),
            pl.BlockSpec((tm, 128), lambda b, i: (i, 0)),
            pl.BlockSpec((QK_ROPE // 2, tm), lambda b, i: (0, i)),
            pl.BlockSpec((QK_ROPE // 2, tm), lambda b, i: (0, i)),
        ],
        out_specs=[
            pl.BlockSpec((1, hq, tm), lambda b, i: (b, 0, i)),
            pl.BlockSpec((1, tm, hq), lambda b, i: (b, i, 0)),
            pl.BlockSpec((1, cpt, hv, MLA_TK), lambda b, i: (b, i, 0, 0)),
        ],
        out_shape=[
            jax.ShapeDtypeStruct((B, hq, S), BF16),
            jax.ShapeDtypeStruct((B, S, hq), BF16),
            jax.ShapeDtypeStruct((B, S // MLA_TK, hv, MLA_TK), BF16),
        ],
        compiler_params=_params(2),
        name="mla_pre",
    )(x, mod, gains, w["wdkv"], w["qn"], w["kvn"], w["wqT"], w["wk"], w["wvT"], w["ones"],
      ck, sk, cT, sT)


def _mla_attn_kernel(qT_ref, k_ref, vT_ref, o_ref, *, n_chunks):
    tq = MLA_TQ
    n_qb = qT_ref.shape[2] // tq

    def scores(qb, j, hh):
        k = k_ref[0, j * MLA_TK:(j + 1) * MLA_TK, hh * HEAD_PAD:(hh + 1) * HEAD_PAD]
        return _dot(k, qT_ref[0, hh * HEAD_PAD:(hh + 1) * HEAD_PAD, qb * tq:(qb + 1) * tq])

    steps = [(qb, j) for qb in range(n_qb) for j in range(n_chunks)]
    pending = {}
    for qb, j in steps[:MLA_LOOKAHEAD]:
        for hh in range(2):
            pending[qb, j, hh] = scores(qb, j, hh)
    res = {}
    for i, (qb, j) in enumerate(steps):
        for hh in range(2):
            if i + MLA_LOOKAHEAD < len(steps):
                nqb, nj = steps[i + MLA_LOOKAHEAD]
                pending[nqb, nj, hh] = scores(nqb, nj, hh)
            s = pending.pop((qb, j, hh))
            if j == 0:
                m, acc = jnp.full((1, tq), NEG_INF, F32), jnp.zeros((V_ROWS, tq), F32)
            else:
                m, acc = res[hh]
            m_new = jnp.maximum(m, jnp.max(s, axis=0, keepdims=True))
            alpha = jnp.exp2(m - m_new)
            p = jnp.exp2(s - m_new).astype(BF16)
            vT = vT_ref[0, j, hh * V_ROWS:(hh + 1) * V_ROWS, :]
            res[hh] = (m_new, alpha * acc + _dot(vT, p))
        if j == n_chunks - 1:
            oT = jnp.concatenate([res[hh][1][0:V_DIM] / res[hh][1][V_DIM:V_DIM + 1] for hh in range(2)], axis=0)
            o_ref[0, qb * tq:(qb + 1) * tq, :] = oT.T.astype(BF16)


def _mla_attn(qT, k, vT):
    B, _, S = qT.shape
    n_chunks = S // MLA_TK
    hp = MLA_HEADS // 2
    n_qb = max(g for g in range(1, MLA_QBLOCKS + 1) if (S // MLA_TQ) % g == 0)
    tq_step = MLA_TQ * n_qb
    return pl.pallas_call(
        functools.partial(_mla_attn_kernel, n_chunks=n_chunks),
        grid=(B, hp, S // tq_step),
        in_specs=[
            pl.BlockSpec((1, 2 * HEAD_PAD, tq_step), lambda b, h, i: (b, h, i)),
            pl.BlockSpec((1, S, 2 * HEAD_PAD), lambda b, h, i: (b, 0, h)),
            pl.BlockSpec((1, n_chunks, 2 * V_ROWS, MLA_TK), lambda b, h, i: (b, 0, h, 0)),
        ],
        out_specs=pl.BlockSpec((1, tq_step, 2 * V_DIM), lambda b, h, i: (b, i, h)),
        out_shape=jax.ShapeDtypeStruct((B, S, MLA_HEADS * V_DIM), BF16),
        compiler_params=_params(3),
        name="mla_attn",
    )(qT, k, vT)


def _na_pre_kernel(x_ref, mod_ref, gains_ref, wqT_ref, wk_ref, wvT_ref, ones_ref,
                   qT_out, k_out, vT_out):
    shift = mod_ref[0, 0:1, :]
    scale = mod_ref[0, 1:2, :]
    for c in range(x_ref.shape[1] // NA_CHUNK):
        r = slice(c * NA_CHUNK, (c + 1) * NA_CHUNK)
        h = (_rms(x_ref[0, r, :], gains_ref[0:1, :]) * (1.0 + scale) + shift).astype(BF16)
        k_out[0, r, :] = _dot(h, wk_ref[...]).astype(BF16)
        qT = _dot_nt(wqT_ref[...], h) * (NA_SCALE * LOG2E)
        zpad = jnp.zeros((NA_HEAD_DIM, qT.shape[1]), BF16)
        for hd in range(NA_HEADS):
            lo = hd * HEAD_PAD + (hd % 2) * NA_HEAD_DIM
            zo = hd * HEAD_PAD + (1 - hd % 2) * NA_HEAD_DIM
            qT_out[0, lo:lo + NA_HEAD_DIM, r] = qT[hd * NA_HEAD_DIM:(hd + 1) * NA_HEAD_DIM].astype(BF16)
            qT_out[0, zo:zo + NA_HEAD_DIM, r] = zpad
        vT = _dot_nt(wvT_ref[...], h) + ones_ref[...]
        vT_out[0, c] = vT.astype(BF16)


def _na_pre(x, mod, gains, w):
    B, S, _ = x.shape
    tm = PRE_TILE if S % PRE_TILE == 0 else TOKEN_TILE
    cpt = tm // NA_CHUNK
    hq = NA_HEADS * HEAD_PAD
    hk = NA_HEADS * NA_HEAD_DIM
    hv = NA_HEADS * V_ROWS
    return pl.pallas_call(
        _na_pre_kernel,
        grid=(B, S // tm),
        in_specs=[
            pl.BlockSpec((1, tm, D_MODEL), lambda b, i: (b, i, 0)),
            pl.BlockSpec((1, N_MOD, D_MODEL), lambda b, i: (b, 0, 0)),
            _const_spec((4, D_MODEL)),
            _const_spec((hk, D_MODEL)),
            _const_spec((D_MODEL, hk)),
            _const_spec((hv, D_MODEL)),
            _const_spec((hv, 1)),
        ],
        out_specs=[
            pl.BlockSpec((1, hq, tm), lambda b, i: (b, 0, i)),
            pl.BlockSpec((1, tm, hk), lambda b, i: (b, i, 0)),
            pl.BlockSpec((1, cpt, hv, NA_CHUNK), lambda b, i: (b, i, 0, 0)),
        ],
        out_shape=[
            jax.ShapeDtypeStruct((B, hq, S), BF16),
            jax.ShapeDtypeStruct((B, S, hk), BF16),
            jax.ShapeDtypeStruct((B, S // NA_CHUNK, hv, NA_CHUNK), BF16),
        ],
        compiler_params=_params(2),
        name="na_pre",
    )(x, mod, gains, w["wqT"], w["wk"], w["wvT"], w["ones"])


def _na_band_start(blk, n_blk):
    return jnp.clip(blk - 1, 0, n_blk - NA_BAND // NA_QROWS)


def _na_attn_kernel(qT_ref, k_ref, vT_ref, bias_ref, o_ref, *, n_blk, group):
    step = pl.program_id(2)
    tq = NA_QROWS * GRID_W
    n_keys = NA_BAND * GRID_W
    n_chunks = n_keys // NA_CHUNK
    band0 = [_na_band_start(step * group + g, n_blk) for g in range(group)]
    pattern = [jnp.where(step * group + g == 0, 0, jnp.where(step * group + g == n_blk - 1, 2, 1))
               for g in range(group)]

    def scores(g, c, hh):
        kc = k_ref[0, pl.ds(pl.multiple_of((band0[g] + c) * NA_CHUNK, NA_CHUNK), NA_CHUNK), :]
        qT = qT_ref[0, hh * HEAD_PAD:(hh + 1) * HEAD_PAD, g * tq:(g + 1) * tq]
        return _dot(kc, qT) + bias_ref[hh, pattern[g], c * NA_CHUNK:(c + 1) * NA_CHUNK, :]

    steps = [(g, c) for g in range(group) for c in range(n_chunks)]
    pending = {}
    for g, c in steps[:NA_LOOKAHEAD]:
        for hh in range(2):
            pending[g, c, hh] = scores(g, c, hh)
    res = {}
    for i, (g, c) in enumerate(steps):
        for hh in range(2):
            if i + NA_LOOKAHEAD < len(steps):
                ng, nc = steps[i + NA_LOOKAHEAD]
                pending[ng, nc, hh] = scores(ng, nc, hh)
            s = pending.pop((g, c, hh))
            if c == 0:
                m, acc = jnp.full((1, tq), NEG_INF, F32), jnp.zeros((V_ROWS, tq), F32)
            else:
                m, acc = res[hh]
            m_new = jnp.maximum(m, jnp.max(s, axis=0, keepdims=True))
            alpha = jnp.exp2(m - m_new)
            p = jnp.exp2((s - m_new).astype(BF16))
            vT = vT_ref[0, band0[g] + c, hh * V_ROWS:(hh + 1) * V_ROWS, :]
            res[hh] = (m_new, alpha * acc + _dot(vT, p))
        if c == n_chunks - 1:
            pair = jnp.concatenate([res[hh][1][0:NA_HEAD_DIM] / res[hh][1][NA_HEAD_DIM:NA_HEAD_DIM + 1]
                                    for hh in range(2)], axis=0)
            o_ref[0, g * tq:(g + 1) * tq, :] = pair.T.astype(BF16)


def _na_attn(qT, k, vT, bias):
    B, _, S = qT.shape
    tq = NA_QROWS * GRID_W
    n_blk = S // tq
    hp = NA_HEADS // 2
    group = max(g for g in range(1, NA_GROUP + 1) if n_blk % g == 0)
    return pl.pallas_call(
        functools.partial(_na_attn_kernel, n_blk=n_blk, group=group),
        grid=(B, hp, n_blk // group),
        in_specs=[
            pl.BlockSpec((1, 2 * HEAD_PAD, group * tq), lambda b, h, i: (b, h, i)),
            pl.BlockSpec((1, S, 2 * NA_HEAD_DIM), lambda b, h, i: (b, 0, h)),
            pl.BlockSpec((1, S // NA_CHUNK, 2 * V_ROWS, NA_CHUNK), lambda b, h, i: (b, 0, h, 0)),
            pl.BlockSpec((2, 3, NA_BAND * GRID_W, tq), lambda b, h, i: (h, 0, 0, 0)),
        ],
        out_specs=pl.BlockSpec((1, group * tq, 2 * NA_HEAD_DIM), lambda b, h, i: (b, i, h)),
        out_shape=jax.ShapeDtypeStruct((B, S, NA_HEADS * NA_HEAD_DIM), BF16),
        compiler_params=_params(3),
        name="na_attn",
    )(qT, k, vT, bias)


def _na_bias_tables(rpb):
    p = np.arange(3)[:, None]
    qi = np.arange(NA_QROWS)[None, :]
    qr = NA_QROWS * p + qi
    r_start = np.clip(qr - WIN_R // 2, 0, NA_BAND - WIN_R)
    kr = np.arange(NA_BAND)[None, :, None]
    valid_r = (kr >= r_start[:, None, :]) & (kr < r_start[:, None, :] + WIN_R)
    dr = np.clip(kr - qr[:, None, :] + WIN_R - 1, 0, 2 * WIN_R - 2)
    c = np.arange(GRID_W)
    c_start = np.clip(c - WIN_C // 2, 0, GRID_W - WIN_C)
    kc = c[:, None]
    valid_c = (kc >= c_start[None, :]) & (kc < c_start[None, :] + WIN_C)
    dc = np.clip(kc - c[None, :] + WIN_C - 1, 0, 2 * WIN_C - 2)
    valid = valid_r[:, :, None, :, None] & valid_c[None, None, :, None, :]
    row_sel = (dr.reshape(-1)[:, None] == np.arange(2 * WIN_R - 1)[None, :]).astype(np.float32)
    col_sel = (dc[None] == np.arange(2 * WIN_C - 1)[:, None, None]).astype(np.float32)
    rows = jnp.sum((rpb.astype(F32) * LOG2E)[:, None, :, :] * row_sel[None, :, :, None], axis=2)
    rows = rows.reshape(NA_HEADS * 3 * NA_BAND, NA_QROWS * (2 * WIN_C - 1))
    col_sel4 = (np.eye(NA_QROWS, dtype=np.float32)[:, None, None, :, None] * col_sel[None, :, :, None, :]
                ).reshape(NA_QROWS * (2 * WIN_C - 1), GRID_W, NA_QROWS * GRID_W)
    bias = jnp.einsum("nj,jkq->nkq", rows, col_sel4, precision=lax.Precision.HIGHEST)
    bias = bias.reshape(NA_HEADS, 3, NA_BAND * GRID_W, NA_QROWS * GRID_W)
    valid = valid.reshape(3, NA_BAND * GRID_W, NA_QROWS * GRID_W)
    return jnp.where(valid[None], bias, NEG_INF)


def _post_ffn_kernel(x_ref, a_ref, mod_ref, gains_ref, wo_ref, wgu_ref, wd_ref, o_ref):
    gate_m = mod_ref[0, 2:3, :]
    shift_f = mod_ref[0, 3:4, :]
    scale_f = mod_ref[0, 4:5, :]
    gate_f = mod_ref[0, 5:6, :]
    n_sub = x_ref.shape[1] // FFN_SUBTILE
    rows = [slice(r * FFN_SUBTILE, (r + 1) * FFN_SUBTILE) for r in range(n_sub)]
    mix = [_dot(a_ref[0, r, :], wo_ref[...]) for r in rows]
    xs, gu, down = [], [], []
    for i in range(n_sub + 2):
        if i < n_sub:
            x = x_ref[0, rows[i], :] + gate_m * _rms(mix[i], gains_ref[1:2, :])
            h = (_rms(x, gains_ref[2:3, :]) * (1.0 + scale_f) + shift_f).astype(BF16)
            xs.append(x)
            gu.append((_dot(h, wgu_ref[:, 0:FFN_HIDDEN]), _dot(h, wgu_ref[:, FFN_HIDDEN:2 * FFN_HIDDEN])))
        if 1 <= i <= n_sub:
            g, u = gu[i - 1]
            act = ((g / (1.0 + jnp.exp(-g))) * u).astype(BF16)
            down.append(_dot(act, wd_ref[...]))
        if i >= 2:
            r = i - 2
            o_ref[0, rows[r], :] = xs[r] + gate_f * _rms(down[r], gains_ref[3:4, :])


def _post_ffn(x, attn, mod, gains, wo, wgu, wd):
    B, S, _ = x.shape
    tm = TOKEN_TILE
    return pl.pallas_call(
        _post_ffn_kernel,
        grid=(B, S // tm),
        in_specs=[
            pl.BlockSpec((1, tm, D_MODEL), lambda b, i: (b, i, 0)),
            pl.BlockSpec((1, tm, D_MODEL), lambda b, i: (b, i, 0)),
            pl.BlockSpec((1, N_MOD, D_MODEL), lambda b, i: (b, 0, 0)),
            _const_spec((4, D_MODEL)),
            _const_spec((D_MODEL, D_MODEL)),
            _const_spec((D_MODEL, 2 * FFN_HIDDEN)),
            _const_spec((FFN_HIDDEN, D_MODEL)),
        ],
        out_specs=pl.BlockSpec((1, tm, D_MODEL), lambda b, i: (b, i, 0)),
        out_shape=jax.ShapeDtypeStruct((B, S, D_MODEL), F32),
        compiler_params=_params(2),
        name="post_ffn",
    )(x, attn, mod, gains, wo, wgu, wd)


def _ones_column(n_heads):
    col = np.zeros((n_heads, V_ROWS, 1), np.float32)
    col[:, V_DIM, 0] = 1.0
    return jnp.asarray(col.reshape(n_heads * V_ROWS, 1))


def _prep_mla(w_dkv, q_norm, kv_norm, w_uq, w_ukv):
    half = QK_ROPE // 2
    r0 = Q_LORA + KV_LORA
    wdkv = jnp.zeros((D_MODEL, LAT_COLS), F32)
    wdkv = wdkv.at[:, :r0 + QK_ROPE].set(w_dkv)
    wdkv = wdkv.at[:, 512:512 + half].set(-w_dkv[:, r0 + half:r0 + QK_ROPE])
    wdkv = wdkv.at[:, 512 + half:512 + QK_ROPE].set(w_dkv[:, r0:r0 + half])
    wq = w_uq.reshape(Q_LORA, MLA_HEADS, QK_NOPE + QK_ROPE)
    wq = jnp.pad(wq, ((0, 0), (0, 0), (0, HEAD_PAD - QK_NOPE - QK_ROPE)))
    wqT = wq.reshape(Q_LORA, MLA_HEADS * HEAD_PAD).T
    wkv = w_ukv.reshape(KV_LORA, MLA_HEADS, QK_NOPE + V_DIM)
    wk_nope = jnp.pad(wkv[:, :, :QK_NOPE], ((0, 0), (0, 0), (0, HEAD_PAD - QK_NOPE)))
    eye = np.zeros((KV_LORA, MLA_HEADS, HEAD_PAD), np.float32)
    for r in range(QK_ROPE):
        eye[r, :, QK_NOPE + r] = 1.0
    wk = jnp.concatenate([wk_nope, jnp.asarray(eye)], axis=0).reshape(2 * KV_LORA, MLA_HEADS * HEAD_PAD)
    wv = jnp.pad(wkv[:, :, QK_NOPE:], ((0, 0), (0, 0), (0, V_ROWS - V_DIM)))
    wvT = wv.reshape(KV_LORA, MLA_HEADS * V_ROWS).T
    return dict(wdkv=wdkv.astype(BF16), qn=q_norm.reshape(1, Q_LORA), kvn=kv_norm.reshape(1, KV_LORA),
                wqT=wqT.astype(BF16), wk=wk.astype(BF16), wvT=wvT.astype(BF16), ones=_ones_column(MLA_HEADS))


def _prep_na(w_qkv):
    hk = NA_HEADS * NA_HEAD_DIM
    wv = w_qkv[:, 2 * hk:].reshape(D_MODEL, NA_HEADS, NA_HEAD_DIM)
    wv = jnp.pad(wv, ((0, 0), (0, 0), (0, V_ROWS - NA_HEAD_DIM)))
    return dict(wqT=w_qkv[:, :hk].T.astype(BF16), wk=w_qkv[:, hk:2 * hk].astype(BF16),
                wvT=wv.reshape(D_MODEL, NA_HEADS * V_ROWS).T.astype(BF16), ones=_ones_column(NA_HEADS))


def _rope_tables(seq_len):
    half = QK_ROPE // 2
    inv_freq = 1.0 / (ROPE_THETA ** (jnp.arange(0, QK_ROPE, 2, dtype=F32) / QK_ROPE))
    ang = jnp.arange(seq_len, dtype=F32)[:, None] * inv_freq[None, :]
    cos, sin = jnp.cos(ang), jnp.sin(ang)
    pad = jnp.zeros((seq_len, 128 - 2 * half), F32)
    ck = jnp.concatenate([cos, cos, pad], axis=1)
    sk = jnp.concatenate([sin, sin, pad], axis=1)
    return ck, sk, cos.T, sin.T


def _trunk(x, mods, gains, mla_w, na_w, na_bias, post_w):
    tables = _rope_tables(x.shape[1])
    for i in range(DEPTH):
        if i % 2 == 0:
            qT, k, vT = _mla_pre(x, mods[i], gains[i], mla_w[i // 2], tables)
            attn = _mla_attn(qT, k, vT)
        else:
            qT, k, vT = _na_pre(x, mods[i], gains[i], na_w[i // 2])
            attn = _na_attn(qT, k, vT, na_bias[i // 2])
        x = _post_ffn(x, attn, mods[i], gains[i], *post_w[i])
    return x


def kernel(x_prompt, x_sample, c_prompt, c_sample, ada_w, ada_b, norm_pre_mix, norm_post_mix, norm_pre_ffn, norm_post_ffn, mla_w_dkv, mla_q_norm, mla_kv_norm, mla_w_uq, mla_w_ukv, mla_w_o, na_w_qkv, na_rpb, na_w_o, ffn_w_gu, ffn_w_down):
    bp = x_prompt.shape[0]
    bs = x_sample.shape[0]
    mod = _ada_mod(jnp.concatenate([c_prompt, c_sample], axis=0), ada_w, ada_b)
    mod = mod.reshape(DEPTH, bp + bs, N_MOD, D_MODEL)
    gains = [jnp.stack([norm_pre_mix[i], norm_post_mix[i], norm_pre_ffn[i], norm_post_ffn[i]]) for i in range(DEPTH)]
    mla_w = [_prep_mla(mla_w_dkv[j], mla_q_norm[j], mla_kv_norm[j], mla_w_uq[j], mla_w_ukv[j])
             for j in range(mla_w_dkv.shape[0])]
    na_w = [_prep_na(na_w_qkv[j]) for j in range(na_w_qkv.shape[0])]
    na_bias = [_na_bias_tables(na_rpb[j]) for j in range(na_rpb.shape[0])]
    post_w = []
    for i in range(DEPTH):
        wo = mla_w_o[i // 2] if i % 2 == 0 else na_w_o[i // 2]
        post_w.append((wo.astype(BF16), ffn_w_gu[i].astype(BF16), ffn_w_down[i].astype(BF16)))
    y_prompt = _trunk(x_prompt, [mod[i, :bp] for i in range(DEPTH)], gains, mla_w, na_w, na_bias, post_w)
    y_sample = _trunk(x_sample, [mod[i, bp:] for i in range(DEPTH)], gains, mla_w, na_w, na_bias, post_w)
    return (y_prompt, y_sample)
```

```python
import functools

import jax
import jax.numpy as jnp
import numpy as np
from jax import lax
from jax.experimental import pallas as pl
from jax.experimental.pallas import tpu as pltpu

F32 = jnp.float32
BF16 = jnp.bfloat16

D_MODEL = 1024
DEPTH = 2
N_MOD = 6
RMS_EPS = 1e-6
NEG_INF = -1e30

MLA_HEADS = 16
Q_LORA = 256
KV_LORA = 128
QK_NOPE = 64
QK_ROPE = 32
V_DIM = 64
ROPE_THETA = 10000.0
MLA_SCALE = (QK_NOPE + QK_ROPE) ** -0.5
LOG2E = 1.4426950408889634
HEAD_PAD = 128
V_ROWS = 80
LAT_COLS = 640

NA_HEADS = 16
NA_HEAD_DIM = 64
GRID_W = 64
WIN_R = 8
WIN_C = 16
NA_QROWS = 4
NA_BAND = 12
NA_SCALE = NA_HEAD_DIM ** -0.5
NA_GROUP = 16
NA_LOOKAHEAD = 3

FFN_HIDDEN = 2816
FFN_SUBTILE = 256

TOKEN_TILE = 512
PRE_TILE = 1024
MLA_TQ = 256
MLA_QBLOCKS = 16
MLA_TK = 256
MLA_LOOKAHEAD = 3
NA_CHUNK = 256

VMEM_LIMIT = 56 * 1024 * 1024

_NT = (((1,), (1,)), ((), ()))


def _dot(a, b):
    return jnp.dot(a, b, preferred_element_type=F32)


def _dot_nt(a, b):
    return lax.dot_general(a, b, _NT, preferred_element_type=F32)


def _rms(x, g):
    ms = jnp.mean(x * x, axis=-1, keepdims=True)
    return x * lax.rsqrt(ms + RMS_EPS) * g


def _const_spec(shape):
    zeros = (0,) * len(shape)
    return pl.BlockSpec(shape, lambda *_: zeros, pipeline_mode=pl.Buffered(1))


def _params(n_axes):
    return pltpu.CompilerParams(
        dimension_semantics=("arbitrary",) * n_axes, vmem_limit_bytes=VMEM_LIMIT)


def _ada_kernel(c_ref, w_ref, b_ref, o_ref):
    c = c_ref[...]
    c_act = c / (1.0 + jnp.exp(-c))
    o_ref[0] = _dot(c_act.astype(BF16), w_ref[0]) + b_ref[0]


def _ada_mod(c_all, ada_w, ada_b):
    n_rows = c_all.shape[0]
    n_out = N_MOD * D_MODEL
    tn = 1536
    return pl.pallas_call(
        _ada_kernel,
        grid=(DEPTH, n_out // tn),
        in_specs=[
            pl.BlockSpec((n_rows, D_MODEL), lambda i, j: (0, 0)),
            pl.BlockSpec((1, D_MODEL, tn), lambda i, j: (i, 0, j)),
            pl.BlockSpec((1, 1, tn), lambda i, j: (i, 0, j)),
        ],
        out_specs=pl.BlockSpec((1, n_rows, tn), lambda i, j: (i, 0, j)),
        out_shape=jax.ShapeDtypeStruct((DEPTH, n_rows, n_out), F32),
        compiler_params=_params(2),
        name="ada_mod",
    )(c_all, ada_w.astype(BF16), ada_b.reshape(DEPTH, 1, n_out))


def _mla_pre_kernel(x_ref, mod_ref, gains_ref, wdkv_ref, qn_ref, kvn_ref, wqT_ref, wk_ref,
                    wvT_ref, ones_ref, ck_ref, sk_ref, cT_ref, sT_ref, qT_out, k_out, vT_out):
    shift = mod_ref[0, 0:1, :]
    scale = mod_ref[0, 1:2, :]
    half = QK_ROPE // 2
    subs = [slice(c * MLA_TK, (c + 1) * MLA_TK) for c in range(x_ref.shape[1] // MLA_TK)]
    lats = []
    for r in subs:
        h = (_rms(x_ref[0, r, :], gains_ref[0:1, :]) * (1.0 + scale) + shift).astype(BF16)
        lats.append(_dot(h, wdkv_ref[...]))
    for c, (r, lat) in enumerate(zip(subs, lats)):
        cq = _rms(lat[:, 0:Q_LORA], qn_ref[...]).astype(BF16)
        ckv = _rms(lat[:, Q_LORA:Q_LORA + KV_LORA], kvn_ref[...]).astype(BF16)
        kr = lat[:, 384:512] * ck_ref[r, :] + lat[:, 512:640] * sk_ref[r, :]
        kin = jnp.concatenate([ckv, kr.astype(BF16)], axis=1)
        k_out[0, r, :] = _dot(kin, wk_ref[...]).astype(BF16)

        qT = _dot_nt(wqT_ref[...], cq) * (MLA_SCALE * LOG2E)
        cT = cT_ref[:, r]
        sT = sT_ref[:, r]
        zpad = jnp.zeros((HEAD_PAD - QK_NOPE - QK_ROPE, qT.shape[1]), BF16)
        for hd in range(MLA_HEADS):
            b0 = hd * HEAD_PAD
            x1 = qT[b0 + QK_NOPE:b0 + QK_NOPE + half]
            x2 = qT[b0 + QK_NOPE + half:b0 + QK_NOPE + QK_ROPE]
            qT_out[0, b0:b0 + QK_NOPE, r] = qT[b0:b0 + QK_NOPE].astype(BF16)
            qT_out[0, b0 + QK_NOPE:b0 + QK_NOPE + half, r] = (x1 * cT - x2 * sT).astype(BF16)
            qT_out[0, b0 + QK_NOPE + half:b0 + QK_NOPE + QK_ROPE, r] = (x2 * cT + x1 * sT).astype(BF16)
            qT_out[0, b0 + QK_NOPE + QK_ROPE:b0 + HEAD_PAD, r] = zpad

        vT = _dot_nt(wvT_ref[...], ckv) + ones_ref[...]
        vT_out[0, c] = vT.astype(BF16)


def _mla_pre(x, mod, gains, w, tables):
    B, S, _ = x.shape
    tm = PRE_TILE if S % PRE_TILE == 0 else TOKEN_TILE
    n_t = S // tm
    cpt = tm // MLA_TK
    ck, sk, cT, sT = tables
    hq = MLA_HEADS * HEAD_PAD
    hv = MLA_HEADS * V_ROWS
    return pl.pallas_call(
        _mla_pre_kernel,
        grid=(B, n_t),
        in_specs=[
            pl.BlockSpec((1, tm, D_MODEL), lambda b, i: (b, i, 0)),
            pl.BlockSpec((1, N_MOD, D_MODEL), lambda b, i: (b, 0, 0)),
            _const_spec((4, D_MODEL)),
            _const_spec((D_MODEL, LAT_COLS)),
            _const_spec((1, Q_LORA)),
            _const_spec((1, KV_LORA)),
            _const_spec((hq, Q_LORA)),
            _const_spec((2 * KV_LORA, hq)),
            _const_spec((hv, KV_LORA)),
            _const_spec((hv, 1)),
            pl.BlockSpec((tm, 128), lambda b, i: (i, 0)),
            pl.BlockSpec((tm, 128), lambda b, i: (i, 0)),
            pl.BlockSpec((QK_ROPE // 2, tm), lambda b, i: (0, i)),
            pl.BlockSpec((QK_ROPE // 2, tm), lambda b, i: (0, i)),
        ],
        out_specs=[
            pl.BlockSpec((1, hq, tm), lambda b, i: (b, 0, i)),
            pl.BlockSpec((1, tm, hq), lambda b, i: (b, i, 0)),
            pl.BlockSpec((1, cpt, hv, MLA_TK), lambda b, i: (b, i, 0, 0)),
        ],
        out_shape=[
            jax.ShapeDtypeStruct((B, hq, S), BF16),
            jax.ShapeDtypeStruct((B, S, hq), BF16),
            jax.ShapeDtypeStruct((B, S // MLA_TK, hv, MLA_TK), BF16),
        ],
        compiler_params=_params(2),
        name="mla_pre",
    )(x, mod, gains, w["wdkv"], w["qn"], w["kvn"], w["wqT"], w["wk"], w["wvT"], w["ones"],
      ck, sk, cT, sT)


def _mla_attn_kernel(qT_ref, k_ref, vT_ref, o_ref, *, n_chunks):
    tq = MLA_TQ
    n_qb = qT_ref.shape[2] // tq

    def scores(qb, j, hh):
        k = k_ref[0, j * MLA_TK:(j + 1) * MLA_TK, hh * HEAD_PAD:(hh + 1) * HEAD_PAD]
        return _dot(k, qT_ref[0, hh * HEAD_PAD:(hh + 1) * HEAD_PAD, qb * tq:(qb + 1) * tq])

    steps = [(qb, j) for qb in range(n_qb) for j in range(n_chunks)]
    pending = {}
    for qb, j in steps[:MLA_LOOKAHEAD]:
        for hh in range(2):
            pending[qb, j, hh] = scores(qb, j, hh)
    res = {}
    for i, (qb, j) in enumerate(steps):
        for hh in range(2):
            if i + MLA_LOOKAHEAD < len(steps):
                nqb, nj = steps[i + MLA_LOOKAHEAD]
                pending[nqb, nj, hh] = scores(nqb, nj, hh)
            s = pending.pop((qb, j, hh))
            if j == 0:
                m, acc = jnp.full((1, tq), NEG_INF, F32), jnp.zeros((V_ROWS, tq), F32)
            else:
                m, acc = res[hh]
            m_new = jnp.maximum(m, jnp.max(s, axis=0, keepdims=True))
            alpha = jnp.exp2(m - m_new)
            p = jnp.exp2(s - m_new).astype(BF16)
            vT = vT_ref[0, j, hh * V_ROWS:(hh + 1) * V_ROWS, :]
            res[hh] = (m_new, alpha * acc + _dot(vT, p))
        if j == n_chunks - 1:
            oT = jnp.concatenate([res[hh][1][0:V_DIM] / res[hh][1][V_DIM:V_DIM + 1] for hh in range(2)], axis=0)
            o_ref[0, qb * tq:(qb + 1) * tq, :] = oT.T.astype(BF16)


def _mla_attn(qT, k, vT):
    B, _, S = qT.shape
    n_chunks = S // MLA_TK
    hp = MLA_HEADS // 2
    n_qb = max(g for g in range(1, MLA_QBLOCKS + 1) if (S // MLA_TQ) % g == 0)
    tq_step = MLA_TQ * n_qb
    return pl.pallas_call(
        functools.partial(_mla_attn_kernel, n_chunks=n_chunks),
        grid=(B, hp, S // tq_step),
        in_specs=[
            pl.BlockSpec((1, 2 * HEAD_PAD, tq_step), lambda b, h, i: (b, h, i)),
            pl.BlockSpec((1, S, 2 * HEAD_PAD), lambda b, h, i: (b, 0, h)),
            pl.BlockSpec((1, n_chunks, 2 * V_ROWS, MLA_TK), lambda b, h, i: (b, 0, h, 0)),
        ],
        out_specs=pl.BlockSpec((1, tq_step, 2 * V_DIM), lambda b, h, i: (b, i, h)),
        out_shape=jax.ShapeDtypeStruct((B, S, MLA_HEADS * V_DIM), BF16),
        compiler_params=_params(3),
        name="mla_attn",
    )(qT, k, vT)


def _na_pre_kernel(x_ref, mod_ref, gains_ref, wqT_ref, wk_ref, wvT_ref, ones_ref,
                   qT_out, k_out, vT_out):
    shift = mod_ref[0, 0:1, :]
    scale = mod_ref[0, 1:2, :]
    for c in range(x_ref.shape[1] // NA_CHUNK):
        r = slice(c * NA_CHUNK, (c + 1) * NA_CHUNK)
        h = (_rms(x_ref[0, r, :], gains_ref[0:1, :]) * (1.0 + scale) + shift).astype(BF16)
        k_out[0, r, :] = _dot(h, wk_ref[...]).astype(BF16)
        qT = _dot_nt(wqT_ref[...], h) * (NA_SCALE * LOG2E)
        zpad = jnp.zeros((NA_HEAD_DIM, qT.shape[1]), BF16)
        for hd in range(NA_HEADS):
            lo = hd * HEAD_PAD + (hd % 2) * NA_HEAD_DIM
            zo = hd * HEAD_PAD + (1 - hd % 2) * NA_HEAD_DIM
            qT_out[0, lo:lo + NA_HEAD_DIM, r] = qT[hd * NA_HEAD_DIM:(hd + 1) * NA_HEAD_DIM].astype(BF16)
            qT_out[0, zo:zo + NA_HEAD_DIM, r] = zpad
        vT = _dot_nt(wvT_ref[...], h) + ones_ref[...]
        vT_out[0, c] = vT.astype(BF16)


def _na_pre(x, mod, gains, w):
    B, S, _ = x.shape
    tm = PRE_TILE if S % PRE_TILE == 0 else TOKEN_TILE
    cpt = tm // NA_CHUNK
    hq = NA_HEADS * HEAD_PAD
    hk = NA_HEADS * NA_HEAD_DIM
    hv = NA_HEADS * V_ROWS
    return pl.pallas_call(
        _na_pre_kernel,
        grid=(B, S // tm),
        in_specs=[
            pl.BlockSpec((1, tm, D_MODEL), lambda b, i: (b, i, 0)),
            pl.BlockSpec((1, N_MOD, D_MODEL), lambda b, i: (b, 0, 0)),
            _const_spec((4, D_MODEL)),
            _const_spec((hk, D_MODEL)),
            _const_spec((D_MODEL, hk)),
            _const_spec((hv, D_MODEL)),
            _const_spec((hv, 1)),
        ],
        out_specs=[
            pl.BlockSpec((1, hq, tm), lambda b, i: (b, 0, i)),
            pl.BlockSpec((1, tm, hk), lambda b, i: (b, i, 0)),
            pl.BlockSpec((1, cpt, hv, NA_CHUNK), lambda b, i: (b, i, 0, 0)),
        ],
        out_shape=[
            jax.ShapeDtypeStruct((B, hq, S), BF16),
            jax.ShapeDtypeStruct((B, S, hk), BF16),
            jax.ShapeDtypeStruct((B, S // NA_CHUNK, hv, NA_CHUNK), BF16),
        ],
        compiler_params=_params(2),
        name="na_pre",
    )(x, mod, gains, w["wqT"], w["wk"], w["wvT"], w["ones"])


def _na_band_start(blk, n_blk):
    return jnp.clip(blk - 1, 0, n_blk - NA_BAND // NA_QROWS)


def _na_attn_kernel(qT_ref, k_ref, vT_ref, bias_ref, o_ref, *, n_blk, group):
    step = pl.program_id(2)
    tq = NA_QROWS * GRID_W
    n_keys = NA_BAND * GRID_W
    n_chunks = n_keys // NA_CHUNK
    band0 = [_na_band_start(step * group + g, n_blk) for g in range(group)]
    pattern = [jnp.where(step * group + g == 0, 0, jnp.where(step * group + g == n_blk - 1, 2, 1))
               for g in range(group)]

    def scores(g, c, hh):
        kc = k_ref[0, pl.ds(pl.multiple_of((band0[g] + c) * NA_CHUNK, NA_CHUNK), NA_CHUNK), :]
        qT = qT_ref[0, hh * HEAD_PAD:(hh + 1) * HEAD_PAD, g * tq:(g + 1) * tq]
        return _dot(kc, qT) + bias_ref[hh, pattern[g], c * NA_CHUNK:(c + 1) * NA_CHUNK, :]

    steps = [(g, c) for g in range(group) for c in range(n_chunks)]
    pending = {}
    for g, c in steps[:NA_LOOKAHEAD]:
        for hh in range(2):
            pending[g, c, hh] = scores(g, c, hh)
    res = {}
    for i, (g, c) in enumerate(steps):
        for hh in range(2):
            if i + NA_LOOKAHEAD < len(steps):
                ng, nc = steps[i + NA_LOOKAHEAD]
                pending[ng, nc, hh] = scores(ng, nc, hh)
            s = pending.pop((g, c, hh))
            if c == 0:
                m, acc = jnp.full((1, tq), NEG_INF, F32), jnp.zeros((V_ROWS, tq), F32)
            else:
                m, acc = res[hh]
            m_new = jnp.maximum(m, jnp.max(s, axis=0, keepdims=True))
            alpha = jnp.exp2(m - m_new)
            p = jnp.exp2((s - m_new).astype(BF16))
            vT = vT_ref[0, band0[g] + c, hh * V_ROWS:(hh + 1) * V_ROWS, :]
            res[hh] = (m_new, alpha * acc + _dot(vT, p))
        if c == n_chunks - 1:
            pair = jnp.concatenate([res[hh][1][0:NA_HEAD_DIM] / res[hh][1][NA_HEAD_DIM:NA_HEAD_DIM + 1]
                                    for hh in range(2)], axis=0)
            o_ref[0, g * tq:(g + 1) * tq, :] = pair.T.astype(BF16)


def _na_attn(qT, k, vT, bias):
    B, _, S = qT.shape
    tq = NA_QROWS * GRID_W
    n_blk = S // tq
    hp = NA_HEADS // 2
    group = max(g for g in range(1, NA_GROUP + 1) if n_blk % g == 0)
    return pl.pallas_call(
        functools.partial(_na_attn_kernel, n_blk=n_blk, group=group),
        grid=(B, hp, n_blk // group),
        in_specs=[
            pl.BlockSpec((1, 2 * HEAD_PAD, group * tq), lambda b, h, i: (b, h, i)),
            pl.BlockSpec((1, S, 2 * NA_HEAD_DIM), lambda b, h, i: (b, 0, h)),
            pl.BlockSpec((1, S // NA_CHUNK, 2 * V_ROWS, NA_CHUNK), lambda b, h, i: (b, 0, h, 0)),
            pl.BlockSpec((2, 3, NA_BAND * GRID_W, tq), lambda b, h, i: (h, 0, 0, 0)),
        ],
        out_specs=pl.BlockSpec((1, group * tq, 2 * NA_HEAD_DIM), lambda b, h, i: (b, i, h)),
        out_shape=jax.ShapeDtypeStruct((B, S, NA_HEADS * NA_HEAD_DIM), BF16),
        compiler_params=_params(3),
        name="na_attn",
    )(qT, k, vT, bias)


def _na_bias_tables(rpb):
    p = np.arange(3)[:, None]
    qi = np.arange(NA_QROWS)[None, :]
    qr = NA_QROWS * p + qi
    r_start = np.clip(qr - WIN_R // 2, 0, NA_BAND - WIN_R)
    kr = np.arange(NA_BAND)[None, :, None]
    valid_r = (kr >= r_start[:, None, :]) & (kr < r_start[:, None, :] + WIN_R)
    dr = np.clip(kr - qr[:, None, :] + WIN_R - 1, 0, 2 * WIN_R - 2)
    c = np.arange(GRID_W)
    c_start = np.clip(c - WIN_C // 2, 0, GRID_W - WIN_C)
    kc = c[:, None]
    valid_c = (kc >= c_start[None, :]) & (kc < c_start[None, :] + WIN_C)
    dc = np.clip(kc - c[None, :] + WIN_C - 1, 0, 2 * WIN_C - 2)
    valid = valid_r[:, :, None, :, None] & valid_c[None, None, :, None, :]
    row_sel = (dr.reshape(-1)[:, None] == np.arange(2 * WIN_R - 1)[None, :]).astype(np.float32)
    col_sel = (dc[None] == np.arange(2 * WIN_C - 1)[:, None, None]).astype(np.float32)
    rows = jnp.sum((rpb.astype(F32) * LOG2E)[:, None, :, :] * row_sel[None, :, :, None], axis=2)
    rows = rows.reshape(NA_HEADS * 3 * NA_BAND, NA_QROWS * (2 * WIN_C - 1))
    col_sel4 = (np.eye(NA_QROWS, dtype=np.float32)[:, None, None, :, None] * col_sel[None, :, :, None, :]
                ).reshape(NA_QROWS * (2 * WIN_C - 1), GRID_W, NA_QROWS * GRID_W)
    bias = jnp.einsum("nj,jkq->nkq", rows, col_sel4, precision=lax.Precision.HIGHEST)
    bias = bias.reshape(NA_HEADS, 3, NA_BAND * GRID_W, NA_QROWS * GRID_W)
    valid = valid.reshape(3, NA_BAND * GRID_W, NA_QROWS * GRID_W)
    return jnp.where(valid[None], bias, NEG_INF)


def _post_ffn_kernel(x_ref, a_ref, mod_ref, gains_ref, wo_ref, wgu_ref, wd_ref, o_ref):
    gate_m = mod_ref[0, 2:3, :]
    shift_f = mod_ref[0, 3:4, :]
    scale_f = mod_ref[0, 4:5, :]
    gate_f = mod_ref[0, 5:6, :]
    n_sub = x_ref.shape[1] // FFN_SUBTILE
    rows = [slice(r * FFN_SUBTILE, (r + 1) * FFN_SUBTILE) for r in range(n_sub)]
    mix = [_dot(a_ref[0, r, :], wo_ref[...]) for r in rows]
    xs, gu, down = [], [], []
    for i in range(n_sub + 2):
        if i < n_sub:
            x = x_ref[0, rows[i], :] + gate_m * _rms(mix[i], gains_ref[1:2, :])
            h = (_rms(x, gains_ref[2:3, :]) * (1.0 + scale_f) + shift_f).astype(BF16)
            xs.append(x)
            gu.append((_dot(h, wgu_ref[:, 0:FFN_HIDDEN]), _dot(h, wgu_ref[:, FFN_HIDDEN:2 * FFN_HIDDEN])))
        if 1 <= i <= n_sub:
            g, u = gu[i - 1]
            act = ((g / (1.0 + jnp.exp(-g))) * u).astype(BF16)
            down.append(_dot(act, wd_ref[...]))
        if i >= 2:
            r = i - 2
            o_ref[0, rows[r], :] = xs[r] + gate_f * _rms(down[r], gains_ref[3:4, :])


def _post_ffn(x, attn, mod, gains, wo, wgu, wd):
    B, S, _ = x.shape
    tm = TOKEN_TILE
    return pl.pallas_call(
        _post_ffn_kernel,
        grid=(B, S // tm),
        in_specs=[
            pl.BlockSpec((1, tm, D_MODEL), lambda b, i: (b, i, 0)),
            pl.BlockSpec((1, tm, D_MODEL), lambda b, i: (b, i, 0)),
            pl.BlockSpec((1, N_MOD, D_MODEL), lambda b, i: (b, 0, 0)),
            _const_spec((4, D_MODEL)),
            _const_spec((D_MODEL, D_MODEL)),
            _const_spec((D_MODEL, 2 * FFN_HIDDEN)),
            _const_spec((FFN_HIDDEN, D_MODEL)),
        ],
        out_specs=pl.BlockSpec((1, tm, D_MODEL), lambda b, i: (b, i, 0)),
        out_shape=jax.ShapeDtypeStruct((B, S, D_MODEL), F32),
        compiler_params=_params(2),
        name="post_ffn",
    )(x, attn, mod, gains, wo, wgu, wd)


def _ones_column(n_heads):
    col = np.zeros((n_heads, V_ROWS, 1), np.float32)
    col[:, V_DIM, 0] = 1.0
    return jnp.asarray(col.reshape(n_heads * V_ROWS, 1))


def _prep_mla(w_dkv, q_norm, kv_norm, w_uq, w_ukv):
    half = QK_ROPE // 2
    r0 = Q_LORA + KV_LORA
    wdkv = jnp.zeros((D_MODEL, LAT_COLS), F32)
    wdkv = wdkv.at[:, :r0 + QK_ROPE].set(w_dkv)
    wdkv = wdkv.at[:, 512:512 + half].set(-w_dkv[:, r0 + half:r0 + QK_ROPE])
    wdkv = wdkv.at[:, 512 + half:512 + QK_ROPE].set(w_dkv[:, r0:r0 + half])
    wq = w_uq.reshape(Q_LORA, MLA_HEADS, QK_NOPE + QK_ROPE)
    wq = jnp.pad(wq, ((0, 0), (0, 0), (0, HEAD_PAD - QK_NOPE - QK_ROPE)))
    wqT = wq.reshape(Q_LORA, MLA_HEADS * HEAD_PAD).T
    wkv = w_ukv.reshape(KV_LORA, MLA_HEADS, QK_NOPE + V_DIM)
    wk_nope = jnp.pad(wkv[:, :, :QK_NOPE], ((0, 0), (0, 0), (0, HEAD_PAD - QK_NOPE)))
    eye = np.zeros((KV_LORA, MLA_HEADS, HEAD_PAD), np.float32)
    for r in range(QK_ROPE):
        eye[r, :, QK_NOPE + r] = 1.0
    wk = jnp.concatenate([wk_nope, jnp.asarray(eye)], axis=0).reshape(2 * KV_LORA, MLA_HEADS * HEAD_PAD)
    wv = jnp.pad(wkv[:, :, QK_NOPE:], ((0, 0), (0, 0), (0, V_ROWS - V_DIM)))
    wvT = wv.reshape(KV_LORA, MLA_HEADS * V_ROWS).T
    return dict(wdkv=wdkv.astype(BF16), qn=q_norm.reshape(1, Q_LORA), kvn=kv_norm.reshape(1, KV_LORA),
                wqT=wqT.astype(BF16), wk=wk.astype(BF16), wvT=wvT.astype(BF16), ones=_ones_column(MLA_HEADS))


def _prep_na(w_qkv):
    hk = NA_HEADS * NA_HEAD_DIM
    wv = w_qkv[:, 2 * hk:].reshape(D_MODEL, NA_HEADS, NA_HEAD_DIM)
    wv = jnp.pad(wv, ((0, 0), (0, 0), (0, V_ROWS - NA_HEAD_DIM)))
    return dict(wqT=w_qkv[:, :hk].T.astype(BF16), wk=w_qkv[:, hk:2 * hk].astype(BF16),
                wvT=wv.reshape(D_MODEL, NA_HEADS * V_ROWS).T.astype(BF16), ones=_ones_column(NA_HEADS))


def _rope_tables(seq_len):
    half = QK_ROPE // 2
    inv_freq = 1.0 / (ROPE_THETA ** (jnp.arange(0, QK_ROPE, 2, dtype=F32) / QK_ROPE))
    ang = jnp.arange(seq_len, dtype=F32)[:, None] * inv_freq[None, :]
    cos, sin = jnp.cos(ang), jnp.sin(ang)
    pad = jnp.zeros((seq_len, 128 - 2 * half), F32)
    ck = jnp.concatenate([cos, cos, pad], axis=1)
    sk = jnp.concatenate([sin, sin, pad], axis=1)
    return ck, sk, cos.T, sin.T


def _trunk(x, mods, gains, mla_w, na_w, na_bias, post_w):
    tables = _rope_tables(x.shape[1])
    for i in range(DEPTH):
        if i % 2 == 0:
            qT, k, vT = _mla_pre(x, mods[i], gains[i], mla_w[i // 2], tables)
            attn = _mla_attn(qT, k, vT)
        else:
            qT, k, vT = _na_pre(x, mods[i], gains[i], na_w[i // 2])
            attn = _na_attn(qT, k, vT, na_bias[i // 2])
        x = _post_ffn(x, attn, mods[i], gains[i], *post_w[i])
    return x


def kernel(x_prompt, x_sample, c_prompt, c_sample, ada_w, ada_b, norm_pre_mix, norm_post_mix, norm_pre_ffn, norm_post_ffn, mla_w_dkv, mla_q_norm, mla_kv_norm, mla_w_uq, mla_w_ukv, mla_w_o, na_w_qkv, na_rpb, na_w_o, ffn_w_gu, ffn_w_down):
    bp = x_prompt.shape[0]
    bs = x_sample.shape[0]
    mod = _ada_mod(jnp.concatenate([c_prompt, c_sample], axis=0), ada_w, ada_b)
    mod = mod.reshape(DEPTH, bp + bs, N_MOD, D_MODEL)
    gains = [jnp.stack([norm_pre_mix[i], norm_post_mix[i], norm_pre_ffn[i], norm_post_ffn[i]]) for i in range(DEPTH)]
    mla_w = [_prep_mla(mla_w_dkv[j], mla_q_norm[j], mla_kv_norm[j], mla_w_uq[j], mla_w_ukv[j])
             for j in range(mla_w_dkv.shape[0])]
    na_w = [_prep_na(na_w_qkv[j]) for j in range(na_w_qkv.shape[0])]
    na_bias = [_na_bias_tables(na_rpb[j]) for j in range(na_rpb.shape[0])]
    post_w = []
    for i in range(DEPTH):
        wo = mla_w_o[i // 2] if i % 2 == 0 else na_w_o[i // 2]
        post_w.append((wo.astype(BF16), ffn_w_gu[i].astype(BF16), ffn_w_down[i].astype(BF16)))
    y_prompt = _trunk(x_prompt, [mod[i, :bp] for i in range(DEPTH)], gains, mla_w, na_w, na_bias, post_w)
    y_sample = _trunk(x_sample, [mod[i, bp:] for i in range(DEPTH)], gains, mla_w, na_w, na_bias, post_w)
    return (y_prompt, y_sample)
```

```python
import functools

import jax
import jax.numpy as jnp
import numpy as np
from jax import lax
from jax.experimental import pallas as pl
from jax.experimental.pallas import tpu as pltpu

F32 = jnp.float32
BF16 = jnp.bfloat16

D_MODEL = 1024
DEPTH = 2
N_MOD = 6
RMS_EPS = 1e-6
NEG_INF = -1e30

MLA_HEADS = 16
Q_LORA = 256
KV_LORA = 128
QK_NOPE = 64
QK_ROPE = 32
V_DIM = 64
ROPE_THETA = 10000.0
MLA_SCALE = (QK_NOPE + QK_ROPE) ** -0.5
LOG2E = 1.4426950408889634
HEAD_PAD = 128
V_ROWS = 80
LANES = 128
KR_COL = Q_LORA + KV_LORA
KR_SWAP_COL = KR_COL + LANES
LAT_COLS = KR_SWAP_COL + LANES

NA_HEADS = 16
NA_HEAD_DIM = 64
GRID_W = 64
WIN_R = 8
WIN_C = 16
NA_QROWS = 4
NA_BAND = 12
NA_SCALE = NA_HEAD_DIM ** -0.5
NA_GROUP = 16
NA_LOOKAHEAD = 3

FFN_HIDDEN = 2816
FFN_SUBTILE = 256

TOKEN_TILE = 512
PRE_TILE = 1024
MLA_TQ = 256
MLA_QBLOCKS = 8
MLA_TK = 256
MLA_LOOKAHEAD = 3
NA_CHUNK = 256

ADA_TILE_N = 1536

V7X_VMEM_BYTES = 64 * 1024 * 1024
VMEM_LIMIT = V7X_VMEM_BYTES * 7 // 8

_NT = (((1,), (1,)), ((), ()))


def _dot(a, b):
    return jnp.dot(a, b, preferred_element_type=F32)


def _dot_nt(a, b):
    return lax.dot_general(a, b, _NT, preferred_element_type=F32)


def _rms(x, g):
    ms = jnp.mean(x * x, axis=-1, keepdims=True)
    return x * lax.rsqrt(ms + RMS_EPS) * g


def _const_spec(shape):
    zeros = (0,) * len(shape)
    return pl.BlockSpec(shape, lambda *_: zeros, pipeline_mode=pl.Buffered(1))


def _params(n_axes):
    return pltpu.CompilerParams(
        dimension_semantics=("arbitrary",) * n_axes, vmem_limit_bytes=VMEM_LIMIT)


def _ada_kernel(c_ref, w_ref, b_ref, o_ref):
    c = c_ref[...]
    c_act = c / (1.0 + jnp.exp(-c))
    o_ref[0] = _dot(c_act.astype(BF16), w_ref[0]) + b_ref[0]


def _ada_mod(c_all, ada_w, ada_b):
    n_rows = c_all.shape[0]
    n_out = N_MOD * D_MODEL
    tn = ADA_TILE_N
    return pl.pallas_call(
        _ada_kernel,
        grid=(DEPTH, n_out // tn),
        in_specs=[
            pl.BlockSpec((n_rows, D_MODEL), lambda i, j: (0, 0)),
            pl.BlockSpec((1, D_MODEL, tn), lambda i, j: (i, 0, j)),
            pl.BlockSpec((1, 1, tn), lambda i, j: (i, 0, j)),
        ],
        out_specs=pl.BlockSpec((1, n_rows, tn), lambda i, j: (i, 0, j)),
        out_shape=jax.ShapeDtypeStruct((DEPTH, n_rows, n_out), F32),
        compiler_params=_params(2),
        name="ada_mod",
    )(c_all, ada_w.astype(BF16), ada_b.reshape(DEPTH, 1, n_out))


def _mla_pre_kernel(x_ref, mod_ref, gains_ref, wdkv_ref, qn_ref, kvn_ref, wqT_ref, wk_ref,
                    wvT_ref, ones_ref, ck_ref, sk_ref, cT_ref, sT_ref, qT_out, k_out, vT_out):
    shift = mod_ref[0, 0:1, :]
    scale = mod_ref[0, 1:2, :]
    half = QK_ROPE // 2
    subs = [slice(c * MLA_TK, (c + 1) * MLA_TK) for c in range(x_ref.shape[1] // MLA_TK)]
    lats = []
    for r in subs:
        h = (_rms(x_ref[0, r, :], gains_ref[0:1, :]) * (1.0 + scale) + shift).astype(BF16)
        lats.append(_dot(h, wdkv_ref[...]))
    for c, (r, lat) in enumerate(zip(subs, lats)):
        cq = _rms(lat[:, 0:Q_LORA], qn_ref[...]).astype(BF16)
        ckv = _rms(lat[:, Q_LORA:Q_LORA + KV_LORA], kvn_ref[...]).astype(BF16)
        kr = (lat[:, KR_COL:KR_COL + LANES] * ck_ref[r, :]
              + lat[:, KR_SWAP_COL:KR_SWAP_COL + LANES] * sk_ref[r, :])
        kin = jnp.concatenate([ckv, kr.astype(BF16)], axis=1)
        k_out[0, r, :] = _dot(kin, wk_ref[...]).astype(BF16)

        qT = _dot_nt(wqT_ref[...], cq) * (MLA_SCALE * LOG2E)
        cT = cT_ref[:, r]
        sT = sT_ref[:, r]
        zpad = jnp.zeros((HEAD_PAD - QK_NOPE - QK_ROPE, qT.shape[1]), BF16)
        for hd in range(MLA_HEADS):
            b0 = hd * HEAD_PAD
            x1 = qT[b0 + QK_NOPE:b0 + QK_NOPE + half]
            x2 = qT[b0 + QK_NOPE + half:b0 + QK_NOPE + QK_ROPE]
            qT_out[0, b0:b0 + QK_NOPE, r] = qT[b0:b0 + QK_NOPE].astype(BF16)
            qT_out[0, b0 + QK_NOPE:b0 + QK_NOPE + half, r] = (x1 * cT - x2 * sT).astype(BF16)
            qT_out[0, b0 + QK_NOPE + half:b0 + QK_NOPE + QK_ROPE, r] = (x2 * cT + x1 * sT).astype(BF16)
            qT_out[0, b0 + QK_NOPE + QK_ROPE:b0 + HEAD_PAD, r] = zpad

        vT = _dot_nt(wvT_ref[...], ckv) + ones_ref[...]
        vT_out[0, c] = vT.astype(BF16)


def _mla_pre(x, mod, gains, w, tables):
    B, S, _ = x.shape
    tm = PRE_TILE if S % PRE_TILE == 0 else TOKEN_TILE
    n_t = S // tm
    cpt = tm // MLA_TK
    ck, sk, cT, sT = tables
    hq = MLA_HEADS * HEAD_PAD
    hv = MLA_HEADS * V_ROWS
    return pl.pallas_call(
        _mla_pre_kernel,
        grid=(B, n_t),
        in_specs=[
            pl.BlockSpec((1, tm, D_MODEL), lambda b, i: (b, i, 0)),
            pl.BlockSpec((1, N_MOD, D_MODEL), lambda b, i: (b, 0, 0)),
            _const_spec((4, D_MODEL)),
            _const_spec((D_MODEL, LAT_COLS)),
            _const_spec((1, Q_LORA)),
            _const_spec((1, KV_LORA)),
            _const_spec((hq, Q_LORA)),
            _const_spec((2 * KV_LORA, hq)),
            _const_spec((hv, KV_LORA)),
            _const_spec((hv, 1)),
            pl.BlockSpec((tm, LANES), lambda b, i: (i, 0)),
            pl.BlockSpec((tm, LANES), lambda b, i: (i, 0)),
            pl.BlockSpec((QK_ROPE // 2, tm), lambda b, i: (0, i)),
            pl.BlockSpec((QK_ROPE // 2, tm), lambda b, i: (0, i)),
        ],
        out_specs=[
            pl.BlockSpec((1, hq, tm), lambda b, i: (b, 0, i)),
            pl.BlockSpec((1, tm, hq), lambda b, i: (b, i, 0)),
            pl.BlockSpec((1, cpt, hv, MLA_TK), lambda b, i: (b, i, 0, 0)),
        ],
        out_shape=[
            jax.ShapeDtypeStruct((B, hq, S), BF16),
            jax.ShapeDtypeStruct((B, S, hq), BF16),
            jax.ShapeDtypeStruct((B, S // MLA_TK, hv, MLA_TK), BF16),
        ],
        compiler_params=_params(2),
        name="mla_pre",
    )(x, mod, gains, w["wdkv"], w["qn"], w["kvn"], w["wqT"], w["wk"], w["wvT"], w["ones"],
      ck, sk, cT, sT)


def _mla_attn_kernel(qT_ref, k_ref, vT_ref, o_ref, *, n_chunks):
    tq = MLA_TQ
    n_qb = qT_ref.shape[2] // tq

    def scores(qb, j, hh):
        k = k_ref[0, j * MLA_TK:(j + 1) * MLA_TK, hh * HEAD_PAD:(hh + 1) * HEAD_PAD]
        return _dot(k, qT_ref[0, hh * HEAD_PAD:(hh + 1) * HEAD_PAD, qb * tq:(qb + 1) * tq])

    steps = [(qb, j) for qb in range(n_qb) for j in range(n_chunks)]
    pending = {}
    for qb, j in steps[:MLA_LOOKAHEAD]:
        for hh in range(2):
            pending[qb, j, hh] = scores(qb, j, hh)
    res = {}
    for i, (qb, j) in enumerate(steps):
        for hh in range(2):
            if i + MLA_LOOKAHEAD < len(steps):
                nqb, nj = steps[i + MLA_LOOKAHEAD]
                pending[nqb, nj, hh] = scores(nqb, nj, hh)
            s = pending.pop((qb, j, hh))
            if j == 0:
                m, acc = jnp.full((1, tq), NEG_INF, F32), jnp.zeros((V_ROWS, tq), F32)
            else:
                m, acc = res[hh]
            m_new = jnp.maximum(m, jnp.max(s, axis=0, keepdims=True))
            alpha = jnp.exp2(m - m_new)
            p = jnp.exp2(s - m_new).astype(BF16)
            vT = vT_ref[0, j, hh * V_ROWS:(hh + 1) * V_ROWS, :]
            res[hh] = (m_new, alpha * acc + _dot(vT, p))
        if j == n_chunks - 1:
            oT = jnp.concatenate([res[hh][1][0:V_DIM] / res[hh][1][V_DIM:V_DIM + 1] for hh in range(2)], axis=0)
            o_ref[0, qb * tq:(qb + 1) * tq, :] = oT.T.astype(BF16)


def _mla_attn(qT, k, vT):
    B, _, S = qT.shape
    n_chunks = S // MLA_TK
    hp = MLA_HEADS // 2
    n_qb = max(g for g in range(1, MLA_QBLOCKS + 1) if (S // MLA_TQ) % g == 0)
    tq_step = MLA_TQ * n_qb
    return pl.pallas_call(
        functools.partial(_mla_attn_kernel, n_chunks=n_chunks),
        grid=(B, hp, S // tq_step),
        in_specs=[
            pl.BlockSpec((1, 2 * HEAD_PAD, tq_step), lambda b, h, i: (b, h, i)),
            pl.BlockSpec((1, S, 2 * HEAD_PAD), lambda b, h, i: (b, 0, h)),
            pl.BlockSpec((1, n_chunks, 2 * V_ROWS, MLA_TK), lambda b, h, i: (b, 0, h, 0)),
        ],
        out_specs=pl.BlockSpec((1, tq_step, 2 * V_DIM), lambda b, h, i: (b, i, h)),
        out_shape=jax.ShapeDtypeStruct((B, S, MLA_HEADS * V_DIM), BF16),
        compiler_params=_params(3),
        name="mla_attn",
    )(qT, k, vT)


def _na_pre_kernel(x_ref, mod_ref, gains_ref, wqT_ref, wk_ref, wvT_ref, ones_ref,
                   qT_out, k_out, vT_out):
    shift = mod_ref[0, 0:1, :]
    scale = mod_ref[0, 1:2, :]
    for c in range(x_ref.shape[1] // NA_CHUNK):
        r = slice(c * NA_CHUNK, (c + 1) * NA_CHUNK)
        h = (_rms(x_ref[0, r, :], gains_ref[0:1, :]) * (1.0 + scale) + shift).astype(BF16)
        k_out[0, r, :] = _dot(h, wk_ref[...]).astype(BF16)
        qT = _dot_nt(wqT_ref[...], h) * (NA_SCALE * LOG2E)
        zpad = jnp.zeros((NA_HEAD_DIM, qT.shape[1]), BF16)
        for hd in range(NA_HEADS):
            lo = hd * HEAD_PAD + (hd % 2) * NA_HEAD_DIM
            zo = hd * HEAD_PAD + (1 - hd % 2) * NA_HEAD_DIM
            qT_out[0, lo:lo + NA_HEAD_DIM, r] = qT[hd * NA_HEAD_DIM:(hd + 1) * NA_HEAD_DIM].astype(BF16)
            qT_out[0, zo:zo + NA_HEAD_DIM, r] = zpad
        vT = _dot_nt(wvT_ref[...], h) + ones_ref[...]
        vT_out[0, c] = vT.astype(BF16)


def _na_pre(x, mod, gains, w):
    B, S, _ = x.shape
    tm = PRE_TILE if S % PRE_TILE == 0 else TOKEN_TILE
    cpt = tm // NA_CHUNK
    hq = NA_HEADS * HEAD_PAD
    hk = NA_HEADS * NA_HEAD_DIM
    hv = NA_HEADS * V_ROWS
    return pl.pallas_call(
        _na_pre_kernel,
        grid=(B, S // tm),
        in_specs=[
            pl.BlockSpec((1, tm, D_MODEL), lambda b, i: (b, i, 0)),
            pl.BlockSpec((1, N_MOD, D_MODEL), lambda b, i: (b, 0, 0)),
            _const_spec((4, D_MODEL)),
            _const_spec((hk, D_MODEL)),
            _const_spec((D_MODEL, hk)),
            _const_spec((hv, D_MODEL)),
            _const_spec((hv, 1)),
        ],
        out_specs=[
            pl.BlockSpec((1, hq, tm), lambda b, i: (b, 0, i)),
            pl.BlockSpec((1, tm, hk), lambda b, i: (b, i, 0)),
            pl.BlockSpec((1, cpt, hv, NA_CHUNK), lambda b, i: (b, i, 0, 0)),
        ],
        out_shape=[
            jax.ShapeDtypeStruct((B, hq, S), BF16),
            jax.ShapeDtypeStruct((B, S, hk), BF16),
            jax.ShapeDtypeStruct((B, S // NA_CHUNK, hv, NA_CHUNK), BF16),
        ],
        compiler_params=_params(2),
        name="na_pre",
    )(x, mod, gains, w["wqT"], w["wk"], w["wvT"], w["ones"])


def _na_band_start(blk, n_blk):
    return jnp.clip(blk - 1, 0, n_blk - NA_BAND // NA_QROWS)


def _na_attn_kernel(qT_ref, k_ref, vT_ref, bias_ref, o_ref, *, n_blk, group):
    step = pl.program_id(2)
    tq = NA_QROWS * GRID_W
    n_keys = NA_BAND * GRID_W
    n_chunks = n_keys // NA_CHUNK
    band0 = [_na_band_start(step * group + g, n_blk) for g in range(group)]
    pattern = [jnp.where(step * group + g == 0, 0, jnp.where(step * group + g == n_blk - 1, 2, 1))
               for g in range(group)]

    def scores(g, c, hh):
        kc = k_ref[0, pl.ds(pl.multiple_of((band0[g] + c) * NA_CHUNK, NA_CHUNK), NA_CHUNK), :]
        qT = qT_ref[0, hh * HEAD_PAD:(hh + 1) * HEAD_PAD, g * tq:(g + 1) * tq]
        return _dot(kc, qT) + bias_ref[hh, pattern[g], c * NA_CHUNK:(c + 1) * NA_CHUNK, :]

    steps = [(g, c) for g in range(group) for c in range(n_chunks)]
    pending = {}
    for g, c in steps[:NA_LOOKAHEAD]:
        for hh in range(2):
            pending[g, c, hh] = scores(g, c, hh)
    res = {}
    for i, (g, c) in enumerate(steps):
        for hh in range(2):
            if i + NA_LOOKAHEAD < len(steps):
                ng, nc = steps[i + NA_LOOKAHEAD]
                pending[ng, nc, hh] = scores(ng, nc, hh)
            s = pending.pop((g, c, hh))
            if c == 0:
                m, acc = jnp.full((1, tq), NEG_INF, F32), jnp.zeros((V_ROWS, tq), F32)
            else:
                m, acc = res[hh]
            m_new = jnp.maximum(m, jnp.max(s, axis=0, keepdims=True))
            alpha = jnp.exp2(m - m_new)
            p = jnp.exp2((s - m_new).astype(BF16))
            vT = vT_ref[0, band0[g] + c, hh * V_ROWS:(hh + 1) * V_ROWS, :]
            res[hh] = (m_new, alpha * acc + _dot(vT, p))
        if c == n_chunks - 1:
            pair = jnp.concatenate([res[hh][1][0:NA_HEAD_DIM] / res[hh][1][NA_HEAD_DIM:NA_HEAD_DIM + 1]
                                    for hh in range(2)], axis=0)
            o_ref[0, g * tq:(g + 1) * tq, :] = pair.T.astype(BF16)


def _na_attn(qT, k, vT, bias):
    B, _, S = qT.shape
    tq = NA_QROWS * GRID_W
    n_blk = S // tq
    hp = NA_HEADS // 2
    group = max(g for g in range(1, NA_GROUP + 1) if n_blk % g == 0)
    return pl.pallas_call(
        functools.partial(_na_attn_kernel, n_blk=n_blk, group=group),
        grid=(B, hp, n_blk // group),
        in_specs=[
            pl.BlockSpec((1, 2 * HEAD_PAD, group * tq), lambda b, h, i: (b, h, i)),
            pl.BlockSpec((1, S, 2 * NA_HEAD_DIM), lambda b, h, i: (b, 0, h)),
            pl.BlockSpec((1, S // NA_CHUNK, 2 * V_ROWS, NA_CHUNK), lambda b, h, i: (b, 0, h, 0)),
            pl.BlockSpec((2, 3, NA_BAND * GRID_W, tq), lambda b, h, i: (h, 0, 0, 0)),
        ],
        out_specs=pl.BlockSpec((1, group * tq, 2 * NA_HEAD_DIM), lambda b, h, i: (b, i, h)),
        out_shape=jax.ShapeDtypeStruct((B, S, NA_HEADS * NA_HEAD_DIM), BF16),
        compiler_params=_params(3),
        name="na_attn",
    )(qT, k, vT, bias)


def _na_bias_tables(rpb):
    p = np.arange(3)[:, None]
    qi = np.arange(NA_QROWS)[None, :]
    qr = NA_QROWS * p + qi
    r_start = np.clip(qr - WIN_R // 2, 0, NA_BAND - WIN_R)
    kr = np.arange(NA_BAND)[None, :, None]
    valid_r = (kr >= r_start[:, None, :]) & (kr < r_start[:, None, :] + WIN_R)
    dr = np.clip(kr - qr[:, None, :] + WIN_R - 1, 0, 2 * WIN_R - 2)
    c = np.arange(GRID_W)
    c_start = np.clip(c - WIN_C // 2, 0, GRID_W - WIN_C)
    kc = c[:, None]
    valid_c = (kc >= c_start[None, :]) & (kc < c_start[None, :] + WIN_C)
    dc = np.clip(kc - c[None, :] + WIN_C - 1, 0, 2 * WIN_C - 2)
    valid = valid_r[:, :, None, :, None] & valid_c[None, None, :, None, :]
    row_sel = (dr.reshape(-1)[:, None] == np.arange(2 * WIN_R - 1)[None, :]).astype(np.float32)
    col_sel = (dc[None] == np.arange(2 * WIN_C - 1)[:, None, None]).astype(np.float32)
    rows = jnp.sum((rpb.astype(F32) * LOG2E)[:, None, :, :] * row_sel[None, :, :, None], axis=2)
    rows = rows.reshape(NA_HEADS * 3 * NA_BAND, NA_QROWS * (2 * WIN_C - 1))
    col_sel4 = (np.eye(NA_QROWS, dtype=np.float32)[:, None, None, :, None] * col_sel[None, :, :, None, :]
                ).reshape(NA_QROWS * (2 * WIN_C - 1), GRID_W, NA_QROWS * GRID_W)
    bias = jnp.einsum("nj,jkq->nkq", rows, col_sel4, precision=lax.Precision.HIGHEST)
    bias = bias.reshape(NA_HEADS, 3, NA_BAND * GRID_W, NA_QROWS * GRID_W)
    valid = valid.reshape(3, NA_BAND * GRID_W, NA_QROWS * GRID_W)
    return jnp.where(valid[None], bias, NEG_INF)


def _post_ffn_kernel(x_ref, a_ref, mod_ref, gains_ref, wo_ref, wgu_ref, wd_ref, o_ref):
    gate_m = mod_ref[0, 2:3, :]
    shift_f = mod_ref[0, 3:4, :]
    scale_f = mod_ref[0, 4:5, :]
    gate_f = mod_ref[0, 5:6, :]
    n_sub = x_ref.shape[1] // FFN_SUBTILE
    rows = [slice(r * FFN_SUBTILE, (r + 1) * FFN_SUBTILE) for r in range(n_sub)]
    mix = [_dot(a_ref[0, r, :], wo_ref[...]) for r in rows]
    xs, gu, down = [], [], []
    for i in range(n_sub + 2):
        if i < n_sub:
            x = x_ref[0, rows[i], :] + gate_m * _rms(mix[i], gains_ref[1:2, :])
            h = (_rms(x, gains_ref[2:3, :]) * (1.0 + scale_f) + shift_f).astype(BF16)
            xs.append(x)
            gu.append((_dot(h, wgu_ref[:, 0:FFN_HIDDEN]), _dot(h, wgu_ref[:, FFN_HIDDEN:2 * FFN_HIDDEN])))
        if 1 <= i <= n_sub:
            g, u = gu[i - 1]
            act = ((g / (1.0 + jnp.exp(-g))) * u).astype(BF16)
            down.append(_dot(act, wd_ref[...]))
        if i >= 2:
            r = i - 2
            o_ref[0, rows[r], :] = xs[r] + gate_f * _rms(down[r], gains_ref[3:4, :])


def _post_ffn(x, attn, mod, gains, wo, wgu, wd):
    B, S, _ = x.shape
    tm = TOKEN_TILE
    return pl.pallas_call(
        _post_ffn_kernel,
        grid=(B, S // tm),
        in_specs=[
            pl.BlockSpec((1, tm, D_MODEL), lambda b, i: (b, i, 0)),
            pl.BlockSpec((1, tm, D_MODEL), lambda b, i: (b, i, 0)),
            pl.BlockSpec((1, N_MOD, D_MODEL), lambda b, i: (b, 0, 0)),
            _const_spec((4, D_MODEL)),
            _const_spec((D_MODEL, D_MODEL)),
            _const_spec((D_MODEL, 2 * FFN_HIDDEN)),
            _const_spec((FFN_HIDDEN, D_MODEL)),
        ],
        out_specs=pl.BlockSpec((1, tm, D_MODEL), lambda b, i: (b, i, 0)),
        out_shape=jax.ShapeDtypeStruct((B, S, D_MODEL), F32),
        compiler_params=_params(2),
        name="post_ffn",
    )(x, attn, mod, gains, wo, wgu, wd)


def _ones_column(n_heads):
    col = np.zeros((n_heads, V_ROWS, 1), np.float32)
    col[:, V_DIM, 0] = 1.0
    return jnp.asarray(col.reshape(n_heads * V_ROWS, 1))


def _prep_mla(w_dkv, q_norm, kv_norm, w_uq, w_ukv):
    half = QK_ROPE // 2
    r0 = Q_LORA + KV_LORA
    wdkv = jnp.zeros((D_MODEL, LAT_COLS), F32)
    wdkv = wdkv.at[:, :r0 + QK_ROPE].set(w_dkv)
    wdkv = wdkv.at[:, KR_SWAP_COL:KR_SWAP_COL + half].set(-w_dkv[:, r0 + half:r0 + QK_ROPE])
    wdkv = wdkv.at[:, KR_SWAP_COL + half:KR_SWAP_COL + QK_ROPE].set(w_dkv[:, r0:r0 + half])
    wq = w_uq.reshape(Q_LORA, MLA_HEADS, QK_NOPE + QK_ROPE)
    wq = jnp.pad(wq, ((0, 0), (0, 0), (0, HEAD_PAD - QK_NOPE - QK_ROPE)))
    wqT = wq.reshape(Q_LORA, MLA_HEADS * HEAD_PAD).T
    wkv = w_ukv.reshape(KV_LORA, MLA_HEADS, QK_NOPE + V_DIM)
    wk_nope = jnp.pad(wkv[:, :, :QK_NOPE], ((0, 0), (0, 0), (0, HEAD_PAD - QK_NOPE)))
    eye = np.zeros((KV_LORA, MLA_HEADS, HEAD_PAD), np.float32)
    for r in range(QK_ROPE):
        eye[r, :, QK_NOPE + r] = 1.0
    wk = jnp.concatenate([wk_nope, jnp.asarray(eye)], axis=0).reshape(2 * KV_LORA, MLA_HEADS * HEAD_PAD)
    wv = jnp.pad(wkv[:, :, QK_NOPE:], ((0, 0), (0, 0), (0, V_ROWS - V_DIM)))
    wvT = wv.reshape(KV_LORA, MLA_HEADS * V_ROWS).T
    return dict(wdkv=wdkv.astype(BF16), qn=q_norm.reshape(1, Q_LORA), kvn=kv_norm.reshape(1, KV_LORA),
                wqT=wqT.astype(BF16), wk=wk.astype(BF16), wvT=wvT.astype(BF16), ones=_ones_column(MLA_HEADS))


def _prep_na(w_qkv):
    hk = NA_HEADS * NA_HEAD_DIM
    wv = w_qkv[:, 2 * hk:].reshape(D_MODEL, NA_HEADS, NA_HEAD_DIM)
    wv = jnp.pad(wv, ((0, 0), (0, 0), (0, V_ROWS - NA_HEAD_DIM)))
    return dict(wqT=w_qkv[:, :hk].T.astype(BF16), wk=w_qkv[:, hk:2 * hk].astype(BF16),
                wvT=wv.reshape(D_MODEL, NA_HEADS * V_ROWS).T.astype(BF16), ones=_ones_column(NA_HEADS))


def _rope_tables(seq_len):
    half = QK_ROPE // 2
    inv_freq = 1.0 / (ROPE_THETA ** (jnp.arange(0, QK_ROPE, 2, dtype=F32) / QK_ROPE))
    ang = jnp.arange(seq_len, dtype=F32)[:, None] * inv_freq[None, :]
    cos, sin = jnp.cos(ang), jnp.sin(ang)
    pad = jnp.zeros((seq_len, LANES - 2 * half), F32)
    ck = jnp.concatenate([cos, cos, pad], axis=1)
    sk = jnp.concatenate([sin, sin, pad], axis=1)
    return ck, sk, cos.T, sin.T


def _trunk(x, mods, gains, mla_w, na_w, na_bias, post_w):
    tables = _rope_tables(x.shape[1])
    for i in range(DEPTH):
        if i % 2 == 0:
            qT, k, vT = _mla_pre(x, mods[i], gains[i], mla_w[i // 2], tables)
            attn = _mla_attn(qT, k, vT)
        else:
            qT, k, vT = _na_pre(x, mods[i], gains[i], na_w[i // 2])
            attn = _na_attn(qT, k, vT, na_bias[i // 2])
        x = _post_ffn(x, attn, mods[i], gains[i], *post_w[i])
    return x


def kernel(x_prompt, x_sample, c_prompt, c_sample, ada_w, ada_b, norm_pre_mix, norm_post_mix, norm_pre_ffn, norm_post_ffn, mla_w_dkv, mla_q_norm, mla_kv_norm, mla_w_uq, mla_w_ukv, mla_w_o, na_w_qkv, na_rpb, na_w_o, ffn_w_gu, ffn_w_down):
    bp = x_prompt.shape[0]
    bs = x_sample.shape[0]
    mod = _ada_mod(jnp.concatenate([c_prompt, c_sample], axis=0), ada_w, ada_b)
    mod = mod.reshape(DEPTH, bp + bs, N_MOD, D_MODEL)
    gains = [jnp.stack([norm_pre_mix[i], norm_post_mix[i], norm_pre_ffn[i], norm_post_ffn[i]]) for i in range(DEPTH)]
    mla_w = [_prep_mla(mla_w_dkv[j], mla_q_norm[j], mla_kv_norm[j], mla_w_uq[j], mla_w_ukv[j])
             for j in range(mla_w_dkv.shape[0])]
    na_w = [_prep_na(na_w_qkv[j]) for j in range(na_w_qkv.shape[0])]
    na_bias = [_na_bias_tables(na_rpb[j]) for j in range(na_rpb.shape[0])]
    post_w = []
    for i in range(DEPTH):
        wo = mla_w_o[i // 2] if i % 2 == 0 else na_w_o[i // 2]
        post_w.append((wo.astype(BF16), ffn_w_gu[i].astype(BF16), ffn_w_down[i].astype(BF16)))
    y_prompt = _trunk(x_prompt, [mod[i, :bp] for i in range(DEPTH)], gains, mla_w, na_w, na_bias, post_w)
    y_sample = _trunk(x_sample, [mod[i, bp:] for i in range(DEPTH)], gains, mla_w, na_w, na_bias, post_w)
    return (y_prompt, y_sample)
```

```python
import functools

import jax
import jax.numpy as jnp
import numpy as np
from jax import lax
from jax.experimental import pallas as pl
from jax.experimental.pallas import tpu as pltpu

F32 = jnp.float32
BF16 = jnp.bfloat16

D_MODEL = 1024
DEPTH = 2
N_MOD = 6
RMS_EPS = 1e-6
NEG_INF = -1e30

MLA_HEADS = 16
Q_LORA = 256
KV_LORA = 128
QK_NOPE = 64
QK_ROPE = 32
V_DIM = 64
ROPE_THETA = 10000.0
MLA_SCALE = (QK_NOPE + QK_ROPE) ** -0.5
LOG2E = 1.4426950408889634
HEAD_PAD = 128
V_ROWS = 80
LANES = 128
KR_COL = Q_LORA + KV_LORA
KR_SWAP_COL = KR_COL + LANES
LAT_COLS = KR_SWAP_COL + LANES

NA_HEADS = 16
NA_HEAD_DIM = 64
GRID_W = 64
WIN_R = 8
WIN_C = 16
NA_QROWS = 4
NA_BAND = 12
NA_SCALE = NA_HEAD_DIM ** -0.5
NA_GROUP = 16
NA_LOOKAHEAD = 4

FFN_HIDDEN = 2816
FFN_SUBTILE = 256

TOKEN_TILE = 512
PRE_TILE = 1024
MLA_TQ = 256
MLA_QBLOCKS = 8
MLA_TK = 256
MLA_LOOKAHEAD = 3
NA_CHUNK = 256

ADA_TILE_N = 1536

V7X_VMEM_BYTES = 64 * 1024 * 1024
VMEM_LIMIT = V7X_VMEM_BYTES * 7 // 8

_NT = (((1,), (1,)), ((), ()))


def _dot(a, b):
    return jnp.dot(a, b, preferred_element_type=F32)


def _dot_nt(a, b):
    return lax.dot_general(a, b, _NT, preferred_element_type=F32)


def _rms(x, g):
    ms = jnp.mean(x * x, axis=-1, keepdims=True)
    return x * lax.rsqrt(ms + RMS_EPS) * g


def _const_spec(shape):
    zeros = (0,) * len(shape)
    return pl.BlockSpec(shape, lambda *_: zeros, pipeline_mode=pl.Buffered(1))


def _params(n_axes):
    return pltpu.CompilerParams(
        dimension_semantics=("arbitrary",) * n_axes, vmem_limit_bytes=VMEM_LIMIT)


def _ada_kernel(c_ref, w_ref, b_ref, o_ref):
    c = c_ref[...]
    c_act = c / (1.0 + jnp.exp(-c))
    o_ref[0] = _dot(c_act.astype(BF16), w_ref[0]) + b_ref[0]


def _ada_mod(c_all, ada_w, ada_b):
    n_rows = c_all.shape[0]
    n_out = N_MOD * D_MODEL
    tn = ADA_TILE_N
    return pl.pallas_call(
        _ada_kernel,
        grid=(DEPTH, n_out // tn),
        in_specs=[
            pl.BlockSpec((n_rows, D_MODEL), lambda i, j: (0, 0)),
            pl.BlockSpec((1, D_MODEL, tn), lambda i, j: (i, 0, j)),
            pl.BlockSpec((1, 1, tn), lambda i, j: (i, 0, j)),
        ],
        out_specs=pl.BlockSpec((1, n_rows, tn), lambda i, j: (i, 0, j)),
        out_shape=jax.ShapeDtypeStruct((DEPTH, n_rows, n_out), F32),
        compiler_params=_params(2),
        name="ada_mod",
    )(c_all, ada_w.astype(BF16), ada_b.reshape(DEPTH, 1, n_out))


def _mla_pre_kernel(x_ref, mod_ref, gains_ref, wdkv_ref, qn_ref, kvn_ref, wqT_ref, wk_ref,
                    wvT_ref, ones_ref, ck_ref, sk_ref, cT_ref, sT_ref, qT_out, k_out, vT_out):
    shift = mod_ref[0, 0:1, :]
    scale = mod_ref[0, 1:2, :]
    half = QK_ROPE // 2
    subs = [slice(c * MLA_TK, (c + 1) * MLA_TK) for c in range(x_ref.shape[1] // MLA_TK)]
    lats = []
    for r in subs:
        h = (_rms(x_ref[0, r, :], gains_ref[0:1, :]) * (1.0 + scale) + shift).astype(BF16)
        lats.append(_dot(h, wdkv_ref[...]))
    for c, (r, lat) in enumerate(zip(subs, lats)):
        cq = _rms(lat[:, 0:Q_LORA], qn_ref[...]).astype(BF16)
        ckv = _rms(lat[:, Q_LORA:Q_LORA + KV_LORA], kvn_ref[...]).astype(BF16)
        kr = (lat[:, KR_COL:KR_COL + LANES] * ck_ref[r, :]
              + lat[:, KR_SWAP_COL:KR_SWAP_COL + LANES] * sk_ref[r, :])
        kin = jnp.concatenate([ckv, kr.astype(BF16)], axis=1)
        k_out[0, r, :] = _dot(kin, wk_ref[...]).astype(BF16)

        qT = _dot_nt(wqT_ref[...], cq) * (MLA_SCALE * LOG2E)
        cT = cT_ref[:, r]
        sT = sT_ref[:, r]
        zpad = jnp.zeros((HEAD_PAD - QK_NOPE - QK_ROPE, qT.shape[1]), BF16)
        for hd in range(MLA_HEADS):
            b0 = hd * HEAD_PAD
            x1 = qT[b0 + QK_NOPE:b0 + QK_NOPE + half]
            x2 = qT[b0 + QK_NOPE + half:b0 + QK_NOPE + QK_ROPE]
            qT_out[0, b0:b0 + QK_NOPE, r] = qT[b0:b0 + QK_NOPE].astype(BF16)
            qT_out[0, b0 + QK_NOPE:b0 + QK_NOPE + half, r] = (x1 * cT - x2 * sT).astype(BF16)
            qT_out[0, b0 + QK_NOPE + half:b0 + QK_NOPE + QK_ROPE, r] = (x2 * cT + x1 * sT).astype(BF16)
            qT_out[0, b0 + QK_NOPE + QK_ROPE:b0 + HEAD_PAD, r] = zpad

        vT = _dot_nt(wvT_ref[...], ckv) + ones_ref[...]
        vT_out[0, c] = vT.astype(BF16)


def _mla_pre(x, mod, gains, w, tables):
    B, S, _ = x.shape
    tm = PRE_TILE if S % PRE_TILE == 0 else TOKEN_TILE
    n_t = S // tm
    cpt = tm // MLA_TK
    ck, sk, cT, sT = tables
    hq = MLA_HEADS * HEAD_PAD
    hv = MLA_HEADS * V_ROWS
    return pl.pallas_call(
        _mla_pre_kernel,
        grid=(B, n_t),
        in_specs=[
            pl.BlockSpec((1, tm, D_MODEL), lambda b, i: (b, i, 0)),
            pl.BlockSpec((1, N_MOD, D_MODEL), lambda b, i: (b, 0, 0)),
            _const_spec((4, D_MODEL)),
            _const_spec((D_MODEL, LAT_COLS)),
            _const_spec((1, Q_LORA)),
            _const_spec((1, KV_LORA)),
            _const_spec((hq, Q_LORA)),
            _const_spec((2 * KV_LORA, hq)),
            _const_spec((hv, KV_LORA)),
            _const_spec((hv, 1)),
            pl.BlockSpec((tm, LANES), lambda b, i: (i, 0)),
            pl.BlockSpec((tm, LANES), lambda b, i: (i, 0)),
            pl.BlockSpec((QK_ROPE // 2, tm), lambda b, i: (0, i)),
            pl.BlockSpec((QK_ROPE // 2, tm), lambda b, i: (0, i)),
        ],
        out_specs=[
            pl.BlockSpec((1, hq, tm), lambda b, i: (b, 0, i)),
            pl.BlockSpec((1, tm, hq), lambda b, i: (b, i, 0)),
            pl.BlockSpec((1, cpt, hv, MLA_TK), lambda b, i: (b, i, 0, 0)),
        ],
        out_shape=[
            jax.ShapeDtypeStruct((B, hq, S), BF16),
            jax.ShapeDtypeStruct((B, S, hq), BF16),
            jax.ShapeDtypeStruct((B, S // MLA_TK, hv, MLA_TK), BF16),
        ],
        compiler_params=_params(2),
        name="mla_pre",
    )(x, mod, gains, w["wdkv"], w["qn"], w["kvn"], w["wqT"], w["wk"], w["wvT"], w["ones"],
      ck, sk, cT, sT)


def _mla_attn_kernel(qT_ref, k_ref, vT_ref, o_ref, *, n_chunks):
    tq = MLA_TQ
    n_qb = qT_ref.shape[2] // tq

    def scores(qb, j, hh):
        k = k_ref[0, j * MLA_TK:(j + 1) * MLA_TK, hh * HEAD_PAD:(hh + 1) * HEAD_PAD]
        return _dot(k, qT_ref[0, hh * HEAD_PAD:(hh + 1) * HEAD_PAD, qb * tq:(qb + 1) * tq])

    steps = [(qb, j) for qb in range(n_qb) for j in range(n_chunks)]
    pending = {}
    for qb, j in steps[:MLA_LOOKAHEAD]:
        for hh in range(2):
            pending[qb, j, hh] = scores(qb, j, hh)
    res = {}
    for i, (qb, j) in enumerate(steps):
        for hh in range(2):
            if i + MLA_LOOKAHEAD < len(steps):
                nqb, nj = steps[i + MLA_LOOKAHEAD]
                pending[nqb, nj, hh] = scores(nqb, nj, hh)
            s = pending.pop((qb, j, hh))
            if j == 0:
                m, acc = jnp.full((1, tq), NEG_INF, F32), jnp.zeros((V_ROWS, tq), F32)
            else:
                m, acc = res[hh]
            m_new = jnp.maximum(m, jnp.max(s, axis=0, keepdims=True))
            alpha = jnp.exp2(m - m_new)
            p = jnp.exp2(s - m_new).astype(BF16)
            vT = vT_ref[0, j, hh * V_ROWS:(hh + 1) * V_ROWS, :]
            res[hh] = (m_new, alpha * acc + _dot(vT, p))
        if j == n_chunks - 1:
            oT = jnp.concatenate([res[hh][1][0:V_DIM] / res[hh][1][V_DIM:V_DIM + 1] for hh in range(2)], axis=0)
            o_ref[0, qb * tq:(qb + 1) * tq, :] = oT.T.astype(BF16)


def _mla_attn(qT, k, vT):
    B, _, S = qT.shape
    n_chunks = S // MLA_TK
    hp = MLA_HEADS // 2
    n_qb = max(g for g in range(1, MLA_QBLOCKS + 1) if (S // MLA_TQ) % g == 0)
    tq_step = MLA_TQ * n_qb
    return pl.pallas_call(
        functools.partial(_mla_attn_kernel, n_chunks=n_chunks),
        grid=(B, hp, S // tq_step),
        in_specs=[
            pl.BlockSpec((1, 2 * HEAD_PAD, tq_step), lambda b, h, i: (b, h, i)),
            pl.BlockSpec((1, S, 2 * HEAD_PAD), lambda b, h, i: (b, 0, h)),
            pl.BlockSpec((1, n_chunks, 2 * V_ROWS, MLA_TK), lambda b, h, i: (b, 0, h, 0)),
        ],
        out_specs=pl.BlockSpec((1, tq_step, 2 * V_DIM), lambda b, h, i: (b, i, h)),
        out_shape=jax.ShapeDtypeStruct((B, S, MLA_HEADS * V_DIM), BF16),
        compiler_params=_params(3),
        name="mla_attn",
    )(qT, k, vT)


def _na_pre_kernel(x_ref, mod_ref, gains_ref, wqT_ref, wk_ref, wvT_ref, ones_ref,
                   qT_out, k_out, vT_out):
    shift = mod_ref[0, 0:1, :]
    scale = mod_ref[0, 1:2, :]
    for c in range(x_ref.shape[1] // NA_CHUNK):
        r = slice(c * NA_CHUNK, (c + 1) * NA_CHUNK)
        h = (_rms(x_ref[0, r, :], gains_ref[0:1, :]) * (1.0 + scale) + shift).astype(BF16)
        k_out[0, r, :] = _dot(h, wk_ref[...]).astype(BF16)
        qT = _dot_nt(wqT_ref[...], h) * (NA_SCALE * LOG2E)
        zpad = jnp.zeros((NA_HEAD_DIM, qT.shape[1]), BF16)
        for hd in range(NA_HEADS):
            lo = hd * HEAD_PAD + (hd % 2) * NA_HEAD_DIM
            zo = hd * HEAD_PAD + (1 - hd % 2) * NA_HEAD_DIM
            qT_out[0, lo:lo + NA_HEAD_DIM, r] = qT[hd * NA_HEAD_DIM:(hd + 1) * NA_HEAD_DIM].astype(BF16)
            qT_out[0, zo:zo + NA_HEAD_DIM, r] = zpad
        vT = _dot_nt(wvT_ref[...], h) + ones_ref[...]
        vT_out[0, c] = vT.astype(BF16)


def _na_pre(x, mod, gains, w):
    B, S, _ = x.shape
    tm = PRE_TILE if S % PRE_TILE == 0 else TOKEN_TILE
    cpt = tm // NA_CHUNK
    hq = NA_HEADS * HEAD_PAD
    hk = NA_HEADS * NA_HEAD_DIM
    hv = NA_HEADS * V_ROWS
    return pl.pallas_call(
        _na_pre_kernel,
        grid=(B, S // tm),
        in_specs=[
            pl.BlockSpec((1, tm, D_MODEL), lambda b, i: (b, i, 0)),
            pl.BlockSpec((1, N_MOD, D_MODEL), lambda b, i: (b, 0, 0)),
            _const_spec((4, D_MODEL)),
            _const_spec((hk, D_MODEL)),
            _const_spec((D_MODEL, hk)),
            _const_spec((hv, D_MODEL)),
            _const_spec((hv, 1)),
        ],
        out_specs=[
            pl.BlockSpec((1, hq, tm), lambda b, i: (b, 0, i)),
            pl.BlockSpec((1, tm, hk), lambda b, i: (b, i, 0)),
            pl.BlockSpec((1, cpt, hv, NA_CHUNK), lambda b, i: (b, i, 0, 0)),
        ],
        out_shape=[
            jax.ShapeDtypeStruct((B, hq, S), BF16),
            jax.ShapeDtypeStruct((B, S, hk), BF16),
            jax.ShapeDtypeStruct((B, S // NA_CHUNK, hv, NA_CHUNK), BF16),
        ],
        compiler_params=_params(2),
        name="na_pre",
    )(x, mod, gains, w["wqT"], w["wk"], w["wvT"], w["ones"])


def _na_band_start(blk, n_blk):
    return jnp.clip(blk - 1, 0, n_blk - NA_BAND // NA_QROWS)


def _na_attn_kernel(qT_ref, k_ref, vT_ref, bias_ref, o_ref, *, n_blk, group):
    step = pl.program_id(2)
    tq = NA_QROWS * GRID_W
    n_keys = NA_BAND * GRID_W
    n_chunks = n_keys // NA_CHUNK
    band0 = [_na_band_start(step * group + g, n_blk) for g in range(group)]
    pattern = [jnp.where(step * group + g == 0, 0, jnp.where(step * group + g == n_blk - 1, 2, 1))
               for g in range(group)]

    def scores(g, c, hh):
        kc = k_ref[0, pl.ds(pl.multiple_of((band0[g] + c) * NA_CHUNK, NA_CHUNK), NA_CHUNK), :]
        qT = qT_ref[0, hh * HEAD_PAD:(hh + 1) * HEAD_PAD, g * tq:(g + 1) * tq]
        return _dot(kc, qT) + bias_ref[hh, pattern[g], c * NA_CHUNK:(c + 1) * NA_CHUNK, :]

    steps = [(g, c) for g in range(group) for c in range(n_chunks)]
    pending = {}
    for g, c in steps[:NA_LOOKAHEAD]:
        for hh in range(2):
            pending[g, c, hh] = scores(g, c, hh)
    res = {}
    for i, (g, c) in enumerate(steps):
        for hh in range(2):
            if i + NA_LOOKAHEAD < len(steps):
                ng, nc = steps[i + NA_LOOKAHEAD]
                pending[ng, nc, hh] = scores(ng, nc, hh)
            s = pending.pop((g, c, hh))
            if c == 0:
                m, acc = jnp.full((1, tq), NEG_INF, F32), jnp.zeros((V_ROWS, tq), F32)
            else:
                m, acc = res[hh]
            m_new = jnp.maximum(m, jnp.max(s, axis=0, keepdims=True))
            alpha = jnp.exp2(m - m_new)
            p = jnp.exp2((s - m_new).astype(BF16))
            vT = vT_ref[0, band0[g] + c, hh * V_ROWS:(hh + 1) * V_ROWS, :]
            res[hh] = (m_new, alpha * acc + _dot(vT, p))
        if c == n_chunks - 1:
            pair = jnp.concatenate([res[hh][1][0:NA_HEAD_DIM] / res[hh][1][NA_HEAD_DIM:NA_HEAD_DIM + 1]
                                    for hh in range(2)], axis=0)
            o_ref[0, g * tq:(g + 1) * tq, :] = pair.T.astype(BF16)


def _na_attn(qT, k, vT, bias):
    B, _, S = qT.shape
    tq = NA_QROWS * GRID_W
    n_blk = S // tq
    hp = NA_HEADS // 2
    group = max(g for g in range(1, NA_GROUP + 1) if n_blk % g == 0)
    return pl.pallas_call(
        functools.partial(_na_attn_kernel, n_blk=n_blk, group=group),
        grid=(B, hp, n_blk // group),
        in_specs=[
            pl.BlockSpec((1, 2 * HEAD_PAD, group * tq), lambda b, h, i: (b, h, i)),
            pl.BlockSpec((1, S, 2 * NA_HEAD_DIM), lambda b, h, i: (b, 0, h)),
            pl.BlockSpec((1, S // NA_CHUNK, 2 * V_ROWS, NA_CHUNK), lambda b, h, i: (b, 0, h, 0)),
            pl.BlockSpec((2, 3, NA_BAND * GRID_W, tq), lambda b, h, i: (h, 0, 0, 0)),
        ],
        out_specs=pl.BlockSpec((1, group * tq, 2 * NA_HEAD_DIM), lambda b, h, i: (b, i, h)),
        out_shape=jax.ShapeDtypeStruct((B, S, NA_HEADS * NA_HEAD_DIM), BF16),
        compiler_params=_params(3),
        name="na_attn",
    )(qT, k, vT, bias)


def _na_bias_tables(rpb):
    p = np.arange(3)[:, None]
    qi = np.arange(NA_QROWS)[None, :]
    qr = NA_QROWS * p + qi
    r_start = np.clip(qr - WIN_R // 2, 0, NA_BAND - WIN_R)
    kr = np.arange(NA_BAND)[None, :, None]
    valid_r = (kr >= r_start[:, None, :]) & (kr < r_start[:, None, :] + WIN_R)
    dr = np.clip(kr - qr[:, None, :] + WIN_R - 1, 0, 2 * WIN_R - 2)
    c = np.arange(GRID_W)
    c_start = np.clip(c - WIN_C // 2, 0, GRID_W - WIN_C)
    kc = c[:, None]
    valid_c = (kc >= c_start[None, :]) & (kc < c_start[None, :] + WIN_C)
    dc = np.clip(kc - c[None, :] + WIN_C - 1, 0, 2 * WIN_C - 2)
    valid = valid_r[:, :, None, :, None] & valid_c[None, None, :, None, :]
    row_sel = (dr.reshape(-1)[:, None] == np.arange(2 * WIN_R - 1)[None, :]).astype(np.float32)
    col_sel = (dc[None] == np.arange(2 * WIN_C - 1)[:, None, None]).astype(np.float32)
    rows = jnp.sum((rpb.astype(F32) * LOG2E)[:, None, :, :] * row_sel[None, :, :, None], axis=2)
    rows = rows.reshape(NA_HEADS * 3 * NA_BAND, NA_QROWS * (2 * WIN_C - 1))
    col_sel4 = (np.eye(NA_QROWS, dtype=np.float32)[:, None, None, :, None] * col_sel[None, :, :, None, :]
                ).reshape(NA_QROWS * (2 * WIN_C - 1), GRID_W, NA_QROWS * GRID_W)
    bias = jnp.einsum("nj,jkq->nkq", rows, col_sel4, precision=lax.Precision.HIGHEST)
    bias = bias.reshape(NA_HEADS, 3, NA_BAND * GRID_W, NA_QROWS * GRID_W)
    valid = valid.reshape(3, NA_BAND * GRID_W, NA_QROWS * GRID_W)
    return jnp.where(valid[None], bias, NEG_INF)


def _post_ffn_kernel(x_ref, a_ref, mod_ref, gains_ref, wo_ref, wgu_ref, wd_ref, o_ref):
    gate_m = mod_ref[0, 2:3, :]
    shift_f = mod_ref[0, 3:4, :]
    scale_f = mod_ref[0, 4:5, :]
    gate_f = mod_ref[0, 5:6, :]
    n_sub = x_ref.shape[1] // FFN_SUBTILE
    rows = [slice(r * FFN_SUBTILE, (r + 1) * FFN_SUBTILE) for r in range(n_sub)]
    mix = [_dot(a_ref[0, r, :], wo_ref[...]) for r in rows]
    xs, gu, down = [], [], []
    for i in range(n_sub + 2):
        if i < n_sub:
            x = x_ref[0, rows[i], :] + gate_m * _rms(mix[i], gains_ref[1:2, :])
            h = (_rms(x, gains_ref[2:3, :]) * (1.0 + scale_f) + shift_f).astype(BF16)
            xs.append(x)
            gu.append((_dot(h, wgu_ref[:, 0:FFN_HIDDEN]), _dot(h, wgu_ref[:, FFN_HIDDEN:2 * FFN_HIDDEN])))
        if 1 <= i <= n_sub:
            g, u = gu[i - 1]
            act = ((g / (1.0 + jnp.exp(-g))) * u).astype(BF16)
            down.append(_dot(act, wd_ref[...]))
        if i >= 2:
            r = i - 2
            o_ref[0, rows[r], :] = xs[r] + gate_f * _rms(down[r], gains_ref[3:4, :])


def _post_ffn(x, attn, mod, gains, wo, wgu, wd):
    B, S, _ = x.shape
    tm = TOKEN_TILE
    return pl.pallas_call(
        _post_ffn_kernel,
        grid=(B, S // tm),
        in_specs=[
            pl.BlockSpec((1, tm, D_MODEL), lambda b, i: (b, i, 0)),
            pl.BlockSpec((1, tm, D_MODEL), lambda b, i: (b, i, 0)),
            pl.BlockSpec((1, N_MOD, D_MODEL), lambda b, i: (b, 0, 0)),
            _const_spec((4, D_MODEL)),
            _const_spec((D_MODEL, D_MODEL)),
            _const_spec((D_MODEL, 2 * FFN_HIDDEN)),
            _const_spec((FFN_HIDDEN, D_MODEL)),
        ],
        out_specs=pl.BlockSpec((1, tm, D_MODEL), lambda b, i: (b, i, 0)),
        out_shape=jax.ShapeDtypeStruct((B, S, D_MODEL), F32),
        compiler_params=_params(2),
        name="post_ffn",
    )(x, attn, mod, gains, wo, wgu, wd)


def _ones_column(n_heads):
    col = np.zeros((n_heads, V_ROWS, 1), np.float32)
    col[:, V_DIM, 0] = 1.0
    return jnp.asarray(col.reshape(n_heads * V_ROWS, 1))


def _prep_mla(w_dkv, q_norm, kv_norm, w_uq, w_ukv):
    half = QK_ROPE // 2
    r0 = Q_LORA + KV_LORA
    wdkv = jnp.zeros((D_MODEL, LAT_COLS), F32)
    wdkv = wdkv.at[:, :r0 + QK_ROPE].set(w_dkv)
    wdkv = wdkv.at[:, KR_SWAP_COL:KR_SWAP_COL + half].set(-w_dkv[:, r0 + half:r0 + QK_ROPE])
    wdkv = wdkv.at[:, KR_SWAP_COL + half:KR_SWAP_COL + QK_ROPE].set(w_dkv[:, r0:r0 + half])
    wq = w_uq.reshape(Q_LORA, MLA_HEADS, QK_NOPE + QK_ROPE)
    wq = jnp.pad(wq, ((0, 0), (0, 0), (0, HEAD_PAD - QK_NOPE - QK_ROPE)))
    wqT = wq.reshape(Q_LORA, MLA_HEADS * HEAD_PAD).T
    wkv = w_ukv.reshape(KV_LORA, MLA_HEADS, QK_NOPE + V_DIM)
    wk_nope = jnp.pad(wkv[:, :, :QK_NOPE], ((0, 0), (0, 0), (0, HEAD_PAD - QK_NOPE)))
    eye = np.zeros((KV_LORA, MLA_HEADS, HEAD_PAD), np.float32)
    for r in range(QK_ROPE):
        eye[r, :, QK_NOPE + r] = 1.0
    wk = jnp.concatenate([wk_nope, jnp.asarray(eye)], axis=0).reshape(2 * KV_LORA, MLA_HEADS * HEAD_PAD)
    wv = jnp.pad(wkv[:, :, QK_NOPE:], ((0, 0), (0, 0), (0, V_ROWS - V_DIM)))
    wvT = wv.reshape(KV_LORA, MLA_HEADS * V_ROWS).T
    return dict(wdkv=wdkv.astype(BF16), qn=q_norm.reshape(1, Q_LORA), kvn=kv_norm.reshape(1, KV_LORA),
                wqT=wqT.astype(BF16), wk=wk.astype(BF16), wvT=wvT.astype(BF16), ones=_ones_column(MLA_HEADS))


def _prep_na(w_qkv):
    hk = NA_HEADS * NA_HEAD_DIM
    wv = w_qkv[:, 2 * hk:].reshape(D_MODEL, NA_HEADS, NA_HEAD_DIM)
    wv = jnp.pad(wv, ((0, 0), (0, 0), (0, V_ROWS - NA_HEAD_DIM)))
    return dict(wqT=w_qkv[:, :hk].T.astype(BF16), wk=w_qkv[:, hk:2 * hk].astype(BF16),
                wvT=wv.reshape(D_MODEL, NA_HEADS * V_ROWS).T.astype(BF16), ones=_ones_column(NA_HEADS))


def _rope_tables(seq_len):
    half = QK_ROPE // 2
    inv_freq = 1.0 / (ROPE_THETA ** (jnp.arange(0, QK_ROPE, 2, dtype=F32) / QK_ROPE))
    ang = jnp.arange(seq_len, dtype=F32)[:, None] * inv_freq[None, :]
    cos, sin = jnp.cos(ang), jnp.sin(ang)
    pad = jnp.zeros((seq_len, LANES - 2 * half), F32)
    ck = jnp.concatenate([cos, cos, pad], axis=1)
    sk = jnp.concatenate([sin, sin, pad], axis=1)
    return ck, sk, cos.T, sin.T


def _trunk(x, mods, gains, mla_w, na_w, na_bias, post_w):
    tables = _rope_tables(x.shape[1])
    for i in range(DEPTH):
        if i % 2 == 0:
            qT, k, vT = _mla_pre(x, mods[i], gains[i], mla_w[i // 2], tables)
            attn = _mla_attn(qT, k, vT)
        else:
            qT, k, vT = _na_pre(x, mods[i], gains[i], na_w[i // 2])
            attn = _na_attn(qT, k, vT, na_bias[i // 2])
        x = _post_ffn(x, attn, mods[i], gains[i], *post_w[i])
    return x


def kernel(x_prompt, x_sample, c_prompt, c_sample, ada_w, ada_b, norm_pre_mix, norm_post_mix, norm_pre_ffn, norm_post_ffn, mla_w_dkv, mla_q_norm, mla_kv_norm, mla_w_uq, mla_w_ukv, mla_w_o, na_w_qkv, na_rpb, na_w_o, ffn_w_gu, ffn_w_down):
    bp = x_prompt.shape[0]
    bs = x_sample.shape[0]
    mod = _ada_mod(jnp.concatenate([c_prompt, c_sample], axis=0), ada_w, ada_b)
    mod = mod.reshape(DEPTH, bp + bs, N_MOD, D_MODEL)
    gains = [jnp.stack([norm_pre_mix[i], norm_post_mix[i], norm_pre_ffn[i], norm_post_ffn[i]]) for i in range(DEPTH)]
    mla_w = [_prep_mla(mla_w_dkv[j], mla_q_norm[j], mla_kv_norm[j], mla_w_uq[j], mla_w_ukv[j])
             for j in range(mla_w_dkv.shape[0])]
    na_w = [_prep_na(na_w_qkv[j]) for j in range(na_w_qkv.shape[0])]
    na_bias = [_na_bias_tables(na_rpb[j]) for j in range(na_rpb.shape[0])]
    post_w = []
    for i in range(DEPTH):
        wo = mla_w_o[i // 2] if i % 2 == 0 else na_w_o[i // 2]
        post_w.append((wo.astype(BF16), ffn_w_gu[i].astype(BF16), ffn_w_down[i].astype(BF16)))
    y_prompt = _trunk(x_prompt, [mod[i, :bp] for i in range(DEPTH)], gains, mla_w, na_w, na_bias, post_w)
    y_sample = _trunk(x_sample, [mod[i, bp:] for i in range(DEPTH)], gains, mla_w, na_w, na_bias, post_w)
    return (y_prompt, y_sample)
```
